```python
import jax, jax.numpy as jnp
from jax import lax
import numpy as np

D_MODEL = 1024
BATCH = 16
SEQ = 2048
DEPTH = 1

N_META = 16
CHUNK = 64
HG_HEADS = 4
HG_DK = 128
HG_DV = 128
GDN_HEADS = 4
GDN_DK = 128
GDN_DV = 128
CONV_W = 4
HG_WIDTH = HG_HEADS * HG_DK
GDN_WIDTH = GDN_HEADS * GDN_DV
D_MIX = HG_WIDTH + GDN_WIDTH
IN_COLS = 4 * HG_WIDTH + 4 * GDN_WIDTH + 2 * GDN_HEADS
EPS = 1e-6

kernel_name = "hymba_hgrn2_gated_deltanet_layer"


def rmsnorm(x, w):
    xf = x.astype(jnp.float32)
    y = xf * lax.rsqrt(jnp.mean(xf * xf, axis=-1, keepdims=True) + EPS)
    return (y * w.astype(jnp.float32)).astype(x.dtype)


def gated_rmsnorm(o, z, w):
    of = jnp.transpose(o, (0, 2, 1, 3)).astype(jnp.float32)
    y = of * lax.rsqrt(jnp.mean(of * of, axis=-1, keepdims=True) + EPS) * w.astype(jnp.float32)
    B, T = y.shape[0], y.shape[1]
    return (y.reshape(B, T, -1) * jax.nn.silu(z.astype(jnp.float32))).astype(z.dtype)


def heads(a, H):
    B, T, _ = a.shape
    return jnp.transpose(a.reshape(B, T, H, -1), (0, 2, 1, 3))


def causal_conv(x, w):
    C = x.shape[-1]
    return lax.conv_general_dilated(x, w[:, None, :].astype(x.dtype), window_strides=(1,),
                                    padding=((CONV_W - 1, 0),),
                                    dimension_numbers=('NWC', 'WIO', 'NWC'),
                                    feature_group_count=C)


def to_chunks(a, C):
    B, H, T = a.shape[:3]
    return jnp.moveaxis(a.reshape((B, H, T // C, C) + a.shape[3:]), 2, 0)


def from_chunks(o):
    nc, B, H, C, d = o.shape
    return jnp.moveaxis(o, 0, 2).reshape(B, H, nc * C, d)


def run_chunked(step, inputs, S0, C):
    xs = tuple(to_chunks(a, C) for a in inputs)
    S, o = lax.scan(step, S0, xs)
    return S, from_chunks(o)


def causal_mixer(step, inputs, S0):
    meta = tuple(a[:, :, :N_META] for a in inputs)
    real = tuple(a[:, :, N_META:] for a in inputs)
    S, o_meta = run_chunked(step, meta, S0, N_META)
    _, o_real = run_chunked(step, real, S, CHUNK)
    return jnp.concatenate([o_meta, o_real], axis=2)


def hgrn2_chunk(S, inp):
    q, k, v, logf = inp
    C = q.shape[2]
    G = jnp.cumsum(logf, axis=2)
    causal = jnp.tril(jnp.ones((C, C), dtype=bool))
    diff = G[:, :, :, None, :] - G[:, :, None, :, :]
    decay = jnp.exp(jnp.where(causal[None, None, :, :, None], diff, -jnp.inf))
    A = jnp.einsum('bhtd,bhtsd,bhsd->bhts', q, decay, k)
    o = jnp.einsum('bhts,bhsv->bhtv', A, v) + jnp.einsum('bhtd,bhdv->bhtv', q * jnp.exp(G), S)
    G_last = G[:, :, -1]
    S_new = jnp.exp(G_last)[..., None] * S + jnp.einsum(
        'bhsd,bhsv->bhdv', k * jnp.exp(G_last[:, :, None, :] - G), v)
    return S_new, o


def gdn_chunk(S, inp):
    q, k, v, g, beta = inp
    C = q.shape[2]
    dv = v.shape[-1]
    gam = jnp.cumsum(g, axis=-1)
    diff = gam[..., :, None] - gam[..., None, :]
    strict = jnp.tril(jnp.ones((C, C), dtype=bool), -1)
    incl = jnp.tril(jnp.ones((C, C), dtype=bool))
    dec_strict = jnp.exp(jnp.where(strict, diff, -jnp.inf))
    dec_incl = jnp.exp(jnp.where(incl, diff, -jnp.inf))
    A = beta[..., None] * jnp.einsum('bhtd,bhsd->bhts', k, k) * dec_strict
    lhs = jnp.eye(C, dtype=A.dtype) + A
    rhs = jnp.concatenate([beta[..., None] * v, (beta * jnp.exp(gam))[..., None] * k], axis=-1)
    sol = lax.linalg.triangular_solve(lhs, rhs, left_side=True, lower=True)
    U, W = sol[..., :dv], sol[..., dv:]
    u = U - jnp.einsum('bhtd,bhdv->bhtv', W, S)
    qk = jnp.einsum('bhtd,bhsd->bhts', q, k) * dec_incl
    o = jnp.einsum('bhtd,bhdv->bhtv', q * jnp.exp(gam)[..., None], S) + jnp.einsum('bhts,bhsv->bhtv', qk, u)
    g_last = gam[..., -1]
    S_new = jnp.exp(g_last)[..., None, None] * S + jnp.einsum(
        'bhsd,bhsv->bhdv', k * jnp.exp(g_last[..., None] - gam)[..., None], u)
    return S_new, o


def hybrid_layer(h, norm_w, w_in, conv_w, lower_bound, hg_norm_w, gdn_A_log, gdn_dt_bias, gdn_norm_w, w_out):
    B, T, _ = h.shape
    f32 = jnp.float32
    u = rmsnorm(h, norm_w)
    proj = jnp.einsum('btd,dc->btc', u, w_in)
    cuts = np.cumsum([HG_WIDTH, HG_WIDTH, HG_WIDTH, HG_WIDTH, 3 * GDN_WIDTH, GDN_WIDTH, GDN_HEADS])
    hg_q, hg_f, hg_i, hg_z, gd_qkv, gd_z, gd_a, gd_b = jnp.split(proj, [int(c) for c in cuts], axis=-1)

    q = heads(jax.nn.silu(hg_q.astype(f32)), HG_HEADS)
    lb = lower_bound.astype(f32)
    f = lb + (1.0 - lb) * jax.nn.sigmoid(hg_f.astype(f32))
    k = heads(1.0 - f, HG_HEADS)
    logf = heads(jnp.log(f), HG_HEADS)
    v = heads(hg_i.astype(f32), HG_HEADS)
    S0 = jnp.zeros((B, HG_HEADS, HG_DK, HG_DV), f32)
    o_hg = causal_mixer(hgrn2_chunk, (q, k, v, logf), S0)
    y_hg = gated_rmsnorm(o_hg, hg_z, hg_norm_w)

    qkv = jax.nn.silu(causal_conv(gd_qkv, conv_w).astype(f32))
    gq, gk, gv = jnp.split(qkv, 3, axis=-1)
    gq = heads(gq, GDN_HEADS)
    gk = heads(gk, GDN_HEADS)
    gv = heads(gv, GDN_HEADS)
    gq = gq * lax.rsqrt(jnp.sum(gq * gq, -1, keepdims=True) + EPS) * (GDN_DK ** -0.5)
    gk = gk * lax.rsqrt(jnp.sum(gk * gk, -1, keepdims=True) + EPS)
    g = -jnp.exp(gdn_A_log.astype(f32)) * jax.nn.softplus(gd_a.astype(f32) + gdn_dt_bias.astype(f32))
    g = jnp.transpose(g, (0, 2, 1))
    beta = jnp.transpose(jax.nn.sigmoid(gd_b.astype(f32)), (0, 2, 1))
    S0g = jnp.zeros((B, GDN_HEADS, GDN_DK, GDN_DV), f32)
    o_gd = causal_mixer(gdn_chunk, (gq, gk, gv, g, beta), S0g)
    y_gd = gated_rmsnorm(o_gd, gd_z, gdn_norm_w)

    y = jnp.concatenate([y_hg, y_gd], axis=-1)
    return jnp.einsum('btc,cd->btd', y, w_out).astype(h.dtype)


def setup_inputs(seed: int = 0) -> dict:
    key = jax.random.key(seed)
    ks = jax.random.split(key, 14)
    f32 = jnp.float32
    x = jax.random.normal(ks[0], (BATCH, SEQ, D_MODEL), f32)
    meta_tokens = jax.random.normal(ks[1], (N_META, D_MODEL), f32)
    norm_w = 1.0 + 0.02 * jax.random.normal(ks[2], (DEPTH, D_MODEL), f32)
    w_in = jax.random.normal(ks[3], (DEPTH, D_MODEL, IN_COLS), f32) * D_MODEL ** -0.5
    conv_w = jax.random.normal(ks[4], (DEPTH, CONV_W, 3 * GDN_WIDTH), f32) * CONV_W ** -0.5
    hg_lb_logits = 0.5 * jax.random.normal(ks[5], (DEPTH + 1, HG_WIDTH), f32)
    hg_norm_w = 1.0 + 0.02 * jax.random.normal(ks[6], (DEPTH, HG_DV), f32)
    gdn_A_log = jnp.log(jax.random.uniform(ks[7], (DEPTH, GDN_HEADS), f32, 1.0, 16.0))
    dt = jnp.exp(jax.random.uniform(ks[8], (DEPTH, GDN_HEADS), f32, jnp.log(0.001), jnp.log(0.1)))
    gdn_dt_bias = dt + jnp.log(-jnp.expm1(-dt))
    gdn_norm_w = 1.0 + 0.02 * jax.random.normal(ks[9], (DEPTH, GDN_DV), f32)
    w_out = jax.random.normal(ks[10], (DEPTH, D_MIX, D_MODEL), f32) * D_MIX ** -0.5
    final_norm_w = 1.0 + 0.02 * jax.random.normal(ks[11], (D_MODEL,), f32)
    return {"x": x, "meta_tokens": meta_tokens, "norm_w": norm_w, "w_in": w_in, "conv_w": conv_w,
            "hg_lb_logits": hg_lb_logits, "hg_norm_w": hg_norm_w, "gdn_A_log": gdn_A_log,
            "gdn_dt_bias": gdn_dt_bias, "gdn_norm_w": gdn_norm_w, "w_out": w_out,
            "final_norm_w": final_norm_w}


def reference(x, meta_tokens, norm_w, w_in, conv_w, hg_lb_logits, hg_norm_w, gdn_A_log,
              gdn_dt_bias, gdn_norm_w, w_out, final_norm_w):
    B = x.shape[0]
    meta = jnp.broadcast_to(meta_tokens[None].astype(x.dtype), (B, N_META, D_MODEL))
    h = jnp.concatenate([meta, x], axis=1)
    lower_bounds = jnp.cumsum(jax.nn.softmax(hg_lb_logits.astype(jnp.float32), axis=0), axis=0)
    for l in range(DEPTH):
        h = h + hybrid_layer(h, norm_w[l], w_in[l], conv_w[l], lower_bounds[l], hg_norm_w[l],
                             gdn_A_log[l], gdn_dt_bias[l], gdn_norm_w[l], w_out[l])
    y = rmsnorm(h, final_norm_w)
    return y[:, N_META:]
```

```python
import functools

import jax
import jax.numpy as jnp
from jax import lax
from jax.experimental import pallas as pl
from jax.experimental.pallas import tpu as pltpu

F32 = jnp.float32
BF16 = jnp.bfloat16

D_MODEL = 1024
N_META = 16
HEADS = 4
DH = 128
WIDTH = HEADS * DH
CONV_W = 4
EPS = 1e-6
AB_PAD = 128
IN_PAD = 8 * WIDTH + AB_PAD

CHUNK = 64
SUB = 16
CONV_HIST = 8
CUM_ROWS = 256
VMEM_LIMIT = 56 * 1024 * 1024


def _dot(a, b):
    return jnp.dot(a, b, preferred_element_type=F32)


def _dot_nt(a, b):
    return lax.dot_general(a, b, (((1,), (1,)), ((), ())), preferred_element_type=F32)


def _dot_tn(a, b):
    return lax.dot_general(a, b, (((0,), (0,)), ((), ())), preferred_element_type=F32)


def _split3(x):
    x1 = x.astype(BF16)
    r = x - x1.astype(F32)
    x2 = r.astype(BF16)
    x3 = (r - x2.astype(F32)).astype(BF16)
    return x1, x2, x3


def _silu(x):
    return x * jax.nn.sigmoid(x)


def _softplus(x):
    return jnp.maximum(x, 0.0) + jnp.log1p(jnp.exp(-jnp.abs(x)))


def _chunk_cumsum(x, chunk):
    rows = x.shape[0]
    blk = min(rows, CUM_ROWS)
    r = lax.broadcasted_iota(jnp.int32, (blk, blk), 0)
    c = lax.broadcasted_iota(jnp.int32, (blk, blk), 1)
    tri = jnp.where((r // chunk == c // chunk) & (c <= r), 1.0, 0.0).astype(BF16)
    outs = []
    for i in range(rows // blk):
        parts = _split3(x[i * blk:(i + 1) * blk])
        outs.append(_dot(tri, parts[0]) + _dot(tri, parts[1]) + _dot(tri, parts[2]))
    return outs[0] if len(outs) == 1 else jnp.concatenate(outs, axis=0)


def _proj_kernel(x_ref, nw_ref, w_ref, cw_ref, lbl_ref, alog_ref, dtb_ref, tail_ref,
                 hq_ref, hk_ref, hv_ref, hg_ref, zg_ref, gq_ref, gk_ref, gv_ref,
                 gcol_ref, grow_ref, *rest, tm, chunk, emit_tail):
    if emit_tail:
        tail_out_ref, cbuf = rest
    else:
        (cbuf,) = rest
    j = pl.program_id(1)

    x = x_ref[0]
    ms = jnp.mean(x * x, axis=-1, keepdims=True)
    u = (x * lax.rsqrt(ms + EPS) * nw_ref[...]).astype(BF16)

    def proj(a, b):
        return _dot(u, w_ref[:, a:b])

    hq_ref[0] = _silu(proj(0, WIDTH)).astype(BF16)
    lbl = lbl_ref[...]
    e = jnp.exp(lbl - jnp.max(lbl, axis=0, keepdims=True))
    lb = e[0:1] / jnp.sum(e, axis=0, keepdims=True)
    f = lb + (1.0 - lb) * jax.nn.sigmoid(proj(WIDTH, 2 * WIDTH))
    hk_ref[0] = (1.0 - f).astype(BF16)
    hg_ref[0] = _chunk_cumsum(jnp.log(f), chunk)
    hv_ref[0] = proj(2 * WIDTH, 3 * WIDTH).astype(BF16)
    zg_ref[0, :, 0:WIDTH] = _silu(proj(3 * WIDTH, 4 * WIDTH)).astype(BF16)

    @pl.when(j == 0)
    def _():
        cbuf[0:CONV_HIST, :] = tail_ref[...]

    cbuf[CONV_HIST:CONV_HIST + tm, :] = proj(4 * WIDTH, 7 * WIDTH)
    base = CONV_HIST - (CONV_W - 1)
    conv = cw_ref[0:1, :] * cbuf[base:base + tm, :]
    for t in range(1, CONV_W):
        conv = conv + cw_ref[t:t + 1, :] * cbuf[base + t:base + t + tm, :]
    hist = cbuf[tm:tm + CONV_HIST, :]
    cbuf[0:CONV_HIST, :] = hist
    if emit_tail:
        tail_out_ref[...] = hist
    qkv = _silu(conv)
    for h in range(HEADS):
        q = qkv[:, h * DH:(h + 1) * DH]
        k = qkv[:, WIDTH + h * DH:WIDTH + (h + 1) * DH]
        q = q * lax.rsqrt(jnp.sum(q * q, axis=-1, keepdims=True) + EPS) * (DH ** -0.5)
        k = k * lax.rsqrt(jnp.sum(k * k, axis=-1, keepdims=True) + EPS)
        gq_ref[0, :, h * DH:(h + 1) * DH] = q.astype(BF16)
        gk_ref[0, :, h * DH:(h + 1) * DH] = k.astype(BF16)
    gv_ref[0] = qkv[:, 2 * WIDTH:3 * WIDTH].astype(BF16)
    zg_ref[0, :, WIDTH:2 * WIDTH] = _silu(proj(7 * WIDTH, 8 * WIDTH)).astype(BF16)

    p = proj(8 * WIDTH, 8 * WIDTH + AB_PAD)
    lane = lax.broadcasted_iota(jnp.int32, p.shape, 1)
    g = -jnp.exp(alog_ref[...]) * _softplus(p + dtb_ref[...])
    gam = _chunk_cumsum(jnp.where(lane < HEADS, g, 0.0), chunk)
    comb = jnp.where(lane < HEADS, gam, jnp.where(lane < 2 * HEADS, jax.nn.sigmoid(p), 0.0))
    gcol_ref[0] = comb
    for c in range(tm // chunk):
        blk = comb[c * chunk:(c + 1) * chunk, :]
        grow_ref[0, c] = blk.T[0:2 * HEADS, :]


def _proj_call(x, nw, w, cw, lbl, alog, dtb, tail, *, tm, chunk, emit_tail):
    B, T, _ = x.shape
    nt = T // tm
    tok = lambda width, dt: jax.ShapeDtypeStruct((B, T, width), dt)
    tok_spec = lambda width: pl.BlockSpec((1, tm, width), lambda b, j: (b, j, 0))
    const = lambda shape: pl.BlockSpec(shape, lambda b, j: (0,) * len(shape))
    out_shape = [tok(WIDTH, BF16), tok(WIDTH, BF16), tok(WIDTH, BF16), tok(WIDTH, F32),
                 tok(2 * WIDTH, BF16), tok(WIDTH, BF16), tok(WIDTH, BF16), tok(WIDTH, BF16),
                 tok(AB_PAD, F32),
                 jax.ShapeDtypeStruct((B, T // chunk, 2 * HEADS, chunk), F32)]
    out_specs = [tok_spec(WIDTH), tok_spec(WIDTH), tok_spec(WIDTH), tok_spec(WIDTH),
                 tok_spec(2 * WIDTH), tok_spec(WIDTH), tok_spec(WIDTH), tok_spec(WIDTH),
                 tok_spec(AB_PAD),
                 pl.BlockSpec((1, tm // chunk, 2 * HEADS, chunk), lambda b, j: (b, j, 0, 0))]
    if emit_tail:
        out_shape.append(jax.ShapeDtypeStruct((CONV_HIST, 3 * WIDTH), F32))
        out_specs.append(const((CONV_HIST, 3 * WIDTH)))
    return pl.pallas_call(
        functools.partial(_proj_kernel, tm=tm, chunk=chunk, emit_tail=emit_tail),
        grid=(B, nt),
        in_specs=[tok_spec(D_MODEL), const((1, D_MODEL)), const((D_MODEL, IN_PAD)),
                  const((CONV_W, 3 * WIDTH)), const(lbl.shape), const((1, AB_PAD)),
                  const((1, AB_PAD)), const((CONV_HIST, 3 * WIDTH))],
        out_specs=out_specs,
        out_shape=out_shape,
        scratch_shapes=[pltpu.VMEM((tm + CONV_HIST, 3 * WIDTH), F32)],
        compiler_params=pltpu.CompilerParams(
            dimension_semantics=("parallel", "arbitrary"), vmem_limit_bytes=VMEM_LIMIT),
        name="proj_meta" if emit_tail else "proj",
    )(x, nw, w, cw, lbl, alog, dtb, tail)


def _hgrn2_chunk(q, k, v, G, st, *, need_o):
    C = q.shape[0]
    g_last = G[C - 1:C, :]
    o = None
    if need_o:
        nb = C // SUB
        g_end = [G[(j + 1) * SUB - 1:(j + 1) * SUB, :] for j in range(nb)]
        row = lax.broadcasted_iota(jnp.int32, (C, C), 0)
        col = lax.broadcasted_iota(jnp.int32, (C, C), 1)
        a = jnp.zeros((C, C), F32)
        if nb > 1:
            g_ref = jnp.concatenate(
                [jnp.broadcast_to(g_end[j], (SUB, DH)) for j in range(nb)], axis=0)
            kp = (k * jnp.exp(g_ref - G)).astype(BF16)
            qp = jnp.concatenate(
                [q * jnp.exp(jnp.minimum(G - g_end[j], 0.0)) for j in range(nb - 1)],
                axis=0).astype(BF16)
            p = _dot_nt(qp, kp)
            for j in range(nb - 1):
                m = (col // SUB == j) & (row // SUB > j)
                a = jnp.where(m, p[j * C:(j + 1) * C], a)
        for b in range(nb):
            gb = G[b * SUB:(b + 1) * SUB]
            qb = q[b * SUB:(b + 1) * SUB]
            kb = k[b * SUB:(b + 1) * SUB]
            rowb = lax.broadcasted_iota(jnp.int32, (SUB, C), 0)
            colb = lax.broadcasted_iota(jnp.int32, (SUB, C), 1)
            d = jnp.zeros((SUB, C), F32)
            for s in range(SUB):
                es = jnp.exp(jnp.minimum(gb - gb[s:s + 1], 0.0))
                cs = jnp.sum(qb * es * kb[s:s + 1], axis=-1, keepdims=True)
                d = jnp.where((colb == b * SUB + s) & (rowb >= s), cs, d)
            if nb == 1:
                a = d
            else:
                rsel = row // SUB == b
                dfull = jnp.concatenate(
                    [d if i == b else jnp.zeros((SUB, C), F32) for i in range(nb)], axis=0)
                a = jnp.where(rsel & (col // SUB == b), dfull, a)
        o = _dot(a.astype(BF16), v.astype(BF16))
        if st is not None:
            o = o + _dot_nt((q * jnp.exp(G)).astype(BF16), st.astype(BF16))
    kk = (k * jnp.exp(g_last - G)).astype(BF16)
    st_new = _dot_tn(v.astype(BF16), kk)
    if st is not None:
        st_new = st_new + st * jnp.exp(g_last)
    return o, st_new


def _unit_lower_inverse(a):
    C = a.shape[0]
    row = lax.broadcasted_iota(jnp.int32, (C, C), 0)
    col = lax.broadcasted_iota(jnp.int32, (C, C), 1)
    eye = jnp.where(row == col, 1.0, 0.0)
    t = eye - jnp.where(row // 2 == col // 2, a, 0.0)
    bs = 2
    while bs < C:
        l = jnp.where((row // (2 * bs) == col // (2 * bs)) & (row // bs > col // bs), a, 0.0)
        tb = t.astype(BF16)
        t = t - _dot(_dot(tb, l.astype(BF16)).astype(BF16), tb)
        bs *= 2
    return t


def _gdn_chunk(q, k, v, gam_c, gam_r, beta_c, s, *, need_o):
    C = q.shape[0]
    row = lax.broadcasted_iota(jnp.int32, (C, C), 0)
    col = lax.broadcasted_iota(jnp.int32, (C, C), 1)
    kb = k.astype(BF16)
    dec = jnp.exp(jnp.minimum(gam_c - gam_r, 0.0))
    if need_o:
        pk = _dot_nt(jnp.concatenate([q, k], axis=0).astype(BF16), kb)
        qk, kk = pk[:C], pk[C:]
    else:
        kk = _dot_nt(kb, kb)
    a = jnp.where(row > col, beta_c * kk * dec, 0.0)
    t = _unit_lower_inverse(a)
    egam = jnp.exp(gam_c)
    rhs = beta_c * v
    qs = None
    if s is not None:
        sb = s.astype(BF16)
        if need_o:
            qks = _dot(jnp.concatenate([q * egam, k * egam], axis=0).astype(BF16), sb)
            qs, ks = qks[:C], qks[C:]
        else:
            ks = _dot((k * egam).astype(BF16), sb)
        rhs = beta_c * (v - ks)
    u = _dot(t.astype(BF16), rhs.astype(BF16))
    ub = u.astype(BF16)
    o = None
    if need_o:
        o = _dot(jnp.where(row >= col, qk * dec, 0.0).astype(BF16), ub)
        if qs is not None:
            o = o + qs
    g_last = gam_c[C - 1:C, :]
    s_new = _dot_tn((k * jnp.exp(g_last - gam_c)).astype(BF16), ub)
    if s is not None:
        s_new = s_new + s * jnp.exp(g_last)
    return o, s_new


def _gated_norm(o, w, z):
    return o * lax.rsqrt(jnp.mean(o * o, axis=-1, keepdims=True) + EPS) * w * z


def _mixer_kernel(hq_ref, hk_ref, hv_ref, hg_ref, zg_ref, gq_ref, gk_ref, gv_ref,
                  gcol_ref, grow_ref, sh0_ref, sg0_ref, hw_ref, gw_ref,
                  y_ref, sh_ref, sg_ref, *, bb, tt, chunk):
    j = pl.program_id(1)

    @pl.when(j == 0)
    def _():
        for b in range(bb):
            sh_ref[b] = sh0_ref[...]
            sg_ref[b] = sg0_ref[...]

    hw = hw_ref[...]
    gw = gw_ref[...]

    def body(c, carry):
        t0 = pl.multiple_of(c * chunk, chunk)
        rows = pl.ds(t0, chunk)
        for b in range(bb):
            gcol = gcol_ref[b, rows, :]
            grow = grow_ref[b, c]
            for h in range(HEADS):
                cs = slice(h * DH, (h + 1) * DH)
                o, st = _hgrn2_chunk(hq_ref[b, rows, cs].astype(F32),
                                     hk_ref[b, rows, cs].astype(F32),
                                     hv_ref[b, rows, cs].astype(F32),
                                     hg_ref[b, rows, cs], sh_ref[b, h], need_o=True)
                sh_ref[b, h] = st
                y_ref[b, rows, cs] = _gated_norm(
                    o, hw, zg_ref[b, rows, cs].astype(F32)).astype(BF16)

                o, s = _gdn_chunk(gq_ref[b, rows, cs].astype(F32),
                                  gk_ref[b, rows, cs].astype(F32),
                                  gv_ref[b, rows, cs].astype(F32),
                                  gcol[:, h:h + 1], grow[h:h + 1, :],
                                  gcol[:, HEADS + h:HEADS + h + 1], sg_ref[b, h], need_o=True)
                sg_ref[b, h] = s
                ys = slice(WIDTH + h * DH, WIDTH + (h + 1) * DH)
                y_ref[b, rows, ys] = _gated_norm(
                    o, gw, zg_ref[b, rows, ys].astype(F32)).astype(BF16)
        return carry

    lax.fori_loop(0, tt // chunk, body, 0)


def _mixer_call(hq, hk, hv, hg, zg, gq, gk, gv, gcol, grow, sh0, sg0, hw, gw, *, bb, tt, chunk):
    B, T, _ = hq.shape
    tok_spec = lambda width: pl.BlockSpec((bb, tt, width), lambda i, j: (i, j, 0))
    const = lambda shape: pl.BlockSpec(shape, lambda i, j: (0,) * len(shape))
    return pl.pallas_call(
        functools.partial(_mixer_kernel, bb=bb, tt=tt, chunk=chunk),
        grid=(B // bb, T // tt),
        in_specs=[tok_spec(WIDTH), tok_spec(WIDTH), tok_spec(WIDTH), tok_spec(WIDTH),
                  tok_spec(2 * WIDTH), tok_spec(WIDTH), tok_spec(WIDTH), tok_spec(WIDTH),
                  tok_spec(AB_PAD),
                  pl.BlockSpec((bb, tt // chunk, 2 * HEADS, chunk), lambda i, j: (i, j, 0, 0)),
                  const((HEADS, DH, DH)), const((HEADS, DH, DH)),
                  const((1, DH)), const((1, DH))],
        out_specs=tok_spec(2 * WIDTH),
        out_shape=jax.ShapeDtypeStruct((B, T, 2 * WIDTH), BF16),
        scratch_shapes=[pltpu.VMEM((bb, HEADS, DH, DH), F32),
                        pltpu.VMEM((bb, HEADS, DH, DH), F32)],
        compiler_params=pltpu.CompilerParams(
            dimension_semantics=("parallel", "arbitrary"), vmem_limit_bytes=VMEM_LIMIT),
        name="mixer",
    )(hq, hk, hv, hg, zg, gq, gk, gv, gcol, grow, sh0, sg0, hw, gw)


def _meta_state_kernel(hk_ref, hv_ref, hg_ref, gk_ref, gv_ref, gcol_ref, grow_ref,
                       sh_ref, sg_ref):
    gcol = gcol_ref[0]
    grow = grow_ref[0, 0]
    for h in range(HEADS):
        cs = slice(h * DH, (h + 1) * DH)
        k = hk_ref[0, :, cs].astype(F32)
        _, st = _hgrn2_chunk(k, k, hv_ref[0, :, cs].astype(F32), hg_ref[0, :, cs], None,
                             need_o=False)
        sh_ref[h] = st
        k = gk_ref[0, :, cs].astype(F32)
        _, s = _gdn_chunk(k, k, gv_ref[0, :, cs].astype(F32), gcol[:, h:h + 1],
                          grow[h:h + 1, :], gcol[:, HEADS + h:HEADS + h + 1], None,
                          need_o=False)
        sg_ref[h] = s


def _meta_state_call(hk, hv, hg, gk, gv, gcol, grow):
    state = jax.ShapeDtypeStruct((HEADS, DH, DH), F32)
    return pl.pallas_call(
        _meta_state_kernel,
        out_shape=[state, state],
        name="meta_state",
    )(hk, hv, hg, gk, gv, gcol, grow)


def _out_kernel(y_ref, x_ref, w_ref, fw_ref, o_ref):
    h = x_ref[...] + _dot(y_ref[...], w_ref[...])
    ms = jnp.mean(h * h, axis=-1, keepdims=True)
    o_ref[...] = h * lax.rsqrt(ms + EPS) * fw_ref[...]


def _out_call(y, x, w, fw, *, tm):
    R = x.shape[0]
    row_spec = pl.BlockSpec((tm, D_MODEL), lambda i: (i, 0))
    return pl.pallas_call(
        _out_kernel,
        grid=(R // tm,),
        in_specs=[row_spec, row_spec, pl.BlockSpec((2 * WIDTH, D_MODEL), lambda i: (0, 0)),
                  pl.BlockSpec((1, D_MODEL), lambda i: (0, 0))],
        out_specs=row_spec,
        out_shape=jax.ShapeDtypeStruct((R, D_MODEL), F32),
        compiler_params=pltpu.CompilerParams(
            dimension_semantics=("parallel",), vmem_limit_bytes=VMEM_LIMIT),
        name="out",
    )(y, x, w, fw)


def kernel(x, meta_tokens, norm_w, w_in, conv_w, hg_lb_logits, hg_norm_w, gdn_A_log,
           gdn_dt_bias, gdn_norm_w, w_out, final_norm_w):
    B, T, D = x.shape
    assert D == D_MODEL and norm_w.shape[0] == 1 and meta_tokens.shape == (N_META, D_MODEL)
    assert w_in.shape == (1, D_MODEL, 8 * WIDTH + 2 * HEADS)

    w = w_in[0]
    w = jnp.concatenate(
        [w, jnp.zeros((D_MODEL, AB_PAD - 2 * HEADS), w.dtype)], axis=1).astype(BF16)
    pad = lambda v: jnp.zeros((1, AB_PAD), F32).at[0, :HEADS].set(v.astype(F32))
    alog, dtb = pad(gdn_A_log[0]), pad(gdn_dt_bias[0])
    nw = norm_w[0].reshape(1, D_MODEL)
    cw = conv_w[0]
    proj = functools.partial(_proj_call, nw=nw, w=w, cw=cw, lbl=hg_lb_logits, alog=alog,
                             dtb=dtb)

    zero_tail = jnp.zeros((CONV_HIST, 3 * WIDTH), F32)
    m = proj(meta_tokens[None], tail=zero_tail, tm=N_META, chunk=N_META, emit_tail=True)
    _, mhk, mhv, mhg, _, _, mgk, mgv, mgcol, mgrow, tail = m
    sh0, sg0 = _meta_state_call(mhk, mhv, mhg, mgk, mgv, mgcol, mgrow)

    hq, hk, hv, hg, zg, gq, gk, gv, gcol, grow = proj(
        x, tail=tail, tm=512, chunk=CHUNK, emit_tail=False)
    y = _mixer_call(hq, hk, hv, hg, zg, gq, gk, gv, gcol, grow, sh0, sg0,
                    hg_norm_w[0].reshape(1, DH), gdn_norm_w[0].reshape(1, DH),
                    bb=1, tt=256, chunk=CHUNK)
    out = _out_call(y.reshape(B * T, 2 * WIDTH), x.reshape(B * T, D_MODEL),
                    w_out[0].astype(BF16), final_norm_w.reshape(1, D_MODEL), tm=512)
    return out.reshape(B, T, D_MODEL)
```

```python
import functools

import jax
import jax.numpy as jnp
from jax import lax
from jax.experimental import pallas as pl
from jax.experimental.pallas import tpu as pltpu

F32 = jnp.float32
BF16 = jnp.bfloat16

D_MODEL = 1024
N_META = 16
HEADS = 4
DH = 128
WIDTH = HEADS * DH
CONV_W = 4
EPS = 1e-6
AB_PAD = 128
IN_PAD = 8 * WIDTH + AB_PAD

CHUNK = 64
SUB = 16
CONV_HIST = 8
CUM_ROWS = 256
VMEM_LIMIT = 56 * 1024 * 1024


def _dot(a, b):
    return jnp.dot(a, b, preferred_element_type=F32)


def _dot_nt(a, b):
    return lax.dot_general(a, b, (((1,), (1,)), ((), ())), preferred_element_type=F32)


def _dot_tn(a, b):
    return lax.dot_general(a, b, (((0,), (0,)), ((), ())), preferred_element_type=F32)


def _split3(x):
    x1 = x.astype(BF16)
    r = x - x1.astype(F32)
    x2 = r.astype(BF16)
    x3 = (r - x2.astype(F32)).astype(BF16)
    return x1, x2, x3


def _silu(x):
    return x * jax.nn.sigmoid(x)


def _softplus(x):
    return jnp.maximum(x, 0.0) + jnp.log1p(jnp.exp(-jnp.abs(x)))


def _chunk_cumsum(x, chunk):
    rows = x.shape[0]
    blk = min(rows, CUM_ROWS)
    r = lax.broadcasted_iota(jnp.int32, (blk, blk), 0)
    c = lax.broadcasted_iota(jnp.int32, (blk, blk), 1)
    tri = jnp.where((r // chunk == c // chunk) & (c <= r), 1.0, 0.0).astype(BF16)
    outs = []
    for i in range(rows // blk):
        parts = _split3(x[i * blk:(i + 1) * blk])
        outs.append(_dot(tri, parts[0]) + _dot(tri, parts[1]) + _dot(tri, parts[2]))
    return outs[0] if len(outs) == 1 else jnp.concatenate(outs, axis=0)


def _proj_kernel(x_ref, nw_ref, w_ref, cw_ref, lbl_ref, alog_ref, dtb_ref, tail_ref,
                 hq_ref, hk_ref, hv_ref, hg_ref, zg_ref, gq_ref, gk_ref, gv_ref,
                 gcol_ref, grow_ref, *rest, tm, chunk, emit_tail):
    if emit_tail:
        tail_out_ref, cbuf = rest
    else:
        (cbuf,) = rest
    j = pl.program_id(1)

    x = x_ref[0]
    ms = jnp.mean(x * x, axis=-1, keepdims=True)
    u = (x * lax.rsqrt(ms + EPS) * nw_ref[...]).astype(BF16)

    def proj(a, b):
        return _dot(u, w_ref[:, a:b])

    hq_ref[0] = _silu(proj(0, WIDTH)).astype(BF16)
    lbl = lbl_ref[...]
    e = jnp.exp(lbl - jnp.max(lbl, axis=0, keepdims=True))
    lb = e[0:1] / jnp.sum(e, axis=0, keepdims=True)
    f = lb + (1.0 - lb) * jax.nn.sigmoid(proj(WIDTH, 2 * WIDTH))
    hk_ref[0] = (1.0 - f).astype(BF16)
    hg_ref[0] = _chunk_cumsum(jnp.log(f), chunk)
    hv_ref[0] = proj(2 * WIDTH, 3 * WIDTH).astype(BF16)
    zg_ref[0, :, 0:WIDTH] = _silu(proj(3 * WIDTH, 4 * WIDTH)).astype(BF16)

    @pl.when(j == 0)
    def _():
        cbuf[0:CONV_HIST, :] = tail_ref[...]

    cbuf[CONV_HIST:CONV_HIST + tm, :] = proj(4 * WIDTH, 7 * WIDTH)
    base = CONV_HIST - (CONV_W - 1)
    conv = cw_ref[0:1, :] * cbuf[base:base + tm, :]
    for t in range(1, CONV_W):
        conv = conv + cw_ref[t:t + 1, :] * cbuf[base + t:base + t + tm, :]
    hist = cbuf[tm:tm + CONV_HIST, :]
    cbuf[0:CONV_HIST, :] = hist
    if emit_tail:
        tail_out_ref[...] = hist
    qkv = _silu(conv)
    for h in range(HEADS):
        q = qkv[:, h * DH:(h + 1) * DH]
        k = qkv[:, WIDTH + h * DH:WIDTH + (h + 1) * DH]
        q = q * lax.rsqrt(jnp.sum(q * q, axis=-1, keepdims=True) + EPS) * (DH ** -0.5)
        k = k * lax.rsqrt(jnp.sum(k * k, axis=-1, keepdims=True) + EPS)
        gq_ref[0, :, h * DH:(h + 1) * DH] = q.astype(BF16)
        gk_ref[0, :, h * DH:(h + 1) * DH] = k.astype(BF16)
    gv_ref[0] = qkv[:, 2 * WIDTH:3 * WIDTH].astype(BF16)
    zg_ref[0, :, WIDTH:2 * WIDTH] = _silu(proj(7 * WIDTH, 8 * WIDTH)).astype(BF16)

    p = proj(8 * WIDTH, 8 * WIDTH + AB_PAD)
    lane = lax.broadcasted_iota(jnp.int32, p.shape, 1)
    g = -jnp.exp(alog_ref[...]) * _softplus(p + dtb_ref[...])
    gam = _chunk_cumsum(jnp.where(lane < HEADS, g, 0.0), chunk)
    comb = jnp.where(lane < HEADS, gam, jnp.where(lane < 2 * HEADS, jax.nn.sigmoid(p), 0.0))
    gcol_ref[0] = comb
    for c in range(tm // chunk):
        blk = comb[c * chunk:(c + 1) * chunk, :]
        grow_ref[0, c] = blk.T[0:2 * HEADS, :]


def _proj_call(x, nw, w, cw, lbl, alog, dtb, tail, *, tm, chunk, emit_tail):
    B, T, _ = x.shape
    nt = T // tm
    tok = lambda width, dt: jax.ShapeDtypeStruct((B, T, width), dt)
    tok_spec = lambda width: pl.BlockSpec((1, tm, width), lambda b, j: (b, j, 0))
    const = lambda shape: pl.BlockSpec(shape, lambda b, j: (0,) * len(shape))
    out_shape = [tok(WIDTH, BF16), tok(WIDTH, BF16), tok(WIDTH, BF16), tok(WIDTH, F32),
                 tok(2 * WIDTH, BF16), tok(WIDTH, BF16), tok(WIDTH, BF16), tok(WIDTH, BF16),
                 tok(AB_PAD, F32),
                 jax.ShapeDtypeStruct((B, T // chunk, 2 * HEADS, chunk), F32)]
    out_specs = [tok_spec(WIDTH), tok_spec(WIDTH), tok_spec(WIDTH), tok_spec(WIDTH),
                 tok_spec(2 * WIDTH), tok_spec(WIDTH), tok_spec(WIDTH), tok_spec(WIDTH),
                 tok_spec(AB_PAD),
                 pl.BlockSpec((1, tm // chunk, 2 * HEADS, chunk), lambda b, j: (b, j, 0, 0))]
    if emit_tail:
        out_shape.append(jax.ShapeDtypeStruct((CONV_HIST, 3 * WIDTH), F32))
        out_specs.append(const((CONV_HIST, 3 * WIDTH)))
    return pl.pallas_call(
        functools.partial(_proj_kernel, tm=tm, chunk=chunk, emit_tail=emit_tail),
        grid=(B, nt),
        in_specs=[tok_spec(D_MODEL), const((1, D_MODEL)), const((D_MODEL, IN_PAD)),
                  const((CONV_W, 3 * WIDTH)), const(lbl.shape), const((1, AB_PAD)),
                  const((1, AB_PAD)), const((CONV_HIST, 3 * WIDTH))],
        out_specs=out_specs,
        out_shape=out_shape,
        scratch_shapes=[pltpu.VMEM((tm + CONV_HIST, 3 * WIDTH), F32)],
        compiler_params=pltpu.CompilerParams(
            dimension_semantics=("parallel", "arbitrary"), vmem_limit_bytes=VMEM_LIMIT),
        name="proj_meta" if emit_tail else "proj",
    )(x, nw, w, cw, lbl, alog, dtb, tail)


def _run_interleaved(*gens):
    live = list(gens)
    while live:
        for g in list(live):
            try:
                next(g)
            except StopIteration:
                live.remove(g)


def _hgrn2_stages(q, k, v, G, st, out, *, need_o, idle=(0, 0)):
    n = len(q)
    C = q[0].shape[0]
    g_last = [G[i][C - 1:C, :] for i in range(n)]
    kk = [(k[i] * jnp.exp(g_last[i] - G[i])).astype(BF16) for i in range(n)]
    st_new = [_dot_tn(v[i].astype(BF16), kk[i]) for i in range(n)]
    if st is not None:
        st_new = [st_new[i] + st[i] * jnp.exp(g_last[i]) for i in range(n)]
    out["s"] = st_new
    if not need_o:
        return
    o_inter = None
    if st is not None:
        o_inter = [_dot_nt((q[i] * jnp.exp(G[i])).astype(BF16), st[i].astype(BF16))
                   for i in range(n)]
    yield
    for _ in range(idle[0]):
        yield
    nb = C // SUB
    row = lax.broadcasted_iota(jnp.int32, (C, C), 0)
    col = lax.broadcasted_iota(jnp.int32, (C, C), 1)
    a = [jnp.zeros((C, C), F32) for _ in range(n)]
    g_end = [[G[i][(j + 1) * SUB - 1:(j + 1) * SUB, :] for j in range(nb)] for i in range(n)]
    if nb > 1:
        p = []
        for i in range(n):
            g_ref = jnp.concatenate(
                [jnp.broadcast_to(g_end[i][j], (SUB, DH)) for j in range(nb)], axis=0)
            kp = (k[i] * jnp.exp(g_ref - G[i])).astype(BF16)
            qp = jnp.concatenate(
                [q[i] * jnp.exp(jnp.minimum(G[i] - g_end[i][j], 0.0)) for j in range(nb - 1)],
                axis=0).astype(BF16)
            p.append(_dot_nt(qp, kp))
        for i in range(n):
            for j in range(nb - 1):
                m = (col // SUB == j) & (row // SUB > j)
                a[i] = jnp.where(m, p[i][j * C:(j + 1) * C], a[i])
    yield
    for _ in range(idle[1]):
        yield
    rowb = lax.broadcasted_iota(jnp.int32, (SUB, C), 0)
    colb = lax.broadcasted_iota(jnp.int32, (SUB, C), 1)
    o = []
    for i in range(n):
        for b in range(nb):
            gb = G[i][b * SUB:(b + 1) * SUB]
            qb = q[i][b * SUB:(b + 1) * SUB]
            kb = k[i][b * SUB:(b + 1) * SUB]
            d = jnp.zeros((SUB, C), F32)
            for s in range(SUB):
                es = jnp.exp(jnp.minimum(gb - gb[s:s + 1], 0.0))
                cs = jnp.sum(qb * es * kb[s:s + 1], axis=-1, keepdims=True)
                d = jnp.where((colb == b * SUB + s) & (rowb >= s), cs, d)
            if nb == 1:
                a[i] = d
            else:
                dfull = jnp.concatenate(
                    [d if r == b else jnp.zeros((SUB, C), F32) for r in range(nb)], axis=0)
                a[i] = jnp.where((row // SUB == b) & (col // SUB == b), dfull, a[i])
        oi = _dot(a[i].astype(BF16), v[i].astype(BF16))
        o.append(oi if o_inter is None else oi + o_inter[i])
    out["o"] = o


def _gdn_stages(q, k, v, gam_c, gam_r, beta_c, s, out, *, need_o):
    n = len(q)
    C = q[0].shape[0]
    rng = range(n)
    row = lax.broadcasted_iota(jnp.int32, (C, C), 0)
    col = lax.broadcasted_iota(jnp.int32, (C, C), 1)
    kb = [k[i].astype(BF16) for i in rng]
    if need_o:
        pk = [_dot_nt(jnp.concatenate([q[i], k[i]], axis=0).astype(BF16), kb[i]) for i in rng]
        qk, kk = [x[:C] for x in pk], [x[C:] for x in pk]
    else:
        kk = [_dot_nt(kb[i], kb[i]) for i in rng]
    yield
    egam = [jnp.exp(gam_c[i]) for i in rng]
    qs = None
    if s is not None:
        sb = [s[i].astype(BF16) for i in rng]
        if need_o:
            qks = [_dot(jnp.concatenate([q[i] * egam[i], k[i] * egam[i]], axis=0).astype(BF16),
                        sb[i]) for i in rng]
            qs, ks = [x[:C] for x in qks], [x[C:] for x in qks]
        else:
            ks = [_dot((k[i] * egam[i]).astype(BF16), sb[i]) for i in rng]
        rhs = [beta_c[i] * (v[i] - ks[i]) for i in rng]
        yield
    else:
        rhs = [beta_c[i] * v[i] for i in rng]
    dec = [jnp.exp(jnp.minimum(gam_c[i] - gam_r[i], 0.0)) for i in rng]
    a = [jnp.where(row > col, beta_c[i] * kk[i] * dec[i], 0.0) for i in rng]
    eye = jnp.where(row == col, 1.0, 0.0)
    t = [eye - jnp.where(row // 2 == col // 2, a[i], 0.0) for i in rng]
    bs = 2
    while bs < C:
        lm = (row // (2 * bs) == col // (2 * bs)) & (row // bs > col // bs)
        tb = [t[i].astype(BF16) for i in rng]
        tl = [_dot(tb[i], jnp.where(lm, a[i], 0.0).astype(BF16)) for i in rng]
        yield
        t = [t[i] - _dot(tl[i].astype(BF16), tb[i]) for i in rng]
        yield
        bs *= 2
    ub = [_dot(t[i].astype(BF16), rhs[i].astype(BF16)).astype(BF16) for i in rng]
    yield
    g_last = [gam_c[i][C - 1:C, :] for i in rng]
    s_new = [_dot_tn((k[i] * jnp.exp(g_last[i] - gam_c[i])).astype(BF16), ub[i]) for i in rng]
    if s is not None:
        s_new = [s_new[i] + s[i] * jnp.exp(g_last[i]) for i in rng]
    out["s"] = s_new
    if need_o:
        o = [_dot(jnp.where(row >= col, qk[i] * dec[i], 0.0).astype(BF16), ub[i]) for i in rng]
        if qs is not None:
            o = [o[i] + qs[i] for i in rng]
        out["o"] = o


def _gated_norm(o, w, z):
    return o * lax.rsqrt(jnp.mean(o * o, axis=-1, keepdims=True) + EPS) * w * z


def _mixer_kernel(hq_ref, hk_ref, hv_ref, hg_ref, zg_ref, gq_ref, gk_ref, gv_ref,
                  gcol_ref, grow_ref, sh0_ref, sg0_ref, hw_ref, gw_ref,
                  y_ref, sh_ref, sg_ref, *, bb, tt, chunk):
    j = pl.program_id(1)

    @pl.when(j == 0)
    def _():
        for b in range(bb):
            sh_ref[b] = sh0_ref[...]
            sg_ref[b] = sg0_ref[...]

    hw = hw_ref[...]
    gw = gw_ref[...]
    probs = [(b, h) for b in range(bb) for h in range(HEADS)]
    hs = lambda h: slice(h * DH, (h + 1) * DH)

    def body(c, carry):
        t0 = pl.multiple_of(c * chunk, chunk)
        rows = pl.ds(t0, chunk)
        ld = lambda ref: [ref[b, rows, hs(h)].astype(F32) for b, h in probs]
        gcol = [gcol_ref[b, rows, :] for b in range(bb)]
        grow = [grow_ref[b, c] for b in range(bb)]
        hout, gout = {}, {}
        _run_interleaved(
            _gdn_stages(ld(gq_ref), ld(gk_ref), ld(gv_ref),
                        [gcol[b][:, h:h + 1] for b, h in probs],
                        [grow[b][h:h + 1, :] for b, h in probs],
                        [gcol[b][:, HEADS + h:HEADS + h + 1] for b, h in probs],
                        [sg_ref[b, h] for b, h in probs], gout, need_o=True),
            _hgrn2_stages(ld(hq_ref), ld(hk_ref), ld(hv_ref), ld(hg_ref),
                          [sh_ref[b, h] for b, h in probs], hout, need_o=True, idle=(1, 8)))
        for i, (b, h) in enumerate(probs):
            sh_ref[b, h] = hout["s"][i]
            sg_ref[b, h] = gout["s"][i]
            y_ref[b, rows, hs(h)] = _gated_norm(
                hout["o"][i], hw, zg_ref[b, rows, hs(h)].astype(F32)).astype(BF16)
            ys = slice(WIDTH + h * DH, WIDTH + (h + 1) * DH)
            y_ref[b, rows, ys] = _gated_norm(
                gout["o"][i], gw, zg_ref[b, rows, ys].astype(F32)).astype(BF16)
        return carry

    lax.fori_loop(0, tt // chunk, body, 0)


def _mixer_call(hq, hk, hv, hg, zg, gq, gk, gv, gcol, grow, sh0, sg0, hw, gw, *, bb, tt, chunk):
    B, T, _ = hq.shape
    tok_spec = lambda width: pl.BlockSpec((bb, tt, width), lambda i, j: (i, j, 0))
    const = lambda shape: pl.BlockSpec(shape, lambda i, j: (0,) * len(shape))
    return pl.pallas_call(
        functools.partial(_mixer_kernel, bb=bb, tt=tt, chunk=chunk),
        grid=(B // bb, T // tt),
        in_specs=[tok_spec(WIDTH), tok_spec(WIDTH), tok_spec(WIDTH), tok_spec(WIDTH),
                  tok_spec(2 * WIDTH), tok_spec(WIDTH), tok_spec(WIDTH), tok_spec(WIDTH),
                  tok_spec(AB_PAD),
                  pl.BlockSpec((bb, tt // chunk, 2 * HEADS, chunk), lambda i, j: (i, j, 0, 0)),
                  const((HEADS, DH, DH)), const((HEADS, DH, DH)),
                  const((1, DH)), const((1, DH))],
        out_specs=tok_spec(2 * WIDTH),
        out_shape=jax.ShapeDtypeStruct((B, T, 2 * WIDTH), BF16),
        scratch_shapes=[pltpu.VMEM((bb, HEADS, DH, DH), F32),
                        pltpu.VMEM((bb, HEADS, DH, DH), F32)],
        compiler_params=pltpu.CompilerParams(
            dimension_semantics=("parallel", "arbitrary"), vmem_limit_bytes=VMEM_LIMIT),
        name="mixer",
    )(hq, hk, hv, hg, zg, gq, gk, gv, gcol, grow, sh0, sg0, hw, gw)


def _meta_state_kernel(hk_ref, hv_ref, hg_ref, gk_ref, gv_ref, gcol_ref, grow_ref,
                       sh_ref, sg_ref):
    gcol = gcol_ref[0]
    grow = grow_ref[0, 0]
    hs = lambda h: slice(h * DH, (h + 1) * DH)
    heads = range(HEADS)
    hk = [hk_ref[0, :, hs(h)].astype(F32) for h in heads]
    gk = [gk_ref[0, :, hs(h)].astype(F32) for h in heads]
    hout, gout = {}, {}
    _run_interleaved(
        _gdn_stages(gk, gk, [gv_ref[0, :, hs(h)].astype(F32) for h in heads],
                    [gcol[:, h:h + 1] for h in heads], [grow[h:h + 1, :] for h in heads],
                    [gcol[:, HEADS + h:HEADS + h + 1] for h in heads], None, gout,
                    need_o=False),
        _hgrn2_stages(hk, hk, [hv_ref[0, :, hs(h)].astype(F32) for h in heads],
                      [hg_ref[0, :, hs(h)] for h in heads], None, hout, need_o=False))
    for h in heads:
        sh_ref[h] = hout["s"][h]
        sg_ref[h] = gout["s"][h]


def _meta_state_call(hk, hv, hg, gk, gv, gcol, grow):
    state = jax.ShapeDtypeStruct((HEADS, DH, DH), F32)
    return pl.pallas_call(
        _meta_state_kernel,
        out_shape=[state, state],
        name="meta_state",
    )(hk, hv, hg, gk, gv, gcol, grow)


def _out_kernel(y_ref, x_ref, w_ref, fw_ref, o_ref):
    h = x_ref[...] + _dot(y_ref[...], w_ref[...])
    ms = jnp.mean(h * h, axis=-1, keepdims=True)
    o_ref[...] = h * lax.rsqrt(ms + EPS) * fw_ref[...]


def _out_call(y, x, w, fw, *, tm):
    R = x.shape[0]
    row_spec = pl.BlockSpec((tm, D_MODEL), lambda i: (i, 0))
    return pl.pallas_call(
        _out_kernel,
        grid=(R // tm,),
        in_specs=[row_spec, row_spec, pl.BlockSpec((2 * WIDTH, D_MODEL), lambda i: (0, 0)),
                  pl.BlockSpec((1, D_MODEL), lambda i: (0, 0))],
        out_specs=row_spec,
        out_shape=jax.ShapeDtypeStruct((R, D_MODEL), F32),
        compiler_params=pltpu.CompilerParams(
            dimension_semantics=("parallel",), vmem_limit_bytes=VMEM_LIMIT),
        name="out",
    )(y, x, w, fw)


def kernel(x, meta_tokens, norm_w, w_in, conv_w, hg_lb_logits, hg_norm_w, gdn_A_log,
           gdn_dt_bias, gdn_norm_w, w_out, final_norm_w):
    B, T, D = x.shape
    assert D == D_MODEL and norm_w.shape[0] == 1 and meta_tokens.shape == (N_META, D_MODEL)
    assert w_in.shape == (1, D_MODEL, 8 * WIDTH + 2 * HEADS)

    w = w_in[0]
    w = jnp.concatenate(
        [w, jnp.zeros((D_MODEL, AB_PAD - 2 * HEADS), w.dtype)], axis=1).astype(BF16)
    pad = lambda v: jnp.zeros((1, AB_PAD), F32).at[0, :HEADS].set(v.astype(F32))
    alog, dtb = pad(gdn_A_log[0]), pad(gdn_dt_bias[0])
    nw = norm_w[0].reshape(1, D_MODEL)
    cw = conv_w[0]
    proj = functools.partial(_proj_call, nw=nw, w=w, cw=cw, lbl=hg_lb_logits, alog=alog,
                             dtb=dtb)

    zero_tail = jnp.zeros((CONV_HIST, 3 * WIDTH), F32)
    m = proj(meta_tokens[None], tail=zero_tail, tm=N_META, chunk=N_META, emit_tail=True)
    _, mhk, mhv, mhg, _, _, mgk, mgv, mgcol, mgrow, tail = m
    sh0, sg0 = _meta_state_call(mhk, mhv, mhg, mgk, mgv, mgcol, mgrow)

    hq, hk, hv, hg, zg, gq, gk, gv, gcol, grow = proj(
        x, tail=tail, tm=512, chunk=CHUNK, emit_tail=False)
    y = _mixer_call(hq, hk, hv, hg, zg, gq, gk, gv, gcol, grow, sh0, sg0,
                    hg_norm_w[0].reshape(1, DH), gdn_norm_w[0].reshape(1, DH),
                    bb=2, tt=256, chunk=CHUNK)
    out = _out_call(y.reshape(B * T, 2 * WIDTH), x.reshape(B * T, D_MODEL),
                    w_out[0].astype(BF16), final_norm_w.reshape(1, D_MODEL), tm=512)
    return out.reshape(B, T, D_MODEL)
```

```python
import functools

import jax
import jax.numpy as jnp
from jax import lax
from jax.experimental import pallas as pl
from jax.experimental.pallas import tpu as pltpu

F32 = jnp.float32
BF16 = jnp.bfloat16

D_MODEL = 1024
N_META = 16
HEADS = 4
DH = 128
WIDTH = HEADS * DH
CONV_W = 4
EPS = 1e-6
AB_PAD = 128
IN_PAD = 8 * WIDTH + AB_PAD

CHUNK = 64
SUB = 16
SAFE_RANGE = 60.0
CONV_HIST = 8
CUM_ROWS = 256
VMEM_LIMIT = 56 * 1024 * 1024


def _dot(a, b):
    return jnp.dot(a, b, preferred_element_type=F32)


def _dot_nt(a, b):
    return lax.dot_general(a, b, (((1,), (1,)), ((), ())), preferred_element_type=F32)


def _dot_tn(a, b):
    return lax.dot_general(a, b, (((0,), (0,)), ((), ())), preferred_element_type=F32)


def _split3(x):
    x1 = x.astype(BF16)
    r = x - x1.astype(F32)
    x2 = r.astype(BF16)
    x3 = (r - x2.astype(F32)).astype(BF16)
    return x1, x2, x3


def _silu(x):
    return x * jax.nn.sigmoid(x)


def _softplus(x):
    return jnp.maximum(x, 0.0) + jnp.log1p(jnp.exp(-jnp.abs(x)))


def _chunk_cumsum(x, chunk):
    rows = x.shape[0]
    blk = min(rows, CUM_ROWS)
    r = lax.broadcasted_iota(jnp.int32, (blk, blk), 0)
    c = lax.broadcasted_iota(jnp.int32, (blk, blk), 1)
    tri = jnp.where((r // chunk == c // chunk) & (c <= r), 1.0, 0.0).astype(BF16)
    outs = []
    for i in range(rows // blk):
        parts = _split3(x[i * blk:(i + 1) * blk])
        outs.append(_dot(tri, parts[0]) + _dot(tri, parts[1]) + _dot(tri, parts[2]))
    return outs[0] if len(outs) == 1 else jnp.concatenate(outs, axis=0)


def _proj_kernel(x_ref, nw_ref, w_ref, cw_ref, lbl_ref, alog_ref, dtb_ref, tail_ref,
                 hq_ref, hk_ref, hv_ref, hg_ref, zg_ref, gq_ref, gk_ref, gv_ref,
                 gcol_ref, grow_ref, *rest, tm, chunk, emit_tail):
    if emit_tail:
        tail_out_ref, cbuf = rest
    else:
        (cbuf,) = rest
    j = pl.program_id(1)

    x = x_ref[0]
    ms = jnp.mean(x * x, axis=-1, keepdims=True)
    u = (x * lax.rsqrt(ms + EPS) * nw_ref[...]).astype(BF16)

    def proj(a, b):
        return _dot(u, w_ref[:, a:b])

    hq_ref[0] = _silu(proj(0, WIDTH)).astype(BF16)
    lbl = lbl_ref[...]
    e = jnp.exp(lbl - jnp.max(lbl, axis=0, keepdims=True))
    lb = e[0:1] / jnp.sum(e, axis=0, keepdims=True)
    f = lb + (1.0 - lb) * jax.nn.sigmoid(proj(WIDTH, 2 * WIDTH))
    hk_ref[0] = (1.0 - f).astype(BF16)
    hg_ref[0] = _chunk_cumsum(jnp.log(f), chunk)
    hv_ref[0] = proj(2 * WIDTH, 3 * WIDTH).astype(BF16)
    zg_ref[0, :, 0:WIDTH] = _silu(proj(3 * WIDTH, 4 * WIDTH)).astype(BF16)

    @pl.when(j == 0)
    def _():
        cbuf[0:CONV_HIST, :] = tail_ref[...]

    cbuf[CONV_HIST:CONV_HIST + tm, :] = proj(4 * WIDTH, 7 * WIDTH)
    base = CONV_HIST - (CONV_W - 1)
    conv = cw_ref[0:1, :] * cbuf[base:base + tm, :]
    for t in range(1, CONV_W):
        conv = conv + cw_ref[t:t + 1, :] * cbuf[base + t:base + t + tm, :]
    hist = cbuf[tm:tm + CONV_HIST, :]
    cbuf[0:CONV_HIST, :] = hist
    if emit_tail:
        tail_out_ref[...] = hist
    qkv = _silu(conv)
    for h in range(HEADS):
        q = qkv[:, h * DH:(h + 1) * DH]
        k = qkv[:, WIDTH + h * DH:WIDTH + (h + 1) * DH]
        q = q * lax.rsqrt(jnp.sum(q * q, axis=-1, keepdims=True) + EPS) * (DH ** -0.5)
        k = k * lax.rsqrt(jnp.sum(k * k, axis=-1, keepdims=True) + EPS)
        gq_ref[0, :, h * DH:(h + 1) * DH] = q.astype(BF16)
        gk_ref[0, :, h * DH:(h + 1) * DH] = k.astype(BF16)
    gv_ref[0] = qkv[:, 2 * WIDTH:3 * WIDTH].astype(BF16)
    zg_ref[0, :, WIDTH:2 * WIDTH] = _silu(proj(7 * WIDTH, 8 * WIDTH)).astype(BF16)

    p = proj(8 * WIDTH, 8 * WIDTH + AB_PAD)
    lane = lax.broadcasted_iota(jnp.int32, p.shape, 1)
    g = -jnp.exp(alog_ref[...]) * _softplus(p + dtb_ref[...])
    gam = _chunk_cumsum(jnp.where(lane < HEADS, g, 0.0), chunk)
    comb = jnp.where(lane < HEADS, gam, jnp.where(lane < 2 * HEADS, jax.nn.sigmoid(p), 0.0))
    gcol_ref[0] = comb
    for c in range(tm // chunk):
        blk = comb[c * chunk:(c + 1) * chunk, :]
        grow_ref[0, c] = blk.T[0:2 * HEADS, :]


def _proj_call(x, nw, w, cw, lbl, alog, dtb, tail, *, tm, chunk, emit_tail):
    B, T, _ = x.shape
    nt = T // tm
    tok = lambda width, dt: jax.ShapeDtypeStruct((B, T, width), dt)
    tok_spec = lambda width: pl.BlockSpec((1, tm, width), lambda b, j: (b, j, 0))
    const = lambda shape: pl.BlockSpec(shape, lambda b, j: (0,) * len(shape))
    out_shape = [tok(WIDTH, BF16), tok(WIDTH, BF16), tok(WIDTH, BF16), tok(WIDTH, F32),
                 tok(2 * WIDTH, BF16), tok(WIDTH, BF16), tok(WIDTH, BF16), tok(WIDTH, BF16),
                 tok(AB_PAD, F32),
                 jax.ShapeDtypeStruct((B, T // chunk, 2 * HEADS, chunk), F32)]
    out_specs = [tok_spec(WIDTH), tok_spec(WIDTH), tok_spec(WIDTH), tok_spec(WIDTH),
                 tok_spec(2 * WIDTH), tok_spec(WIDTH), tok_spec(WIDTH), tok_spec(WIDTH),
                 tok_spec(AB_PAD),
                 pl.BlockSpec((1, tm // chunk, 2 * HEADS, chunk), lambda b, j: (b, j, 0, 0))]
    if emit_tail:
        out_shape.append(jax.ShapeDtypeStruct((CONV_HIST, 3 * WIDTH), F32))
        out_specs.append(const((CONV_HIST, 3 * WIDTH)))
    return pl.pallas_call(
        functools.partial(_proj_kernel, tm=tm, chunk=chunk, emit_tail=emit_tail),
        grid=(B, nt),
        in_specs=[tok_spec(D_MODEL), const((1, D_MODEL)), const((D_MODEL, IN_PAD)),
                  const((CONV_W, 3 * WIDTH)), const(lbl.shape), const((1, AB_PAD)),
                  const((1, AB_PAD)), const((CONV_HIST, 3 * WIDTH))],
        out_specs=out_specs,
        out_shape=out_shape,
        scratch_shapes=[pltpu.VMEM((tm + CONV_HIST, 3 * WIDTH), F32)],
        compiler_params=pltpu.CompilerParams(
            dimension_semantics=("parallel", "arbitrary"), vmem_limit_bytes=VMEM_LIMIT),
        name="proj_meta" if emit_tail else "proj",
    )(x, nw, w, cw, lbl, alog, dtb, tail)


def _run_interleaved(*gens):
    live = list(gens)
    while live:
        for g in list(live):
            try:
                next(g)
            except StopIteration:
                live.remove(g)


def _hgrn2_stages(q, k, v, G, st, out, *, need_o, bounded=False, idle=(0, 0)):
    n = len(q)
    C = q[0].shape[0]
    g_last = [G[i][C - 1:C, :] for i in range(n)]
    kk = [(k[i] * jnp.exp(g_last[i] - G[i])).astype(BF16) for i in range(n)]
    st_new = [_dot_tn(v[i].astype(BF16), kk[i]) for i in range(n)]
    if st is not None:
        st_new = [st_new[i] + st[i] * jnp.exp(g_last[i]) for i in range(n)]
    out["s"] = st_new
    if not need_o:
        return
    o_inter = None
    if st is not None:
        o_inter = [_dot_nt((q[i] * jnp.exp(G[i])).astype(BF16), st[i].astype(BF16))
                   for i in range(n)]
    yield
    for _ in range(idle[0]):
        yield
    row = lax.broadcasted_iota(jnp.int32, (C, C), 0)
    col = lax.broadcasted_iota(jnp.int32, (C, C), 1)
    if bounded:
        g_mid = [G[i][C // 2 - 1:C // 2, :] for i in range(n)]
        p = [_dot_nt((q[i] * jnp.exp(G[i] - g_mid[i])).astype(BF16),
                     (k[i] * jnp.exp(g_mid[i] - G[i])).astype(BF16)) for i in range(n)]
        yield
        for _ in range(idle[1]):
            yield
        o = [_dot(jnp.where(row >= col, p[i], 0.0).astype(BF16), v[i].astype(BF16))
             for i in range(n)]
        out["o"] = o if o_inter is None else [o[i] + o_inter[i] for i in range(n)]
        return
    nb = C // SUB
    a = [jnp.zeros((C, C), F32) for _ in range(n)]
    g_end = [[G[i][(j + 1) * SUB - 1:(j + 1) * SUB, :] for j in range(nb)] for i in range(n)]
    if nb > 1:
        p = []
        for i in range(n):
            g_ref = jnp.concatenate(
                [jnp.broadcast_to(g_end[i][j], (SUB, DH)) for j in range(nb)], axis=0)
            kp = (k[i] * jnp.exp(g_ref - G[i])).astype(BF16)
            qp = jnp.concatenate(
                [q[i] * jnp.exp(jnp.minimum(G[i] - g_end[i][j], 0.0)) for j in range(nb - 1)],
                axis=0).astype(BF16)
            p.append(_dot_nt(qp, kp))
        for i in range(n):
            for j in range(nb - 1):
                m = (col // SUB == j) & (row // SUB > j)
                a[i] = jnp.where(m, p[i][j * C:(j + 1) * C], a[i])
    yield
    for _ in range(idle[1]):
        yield
    rowb = lax.broadcasted_iota(jnp.int32, (SUB, C), 0)
    colb = lax.broadcasted_iota(jnp.int32, (SUB, C), 1)
    o = []
    for i in range(n):
        for b in range(nb):
            gb = G[i][b * SUB:(b + 1) * SUB]
            qb = q[i][b * SUB:(b + 1) * SUB]
            kb = k[i][b * SUB:(b + 1) * SUB]
            d = jnp.zeros((SUB, C), F32)
            for s in range(SUB):
                es = jnp.exp(jnp.minimum(gb - gb[s:s + 1], 0.0))
                cs = jnp.sum(qb * es * kb[s:s + 1], axis=-1, keepdims=True)
                d = jnp.where((colb == b * SUB + s) & (rowb >= s), cs, d)
            if nb == 1:
                a[i] = d
            else:
                dfull = jnp.concatenate(
                    [d if r == b else jnp.zeros((SUB, C), F32) for r in range(nb)], axis=0)
                a[i] = jnp.where((row // SUB == b) & (col // SUB == b), dfull, a[i])
        oi = _dot(a[i].astype(BF16), v[i].astype(BF16))
        o.append(oi if o_inter is None else oi + o_inter[i])
    out["o"] = o


def _gdn_stages(q, k, v, gam_c, gam_r, beta_c, s, out, *, need_o):
    n = len(q)
    C = q[0].shape[0]
    rng = range(n)
    row = lax.broadcasted_iota(jnp.int32, (C, C), 0)
    col = lax.broadcasted_iota(jnp.int32, (C, C), 1)
    kb = [k[i].astype(BF16) for i in rng]
    if need_o:
        pk = [_dot_nt(jnp.concatenate([q[i], k[i]], axis=0).astype(BF16), kb[i]) for i in rng]
        qk, kk = [x[:C] for x in pk], [x[C:] for x in pk]
    else:
        kk = [_dot_nt(kb[i], kb[i]) for i in rng]
    yield
    egam = [jnp.exp(gam_c[i]) for i in rng]
    qs = None
    if s is not None:
        sb = [s[i].astype(BF16) for i in rng]
        if need_o:
            qks = [_dot(jnp.concatenate([q[i] * egam[i], k[i] * egam[i]], axis=0).astype(BF16),
                        sb[i]) for i in rng]
            qs, ks = [x[:C] for x in qks], [x[C:] for x in qks]
        else:
            ks = [_dot((k[i] * egam[i]).astype(BF16), sb[i]) for i in rng]
        rhs = [beta_c[i] * (v[i] - ks[i]) for i in rng]
        yield
    else:
        rhs = [beta_c[i] * v[i] for i in rng]
    dec = [jnp.exp(jnp.minimum(gam_c[i] - gam_r[i], 0.0)) for i in rng]
    a = [jnp.where(row > col, beta_c[i] * kk[i] * dec[i], 0.0) for i in rng]
    eye = jnp.where(row == col, 1.0, 0.0)
    t = [eye - jnp.where(row // 2 == col // 2, a[i], 0.0) for i in rng]
    bs = 2
    while bs < C:
        lm = (row // (2 * bs) == col // (2 * bs)) & (row // bs > col // bs)
        tb = [t[i].astype(BF16) for i in rng]
        tl = [_dot(tb[i], jnp.where(lm, a[i], 0.0).astype(BF16)) for i in rng]
        yield
        t = [t[i] - _dot(tl[i].astype(BF16), tb[i]) for i in rng]
        yield
        bs *= 2
    ub = [_dot(t[i].astype(BF16), rhs[i].astype(BF16)).astype(BF16) for i in rng]
    yield
    g_last = [gam_c[i][C - 1:C, :] for i in rng]
    s_new = [_dot_tn((k[i] * jnp.exp(g_last[i] - gam_c[i])).astype(BF16), ub[i]) for i in rng]
    if s is not None:
        s_new = [s_new[i] + s[i] * jnp.exp(g_last[i]) for i in rng]
    out["s"] = s_new
    if need_o:
        o = [_dot(jnp.where(row >= col, qk[i] * dec[i], 0.0).astype(BF16), ub[i]) for i in rng]
        if qs is not None:
            o = [o[i] + qs[i] for i in rng]
        out["o"] = o


def _gated_norm(o, w, z):
    return o * lax.rsqrt(jnp.mean(o * o, axis=-1, keepdims=True) + EPS) * w * z


def _mixer_kernel(hq_ref, hk_ref, hv_ref, hg_ref, zg_ref, gq_ref, gk_ref, gv_ref,
                  gcol_ref, grow_ref, sh0_ref, sg0_ref, hw_ref, gw_ref,
                  y_ref, sh_ref, sg_ref, bounded_ref, *, bb, tt, chunk):
    j = pl.program_id(1)

    @pl.when(j == 0)
    def _():
        for b in range(bb):
            sh_ref[b] = sh0_ref[...]
            sg_ref[b] = sg0_ref[...]

    for c in range(tt // chunk):
        span = None
        for b in range(bb):
            g_mid = hg_ref[b, c * chunk + chunk // 2 - 1:c * chunk + chunk // 2, :]
            g_last = hg_ref[b, (c + 1) * chunk - 1:(c + 1) * chunk, :]
            s = jnp.maximum(-g_mid, g_mid - g_last)
            span = s if span is None else jnp.maximum(span, s)
        bounded_ref[c] = jnp.where(jnp.max(span) <= SAFE_RANGE, 1, 0)

    hw = hw_ref[...]
    gw = gw_ref[...]
    probs = [(b, h) for b in range(bb) for h in range(HEADS)]
    hs = lambda h: slice(h * DH, (h + 1) * DH)

    def step(c, bounded):
        t0 = pl.multiple_of(c * chunk, chunk)
        rows = pl.ds(t0, chunk)
        ld = lambda ref: [ref[b, rows, hs(h)].astype(F32) for b, h in probs]
        gcol = [gcol_ref[b, rows, :] for b in range(bb)]
        grow = [grow_ref[b, c] for b in range(bb)]
        hout, gout = {}, {}
        _run_interleaved(
            _gdn_stages(ld(gq_ref), ld(gk_ref), ld(gv_ref),
                        [gcol[b][:, h:h + 1] for b, h in probs],
                        [grow[b][h:h + 1, :] for b, h in probs],
                        [gcol[b][:, HEADS + h:HEADS + h + 1] for b, h in probs],
                        [sg_ref[b, h] for b, h in probs], gout, need_o=True),
            _hgrn2_stages(ld(hq_ref), ld(hk_ref), ld(hv_ref), ld(hg_ref),
                          [sh_ref[b, h] for b, h in probs], hout, need_o=True,
                          bounded=bounded, idle=(1, 8)))
        for i, (b, h) in enumerate(probs):
            sh_ref[b, h] = hout["s"][i]
            sg_ref[b, h] = gout["s"][i]
            y_ref[b, rows, hs(h)] = _gated_norm(
                hout["o"][i], hw, zg_ref[b, rows, hs(h)].astype(F32)).astype(BF16)
            ys = slice(WIDTH + h * DH, WIDTH + (h + 1) * DH)
            y_ref[b, rows, ys] = _gated_norm(
                gout["o"][i], gw, zg_ref[b, rows, ys].astype(F32)).astype(BF16)

    def body(c, carry):
        is_bounded = bounded_ref[c] == 1
        pl.when(is_bounded)(lambda: step(c, True))
        pl.when(jnp.logical_not(is_bounded))(lambda: step(c, False))
        return carry

    lax.fori_loop(0, tt // chunk, body, 0)


def _mixer_call(hq, hk, hv, hg, zg, gq, gk, gv, gcol, grow, sh0, sg0, hw, gw, *, bb, tt, chunk):
    B, T, _ = hq.shape
    tok_spec = lambda width: pl.BlockSpec((bb, tt, width), lambda i, j: (i, j, 0))
    const = lambda shape: pl.BlockSpec(shape, lambda i, j: (0,) * len(shape))
    return pl.pallas_call(
        functools.partial(_mixer_kernel, bb=bb, tt=tt, chunk=chunk),
        grid=(B // bb, T // tt),
        in_specs=[tok_spec(WIDTH), tok_spec(WIDTH), tok_spec(WIDTH), tok_spec(WIDTH),
                  tok_spec(2 * WIDTH), tok_spec(WIDTH), tok_spec(WIDTH), tok_spec(WIDTH),
                  tok_spec(AB_PAD),
                  pl.BlockSpec((bb, tt // chunk, 2 * HEADS, chunk), lambda i, j: (i, j, 0, 0)),
                  const((HEADS, DH, DH)), const((HEADS, DH, DH)),
                  const((1, DH)), const((1, DH))],
        out_specs=tok_spec(2 * WIDTH),
        out_shape=jax.ShapeDtypeStruct((B, T, 2 * WIDTH), BF16),
        scratch_shapes=[pltpu.VMEM((bb, HEADS, DH, DH), F32),
                        pltpu.VMEM((bb, HEADS, DH, DH), F32),
                        pltpu.SMEM((tt // chunk,), jnp.int32)],
        compiler_params=pltpu.CompilerParams(
            dimension_semantics=("parallel", "arbitrary"), vmem_limit_bytes=VMEM_LIMIT),
        name="mixer",
    )(hq, hk, hv, hg, zg, gq, gk, gv, gcol, grow, sh0, sg0, hw, gw)


def _meta_state_kernel(hk_ref, hv_ref, hg_ref, gk_ref, gv_ref, gcol_ref, grow_ref,
                       sh_ref, sg_ref):
    gcol = gcol_ref[0]
    grow = grow_ref[0, 0]
    hs = lambda h: slice(h * DH, (h + 1) * DH)
    heads = range(HEADS)
    hk = [hk_ref[0, :, hs(h)].astype(F32) for h in heads]
    gk = [gk_ref[0, :, hs(h)].astype(F32) for h in heads]
    hout, gout = {}, {}
    _run_interleaved(
        _gdn_stages(gk, gk, [gv_ref[0, :, hs(h)].astype(F32) for h in heads],
                    [gcol[:, h:h + 1] for h in heads], [grow[h:h + 1, :] for h in heads],
                    [gcol[:, HEADS + h:HEADS + h + 1] for h in heads], None, gout,
                    need_o=False),
        _hgrn2_stages(hk, hk, [hv_ref[0, :, hs(h)].astype(F32) for h in heads],
                      [hg_ref[0, :, hs(h)] for h in heads], None, hout, need_o=False))
    for h in heads:
        sh_ref[h] = hout["s"][h]
        sg_ref[h] = gout["s"][h]


def _meta_state_call(hk, hv, hg, gk, gv, gcol, grow):
    state = jax.ShapeDtypeStruct((HEADS, DH, DH), F32)
    return pl.pallas_call(
        _meta_state_kernel,
        out_shape=[state, state],
        name="meta_state",
    )(hk, hv, hg, gk, gv, gcol, grow)


def _out_kernel(y_ref, x_ref, w_ref, fw_ref, o_ref):
    h = x_ref[...] + _dot(y_ref[...], w_ref[...])
    ms = jnp.mean(h * h, axis=-1, keepdims=True)
    o_ref[...] = h * lax.rsqrt(ms + EPS) * fw_ref[...]


def _out_call(y, x, w, fw, *, tm):
    R = x.shape[0]
    row_spec = pl.BlockSpec((tm, D_MODEL), lambda i: (i, 0))
    return pl.pallas_call(
        _out_kernel,
        grid=(R // tm,),
        in_specs=[row_spec, row_spec, pl.BlockSpec((2 * WIDTH, D_MODEL), lambda i: (0, 0)),
                  pl.BlockSpec((1, D_MODEL), lambda i: (0, 0))],
        out_specs=row_spec,
        out_shape=jax.ShapeDtypeStruct((R, D_MODEL), F32),
        compiler_params=pltpu.CompilerParams(
            dimension_semantics=("parallel",), vmem_limit_bytes=VMEM_LIMIT),
        name="out",
    )(y, x, w, fw)


def kernel(x, meta_tokens, norm_w, w_in, conv_w, hg_lb_logits, hg_norm_w, gdn_A_log,
           gdn_dt_bias, gdn_norm_w, w_out, final_norm_w):
    B, T, D = x.shape
    assert D == D_MODEL and norm_w.shape[0] == 1 and meta_tokens.shape == (N_META, D_MODEL)
    assert w_in.shape == (1, D_MODEL, 8 * WIDTH + 2 * HEADS)

    w = w_in[0]
    w = jnp.concatenate(
        [w, jnp.zeros((D_MODEL, AB_PAD - 2 * HEADS), w.dtype)], axis=1).astype(BF16)
    pad = lambda v: jnp.zeros((1, AB_PAD), F32).at[0, :HEADS].set(v.astype(F32))
    alog, dtb = pad(gdn_A_log[0]), pad(gdn_dt_bias[0])
    nw = norm_w[0].reshape(1, D_MODEL)
    cw = conv_w[0]
    proj = functools.partial(_proj_call, nw=nw, w=w, cw=cw, lbl=hg_lb_logits, alog=alog,
                             dtb=dtb)

    zero_tail = jnp.zeros((CONV_HIST, 3 * WIDTH), F32)
    m = proj(meta_tokens[None], tail=zero_tail, tm=N_META, chunk=N_META, emit_tail=True)
    _, mhk, mhv, mhg, _, _, mgk, mgv, mgcol, mgrow, tail = m
    sh0, sg0 = _meta_state_call(mhk, mhv, mhg, mgk, mgv, mgcol, mgrow)

    hq, hk, hv, hg, zg, gq, gk, gv, gcol, grow = proj(
        x, tail=tail, tm=512, chunk=CHUNK, emit_tail=False)
    y = _mixer_call(hq, hk, hv, hg, zg, gq, gk, gv, gcol, grow, sh0, sg0,
                    hg_norm_w[0].reshape(1, DH), gdn_norm_w[0].reshape(1, DH),
                    bb=4, tt=256, chunk=CHUNK)
    out = _out_call(y.reshape(B * T, 2 * WIDTH), x.reshape(B * T, D_MODEL),
                    w_out[0].astype(BF16), final_norm_w.reshape(1, D_MODEL), tm=512)
    return out.reshape(B, T, D_MODEL)
```

```python
import functools

import jax
import jax.numpy as jnp
from jax import lax
from jax.experimental import pallas as pl
from jax.experimental.pallas import tpu as pltpu

F32 = jnp.float32
BF16 = jnp.bfloat16

D_MODEL = 1024
N_META = 16
HEADS = 4
DH = 128
WIDTH = HEADS * DH
CONV_W = 4
EPS = 1e-6
AB_PAD = 128
IN_PAD = 8 * WIDTH + AB_PAD

CHUNK = 64
SUB = 16
SAFE_RANGE = 60.0
CONV_HIST = 8
CUM_ROWS = 256
ROW_BLOCK = 128
VMEM_LIMIT = 56 * 1024 * 1024


def _dot(a, b):
    return jnp.dot(a, b, preferred_element_type=F32)


def _dot_nt(a, b):
    return lax.dot_general(a, b, (((1,), (1,)), ((), ())), preferred_element_type=F32)


def _dot_tn(a, b):
    return lax.dot_general(a, b, (((0,), (0,)), ((), ())), preferred_element_type=F32)


def _split3(x):
    x1 = x.astype(BF16)
    r = x - x1.astype(F32)
    x2 = r.astype(BF16)
    x3 = (r - x2.astype(F32)).astype(BF16)
    return x1, x2, x3


def _silu(x):
    return x * jax.nn.sigmoid(x)


def _softplus(x):
    return jnp.maximum(x, 0.0) + jnp.log1p(jnp.exp(-jnp.abs(x)))


def _chunk_cumsum(x, chunk):
    rows = x.shape[0]
    blk = min(rows, CUM_ROWS)
    r = lax.broadcasted_iota(jnp.int32, (blk, blk), 0)
    c = lax.broadcasted_iota(jnp.int32, (blk, blk), 1)
    tri = jnp.where((r // chunk == c // chunk) & (c <= r), 1.0, 0.0).astype(BF16)
    outs = []
    for i in range(rows // blk):
        parts = _split3(x[i * blk:(i + 1) * blk])
        outs.append(_dot(tri, parts[0]) + _dot(tri, parts[1]) + _dot(tri, parts[2]))
    return outs[0] if len(outs) == 1 else jnp.concatenate(outs, axis=0)


def _proj_kernel(x_ref, nw_ref, w_ref, cw_ref, lbl_ref, alog_ref, dtb_ref, tail_ref,
                 hq_ref, hk_ref, hv_ref, hg_ref, zg_ref, gq_ref, gk_ref, gv_ref,
                 gcol_ref, grow_ref, *rest, tm, chunk, emit_tail):
    if emit_tail:
        tail_out_ref, cbuf, logf_buf, ab_buf = rest
    else:
        cbuf, logf_buf, ab_buf = rest
    j = pl.program_id(1)

    @pl.when(j == 0)
    def _():
        cbuf[0:CONV_HIST, :] = tail_ref[...]

    lbl = lbl_ref[...]
    e = jnp.exp(lbl - jnp.max(lbl, axis=0, keepdims=True))
    lb = e[0:1] / jnp.sum(e, axis=0, keepdims=True)

    rb = min(tm, ROW_BLOCK)
    for r0 in range(0, tm, rb):
        rows = slice(r0, r0 + rb)
        x = x_ref[0, rows, :]
        ms = jnp.mean(x * x, axis=-1, keepdims=True)
        u = (x * lax.rsqrt(ms + EPS) * nw_ref[...]).astype(BF16)

        def proj(a, b, u=u):
            return _dot(u, w_ref[:, a:b])

        cbuf[CONV_HIST + r0:CONV_HIST + r0 + rb, :] = proj(4 * WIDTH, 7 * WIDTH)
        xe = cbuf[r0:r0 + CONV_HIST + rb, :]
        acc = cw_ref[0:1, :] * xe
        for t in range(1, CONV_W):
            acc = pltpu.roll(acc, 1, 0) + cw_ref[t:t + 1, :] * xe
        qkv = _silu(acc[CONV_HIST:CONV_HIST + rb])
        for h in range(HEADS):
            q = qkv[:, h * DH:(h + 1) * DH]
            k = qkv[:, WIDTH + h * DH:WIDTH + (h + 1) * DH]
            q = q * lax.rsqrt(jnp.sum(q * q, axis=-1, keepdims=True) + EPS) * (DH ** -0.5)
            k = k * lax.rsqrt(jnp.sum(k * k, axis=-1, keepdims=True) + EPS)
            gq_ref[0, rows, h * DH:(h + 1) * DH] = q.astype(BF16)
            gk_ref[0, rows, h * DH:(h + 1) * DH] = k.astype(BF16)
        gv_ref[0, rows, :] = qkv[:, 2 * WIDTH:3 * WIDTH].astype(BF16)
        zg_ref[0, rows, WIDTH:2 * WIDTH] = _silu(proj(7 * WIDTH, 8 * WIDTH)).astype(BF16)

        hq_ref[0, rows, :] = _silu(proj(0, WIDTH)).astype(BF16)
        f = lb + (1.0 - lb) * jax.nn.sigmoid(proj(WIDTH, 2 * WIDTH))
        hk_ref[0, rows, :] = (1.0 - f).astype(BF16)
        logf_buf[rows, :] = jnp.log(f)
        hv_ref[0, rows, :] = proj(2 * WIDTH, 3 * WIDTH).astype(BF16)
        zg_ref[0, rows, 0:WIDTH] = _silu(proj(3 * WIDTH, 4 * WIDTH)).astype(BF16)

        ab_buf[rows, :] = proj(8 * WIDTH, 8 * WIDTH + AB_PAD)

    for r0 in range(0, tm, rb):
        rows = slice(r0, r0 + rb)
        hg_ref[0, rows, :] = _chunk_cumsum(logf_buf[rows, :], chunk)
        p = ab_buf[rows, :]
        lane = lax.broadcasted_iota(jnp.int32, p.shape, 1)
        g = -jnp.exp(alog_ref[...]) * _softplus(p + dtb_ref[...])
        gam = _chunk_cumsum(jnp.where(lane < HEADS, g, 0.0), chunk)
        comb = jnp.where(lane < HEADS, gam,
                         jnp.where(lane < 2 * HEADS, jax.nn.sigmoid(p), 0.0))
        gcol_ref[0, rows, :] = comb
        for c in range(rb // chunk):
            blk = comb[c * chunk:(c + 1) * chunk, :]
            grow_ref[0, r0 // chunk + c] = blk.T[0:2 * HEADS, :]

    hist = cbuf[tm:tm + CONV_HIST, :]
    cbuf[0:CONV_HIST, :] = hist
    if emit_tail:
        tail_out_ref[...] = hist


def _proj_call(x, nw, w, cw, lbl, alog, dtb, tail, *, tm, chunk, emit_tail):
    B, T, _ = x.shape
    nt = T // tm
    tok = lambda width, dt: jax.ShapeDtypeStruct((B, T, width), dt)
    tok_spec = lambda width: pl.BlockSpec((1, tm, width), lambda b, j: (b, j, 0))
    const = lambda shape: pl.BlockSpec(shape, lambda b, j: (0,) * len(shape))
    out_shape = [tok(WIDTH, BF16), tok(WIDTH, BF16), tok(WIDTH, BF16), tok(WIDTH, F32),
                 tok(2 * WIDTH, BF16), tok(WIDTH, BF16), tok(WIDTH, BF16), tok(WIDTH, BF16),
                 tok(AB_PAD, F32),
                 jax.ShapeDtypeStruct((B, T // chunk, 2 * HEADS, chunk), F32)]
    out_specs = [tok_spec(WIDTH), tok_spec(WIDTH), tok_spec(WIDTH), tok_spec(WIDTH),
                 tok_spec(2 * WIDTH), tok_spec(WIDTH), tok_spec(WIDTH), tok_spec(WIDTH),
                 tok_spec(AB_PAD),
                 pl.BlockSpec((1, tm // chunk, 2 * HEADS, chunk), lambda b, j: (b, j, 0, 0))]
    if emit_tail:
        out_shape.append(jax.ShapeDtypeStruct((CONV_HIST, 3 * WIDTH), F32))
        out_specs.append(const((CONV_HIST, 3 * WIDTH)))
    return pl.pallas_call(
        functools.partial(_proj_kernel, tm=tm, chunk=chunk, emit_tail=emit_tail),
        grid=(B, nt),
        in_specs=[tok_spec(D_MODEL), const((1, D_MODEL)), const((D_MODEL, IN_PAD)),
                  const((CONV_W, 3 * WIDTH)), const(lbl.shape), const((1, AB_PAD)),
                  const((1, AB_PAD)), const((CONV_HIST, 3 * WIDTH))],
        out_specs=out_specs,
        out_shape=out_shape,
        scratch_shapes=[pltpu.VMEM((tm + CONV_HIST, 3 * WIDTH), F32),
                        pltpu.VMEM((tm, WIDTH), F32), pltpu.VMEM((tm, AB_PAD), F32)],
        compiler_params=pltpu.CompilerParams(
            dimension_semantics=("parallel", "arbitrary"), vmem_limit_bytes=VMEM_LIMIT),
        name="proj_meta" if emit_tail else "proj",
    )(x, nw, w, cw, lbl, alog, dtb, tail)


def _run_interleaved(*gens):
    live = list(gens)
    while live:
        for g in list(live):
            try:
                next(g)
            except StopIteration:
                live.remove(g)


def _hgrn2_stages(q, k, v, G, st, out, *, need_o, bounded=False, idle=(0, 0)):
    n = len(q)
    C = q[0].shape[0]
    g_last = [G[i][C - 1:C, :] for i in range(n)]
    kk = [(k[i] * jnp.exp(g_last[i] - G[i])).astype(BF16) for i in range(n)]
    st_new = [_dot_tn(v[i].astype(BF16), kk[i]) for i in range(n)]
    if st is not None:
        st_new = [st_new[i] + st[i] * jnp.exp(g_last[i]) for i in range(n)]
    out["s"] = st_new
    if not need_o:
        return
    o_inter = None
    if st is not None:
        o_inter = [_dot_nt((q[i] * jnp.exp(G[i])).astype(BF16), st[i].astype(BF16))
                   for i in range(n)]
    yield
    for _ in range(idle[0]):
        yield
    row = lax.broadcasted_iota(jnp.int32, (C, C), 0)
    col = lax.broadcasted_iota(jnp.int32, (C, C), 1)
    if bounded:
        g_mid = [G[i][C // 2 - 1:C // 2, :] for i in range(n)]
        p = [_dot_nt((q[i] * jnp.exp(G[i] - g_mid[i])).astype(BF16),
                     (k[i] * jnp.exp(g_mid[i] - G[i])).astype(BF16)) for i in range(n)]
        yield
        for _ in range(idle[1]):
            yield
        o = [_dot(jnp.where(row >= col, p[i], 0.0).astype(BF16), v[i].astype(BF16))
             for i in range(n)]
        out["o"] = o if o_inter is None else [o[i] + o_inter[i] for i in range(n)]
        return
    nb = C // SUB
    a = [jnp.zeros((C, C), F32) for _ in range(n)]
    g_end = [[G[i][(j + 1) * SUB - 1:(j + 1) * SUB, :] for j in range(nb)] for i in range(n)]
    if nb > 1:
        p = []
        for i in range(n):
            g_ref = jnp.concatenate(
                [jnp.broadcast_to(g_end[i][j], (SUB, DH)) for j in range(nb)], axis=0)
            kp = (k[i] * jnp.exp(g_ref - G[i])).astype(BF16)
            qp = jnp.concatenate(
                [q[i] * jnp.exp(jnp.minimum(G[i] - g_end[i][j], 0.0)) for j in range(nb - 1)],
                axis=0).astype(BF16)
            p.append(_dot_nt(qp, kp))
        for i in range(n):
            for j in range(nb - 1):
                m = (col // SUB == j) & (row // SUB > j)
                a[i] = jnp.where(m, p[i][j * C:(j + 1) * C], a[i])
    yield
    for _ in range(idle[1]):
        yield
    rowb = lax.broadcasted_iota(jnp.int32, (SUB, C), 0)
    colb = lax.broadcasted_iota(jnp.int32, (SUB, C), 1)
    o = []
    for i in range(n):
        for b in range(nb):
            gb = G[i][b * SUB:(b + 1) * SUB]
            qb = q[i][b * SUB:(b + 1) * SUB]
            kb = k[i][b * SUB:(b + 1) * SUB]
            d = jnp.zeros((SUB, C), F32)
            for s in range(SUB):
                es = jnp.exp(jnp.minimum(gb - gb[s:s + 1], 0.0))
                cs = jnp.sum(qb * es * kb[s:s + 1], axis=-1, keepdims=True)
                d = jnp.where((colb == b * SUB + s) & (rowb >= s), cs, d)
            if nb == 1:
                a[i] = d
            else:
                dfull = jnp.concatenate(
                    [d if r == b else jnp.zeros((SUB, C), F32) for r in range(nb)], axis=0)
                a[i] = jnp.where((row // SUB == b) & (col // SUB == b), dfull, a[i])
        oi = _dot(a[i].astype(BF16), v[i].astype(BF16))
        o.append(oi if o_inter is None else oi + o_inter[i])
    out["o"] = o


def _gdn_stages(q, k, v, gam_c, gam_r, beta_c, s, out, *, need_o):
    n = len(q)
    C = q[0].shape[0]
    rng = range(n)
    row = lax.broadcasted_iota(jnp.int32, (C, C), 0)
    col = lax.broadcasted_iota(jnp.int32, (C, C), 1)
    kb = [k[i].astype(BF16) for i in rng]
    if need_o:
        pk = [_dot_nt(jnp.concatenate([q[i], k[i]], axis=0).astype(BF16), kb[i]) for i in rng]
        qk, kk = [x[:C] for x in pk], [x[C:] for x in pk]
    else:
        kk = [_dot_nt(kb[i], kb[i]) for i in rng]
    yield
    egam = [jnp.exp(gam_c[i]) for i in rng]
    qs = None
    if s is not None:
        sb = [s[i].astype(BF16) for i in rng]
        if need_o:
            qks = [_dot(jnp.concatenate([q[i] * egam[i], k[i] * egam[i]], axis=0).astype(BF16),
                        sb[i]) for i in rng]
            qs, ks = [x[:C] for x in qks], [x[C:] for x in qks]
        else:
            ks = [_dot((k[i] * egam[i]).astype(BF16), sb[i]) for i in rng]
        rhs = [beta_c[i] * (v[i] - ks[i]) for i in rng]
        yield
    else:
        rhs = [beta_c[i] * v[i] for i in rng]
    dec = [jnp.exp(jnp.minimum(gam_c[i] - gam_r[i], 0.0)) for i in rng]
    a = [jnp.where(row > col, beta_c[i] * kk[i] * dec[i], 0.0) for i in rng]
    eye = jnp.where(row == col, 1.0, 0.0)
    t = [eye - jnp.where(row // 2 == col // 2, a[i], 0.0) for i in rng]
    bs = 2
    while bs < C:
        lm = (row // (2 * bs) == col // (2 * bs)) & (row // bs > col // bs)
        tb = [t[i].astype(BF16) for i in rng]
        tl = [_dot(tb[i], jnp.where(lm, a[i], 0.0).astype(BF16)) for i in rng]
        yield
        t = [t[i] - _dot(tl[i].astype(BF16), tb[i]) for i in rng]
        yield
        bs *= 2
    ub = [_dot(t[i].astype(BF16), rhs[i].astype(BF16)).astype(BF16) for i in rng]
    yield
    g_last = [gam_c[i][C - 1:C, :] for i in rng]
    s_new = [_dot_tn((k[i] * jnp.exp(g_last[i] - gam_c[i])).astype(BF16), ub[i]) for i in rng]
    if s is not None:
        s_new = [s_new[i] + s[i] * jnp.exp(g_last[i]) for i in rng]
    out["s"] = s_new
    if need_o:
        o = [_dot(jnp.where(row >= col, qk[i] * dec[i], 0.0).astype(BF16), ub[i]) for i in rng]
        if qs is not None:
            o = [o[i] + qs[i] for i in rng]
        out["o"] = o


def _gated_norm(o, w, z):
    return o * lax.rsqrt(jnp.mean(o * o, axis=-1, keepdims=True) + EPS) * w * z


def _mixer_kernel(hq_ref, hk_ref, hv_ref, hg_ref, zg_ref, gq_ref, gk_ref, gv_ref,
                  gcol_ref, grow_ref, sh0_ref, sg0_ref, hw_ref, gw_ref,
                  y_ref, sh_ref, sg_ref, bounded_ref, *, bb, tt, chunk):
    j = pl.program_id(1)

    @pl.when(j == 0)
    def _():
        for b in range(bb):
            sh_ref[b] = sh0_ref[...]
            sg_ref[b] = sg0_ref[...]

    for c in range(tt // chunk):
        span = None
        for b in range(bb):
            g_mid = hg_ref[b, c * chunk + chunk // 2 - 1:c * chunk + chunk // 2, :]
            g_last = hg_ref[b, (c + 1) * chunk - 1:(c + 1) * chunk, :]
            s = jnp.maximum(-g_mid, g_mid - g_last)
            span = s if span is None else jnp.maximum(span, s)
        bounded_ref[c] = jnp.where(jnp.max(span) <= SAFE_RANGE, 1, 0)

    hw = hw_ref[...]
    gw = gw_ref[...]
    probs = [(b, h) for b in range(bb) for h in range(HEADS)]
    hs = lambda h: slice(h * DH, (h + 1) * DH)

    def step(c, bounded):
        t0 = pl.multiple_of(c * chunk, chunk)
        rows = pl.ds(t0, chunk)
        ld = lambda ref: [ref[b, rows, hs(h)].astype(F32) for b, h in probs]
        gcol = [gcol_ref[b, rows, :] for b in range(bb)]
        grow = [grow_ref[b, c] for b in range(bb)]
        hout, gout = {}, {}
        _run_interleaved(
            _gdn_stages(ld(gq_ref), ld(gk_ref), ld(gv_ref),
                        [gcol[b][:, h:h + 1] for b, h in probs],
                        [grow[b][h:h + 1, :] for b, h in probs],
                        [gcol[b][:, HEADS + h:HEADS + h + 1] for b, h in probs],
                        [sg_ref[b, h] for b, h in probs], gout, need_o=True),
            _hgrn2_stages(ld(hq_ref), ld(hk_ref), ld(hv_ref), ld(hg_ref),
                          [sh_ref[b, h] for b, h in probs], hout, need_o=True,
                          bounded=bounded, idle=(1, 8)))
        for i, (b, h) in enumerate(probs):
            sh_ref[b, h] = hout["s"][i]
            sg_ref[b, h] = gout["s"][i]
            y_ref[b, rows, hs(h)] = _gated_norm(
                hout["o"][i], hw, zg_ref[b, rows, hs(h)].astype(F32)).astype(BF16)
            ys = slice(WIDTH + h * DH, WIDTH + (h + 1) * DH)
            y_ref[b, rows, ys] = _gated_norm(
                gout["o"][i], gw, zg_ref[b, rows, ys].astype(F32)).astype(BF16)

    def body(c, carry):
        is_bounded = bounded_ref[c] == 1
        pl.when(is_bounded)(lambda: step(c, True))
        pl.when(jnp.logical_not(is_bounded))(lambda: step(c, False))
        return carry

    lax.fori_loop(0, tt // chunk, body, 0)


def _mixer_call(hq, hk, hv, hg, zg, gq, gk, gv, gcol, grow, sh0, sg0, hw, gw, *, bb, tt, chunk):
    B, T, _ = hq.shape
    tok_spec = lambda width: pl.BlockSpec((bb, tt, width), lambda i, j: (i, j, 0))
    const = lambda shape: pl.BlockSpec(shape, lambda i, j: (0,) * len(shape))
    return pl.pallas_call(
        functools.partial(_mixer_kernel, bb=bb, tt=tt, chunk=chunk),
        grid=(B // bb, T // tt),
        in_specs=[tok_spec(WIDTH), tok_spec(WIDTH), tok_spec(WIDTH), tok_spec(WIDTH),
                  tok_spec(2 * WIDTH), tok_spec(WIDTH), tok_spec(WIDTH), tok_spec(WIDTH),
                  tok_spec(AB_PAD),
                  pl.BlockSpec((bb, tt // chunk, 2 * HEADS, chunk), lambda i, j: (i, j, 0, 0)),
                  const((HEADS, DH, DH)), const((HEADS, DH, DH)),
                  const((1, DH)), const((1, DH))],
        out_specs=tok_spec(2 * WIDTH),
        out_shape=jax.ShapeDtypeStruct((B, T, 2 * WIDTH), BF16),
        scratch_shapes=[pltpu.VMEM((bb, HEADS, DH, DH), F32),
                        pltpu.VMEM((bb, HEADS, DH, DH), F32),
                        pltpu.SMEM((tt // chunk,), jnp.int32)],
        compiler_params=pltpu.CompilerParams(
            dimension_semantics=("parallel", "arbitrary"), vmem_limit_bytes=VMEM_LIMIT),
        name="mixer",
    )(hq, hk, hv, hg, zg, gq, gk, gv, gcol, grow, sh0, sg0, hw, gw)


def _meta_state_kernel(hk_ref, hv_ref, hg_ref, gk_ref, gv_ref, gcol_ref, grow_ref,
                       sh_ref, sg_ref):
    gcol = gcol_ref[0]
    grow = grow_ref[0, 0]
    hs = lambda h: slice(h * DH, (h + 1) * DH)
    heads = range(HEADS)
    hk = [hk_ref[0, :, hs(h)].astype(F32) for h in heads]
    gk = [gk_ref[0, :, hs(h)].astype(F32) for h in heads]
    hout, gout = {}, {}
    _run_interleaved(
        _gdn_stages(gk, gk, [gv_ref[0, :, hs(h)].astype(F32) for h in heads],
                    [gcol[:, h:h + 1] for h in heads], [grow[h:h + 1, :] for h in heads],
                    [gcol[:, HEADS + h:HEADS + h + 1] for h in heads], None, gout,
                    need_o=False),
        _hgrn2_stages(hk, hk, [hv_ref[0, :, hs(h)].astype(F32) for h in heads],
                      [hg_ref[0, :, hs(h)] for h in heads], None, hout, need_o=False))
    for h in heads:
        sh_ref[h] = hout["s"][h]
        sg_ref[h] = gout["s"][h]


def _meta_state_call(hk, hv, hg, gk, gv, gcol, grow):
    state = jax.ShapeDtypeStruct((HEADS, DH, DH), F32)
    return pl.pallas_call(
        _meta_state_kernel,
        out_shape=[state, state],
        name="meta_state",
    )(hk, hv, hg, gk, gv, gcol, grow)


def _out_kernel(y_ref, x_ref, w_ref, fw_ref, o_ref):
    h = x_ref[...] + _dot(y_ref[...], w_ref[...])
    ms = jnp.mean(h * h, axis=-1, keepdims=True)
    o_ref[...] = h * lax.rsqrt(ms + EPS) * fw_ref[...]


def _out_call(y, x, w, fw, *, tm):
    R = x.shape[0]
    row_spec = pl.BlockSpec((tm, D_MODEL), lambda i: (i, 0))
    return pl.pallas_call(
        _out_kernel,
        grid=(R // tm,),
        in_specs=[row_spec, row_spec, pl.BlockSpec((2 * WIDTH, D_MODEL), lambda i: (0, 0)),
                  pl.BlockSpec((1, D_MODEL), lambda i: (0, 0))],
        out_specs=row_spec,
        out_shape=jax.ShapeDtypeStruct((R, D_MODEL), F32),
        compiler_params=pltpu.CompilerParams(
            dimension_semantics=("parallel",), vmem_limit_bytes=VMEM_LIMIT),
        name="out",
    )(y, x, w, fw)


def kernel(x, meta_tokens, norm_w, w_in, conv_w, hg_lb_logits, hg_norm_w, gdn_A_log,
           gdn_dt_bias, gdn_norm_w, w_out, final_norm_w):
    B, T, D = x.shape
    assert D == D_MODEL and norm_w.shape[0] == 1 and meta_tokens.shape == (N_META, D_MODEL)
    assert w_in.shape == (1, D_MODEL, 8 * WIDTH + 2 * HEADS)

    w = w_in[0]
    w = jnp.concatenate(
        [w, jnp.zeros((D_MODEL, AB_PAD - 2 * HEADS), w.dtype)], axis=1).astype(BF16)
    pad = lambda v: jnp.zeros((1, AB_PAD), F32).at[0, :HEADS].set(v.astype(F32))
    alog, dtb = pad(gdn_A_log[0]), pad(gdn_dt_bias[0])
    nw = norm_w[0].reshape(1, D_MODEL)
    cw = conv_w[0]
    proj = functools.partial(_proj_call, nw=nw, w=w, cw=cw, lbl=hg_lb_logits, alog=alog,
                             dtb=dtb)

    zero_tail = jnp.zeros((CONV_HIST, 3 * WIDTH), F32)
    m = proj(meta_tokens[None], tail=zero_tail, tm=N_META, chunk=N_META, emit_tail=True)
    _, mhk, mhv, mhg, _, _, mgk, mgv, mgcol, mgrow, tail = m
    sh0, sg0 = _meta_state_call(mhk, mhv, mhg, mgk, mgv, mgcol, mgrow)

    hq, hk, hv, hg, zg, gq, gk, gv, gcol, grow = proj(
        x, tail=tail, tm=512, chunk=CHUNK, emit_tail=False)
    y = _mixer_call(hq, hk, hv, hg, zg, gq, gk, gv, gcol, grow, sh0, sg0,
                    hg_norm_w[0].reshape(1, DH), gdn_norm_w[0].reshape(1, DH),
                    bb=4, tt=256, chunk=CHUNK)
    out = _out_call(y.reshape(B * T, 2 * WIDTH), x.reshape(B * T, D_MODEL),
                    w_out[0].astype(BF16), final_norm_w.reshape(1, D_MODEL), tm=512)
    return out.reshape(B, T, D_MODEL)
```

```python
import functools

import jax
import jax.numpy as jnp
from jax import lax
from jax.experimental import pallas as pl
from jax.experimental.pallas import tpu as pltpu

F32 = jnp.float32
BF16 = jnp.bfloat16

D_MODEL = 1024
N_META = 16
HEADS = 4
DH = 128
WIDTH = HEADS * DH
CONV_W = 4
EPS = 1e-6
AB_PAD = 128
IN_PAD = 8 * WIDTH + AB_PAD

CHUNK = 64
SUB = 16
SAFE_RANGE = 60.0
CONV_HIST = 8
CUM_ROWS = 256
ROW_BLOCK = 128
RECUR_GAP = 3
VMEM_LIMIT = 56 * 1024 * 1024


def _dot(a, b):
    return jnp.dot(a, b, preferred_element_type=F32)


def _dot_nt(a, b):
    return lax.dot_general(a, b, (((1,), (1,)), ((), ())), preferred_element_type=F32)


def _dot_tn(a, b):
    return lax.dot_general(a, b, (((0,), (0,)), ((), ())), preferred_element_type=F32)


def _split2(x):
    x1 = x.astype(BF16)
    x2 = (x - x1.astype(F32)).astype(BF16)
    return x1, x2


def _silu(x):
    return x * jax.nn.sigmoid(x)


def _softplus(x):
    return jnp.maximum(x, 0.0) + jnp.log1p(jnp.exp(-jnp.abs(x)))


def _chunk_cumsum(x, chunk):
    rows = x.shape[0]
    blk = min(rows, CUM_ROWS)
    r = lax.broadcasted_iota(jnp.int32, (blk, blk), 0)
    c = lax.broadcasted_iota(jnp.int32, (blk, blk), 1)
    tri = jnp.where((r // chunk == c // chunk) & (c <= r), 1.0, 0.0).astype(BF16)
    outs = []
    for i in range(rows // blk):
        hi, lo = _split2(x[i * blk:(i + 1) * blk])
        outs.append(_dot(tri, hi) + _dot(tri, lo))
    return outs[0] if len(outs) == 1 else jnp.concatenate(outs, axis=0)


def _proj_kernel(x_ref, nw_ref, w_ref, cw_ref, lbl_ref, alog_ref, dtb_ref, tail_ref,
                 hq_ref, hk_ref, hv_ref, hg_ref, zg_ref, gq_ref, gk_ref, gv_ref,
                 gcol_ref, grow_ref, *rest, tm, chunk, emit_tail):
    if emit_tail:
        tail_out_ref, cbuf, logf_buf, ab_buf = rest
    else:
        cbuf, logf_buf, ab_buf = rest
    j = pl.program_id(1)

    @pl.when(j == 0)
    def _():
        cbuf[0:CONV_HIST, :] = tail_ref[...]

    lbl = lbl_ref[...]
    e = jnp.exp(lbl - jnp.max(lbl, axis=0, keepdims=True))
    lb = e[0:1] / jnp.sum(e, axis=0, keepdims=True)

    rb = min(tm, ROW_BLOCK)
    for r0 in range(0, tm, rb):
        rows = slice(r0, r0 + rb)
        x = x_ref[0, rows, :]
        ms = jnp.mean(x * x, axis=-1, keepdims=True)
        u = (x * lax.rsqrt(ms + EPS) * nw_ref[...]).astype(BF16)

        def proj(a, b, u=u):
            return _dot(u, w_ref[:, a:b])

        cbuf[CONV_HIST + r0:CONV_HIST + r0 + rb, :] = proj(4 * WIDTH, 7 * WIDTH)
        xe = cbuf[r0:r0 + CONV_HIST + rb, :]
        acc = cw_ref[0:1, :] * xe
        for t in range(1, CONV_W):
            acc = pltpu.roll(acc, 1, 0) + cw_ref[t:t + 1, :] * xe
        qkv = _silu(acc[CONV_HIST:CONV_HIST + rb])
        for h in range(HEADS):
            q = qkv[:, h * DH:(h + 1) * DH]
            k = qkv[:, WIDTH + h * DH:WIDTH + (h + 1) * DH]
            q = q * lax.rsqrt(jnp.sum(q * q, axis=-1, keepdims=True) + EPS) * (DH ** -0.5)
            k = k * lax.rsqrt(jnp.sum(k * k, axis=-1, keepdims=True) + EPS)
            gq_ref[0, rows, h * DH:(h + 1) * DH] = q.astype(BF16)
            gk_ref[0, rows, h * DH:(h + 1) * DH] = k.astype(BF16)
        gv_ref[0, rows, :] = qkv[:, 2 * WIDTH:3 * WIDTH].astype(BF16)
        zg_ref[0, rows, WIDTH:2 * WIDTH] = _silu(proj(7 * WIDTH, 8 * WIDTH)).astype(BF16)

        hq_ref[0, rows, :] = _silu(proj(0, WIDTH)).astype(BF16)
        f = lb + (1.0 - lb) * jax.nn.sigmoid(proj(WIDTH, 2 * WIDTH))
        hk_ref[0, rows, :] = (1.0 - f).astype(BF16)
        logf_buf[rows, :] = jnp.log(f)
        hv_ref[0, rows, :] = proj(2 * WIDTH, 3 * WIDTH).astype(BF16)
        zg_ref[0, rows, 0:WIDTH] = _silu(proj(3 * WIDTH, 4 * WIDTH)).astype(BF16)

        ab_buf[rows, :] = proj(8 * WIDTH, 8 * WIDTH + AB_PAD)

    for r0 in range(0, tm, rb):
        rows = slice(r0, r0 + rb)
        hg_ref[0, rows, :] = _chunk_cumsum(logf_buf[rows, :], chunk)
        p = ab_buf[rows, :]
        lane = lax.broadcasted_iota(jnp.int32, p.shape, 1)
        g = -jnp.exp(alog_ref[...]) * _softplus(p + dtb_ref[...])
        gam = _chunk_cumsum(jnp.where(lane < HEADS, g, 0.0), chunk)
        comb = jnp.where(lane < HEADS, gam,
                         jnp.where(lane < 2 * HEADS, jax.nn.sigmoid(p), 0.0))
        gcol_ref[0, rows, :] = comb
        for c in range(rb // chunk):
            blk = comb[c * chunk:(c + 1) * chunk, :]
            grow_ref[0, r0 // chunk + c] = blk.T[0:2 * HEADS, :]

    hist = cbuf[tm:tm + CONV_HIST, :]
    cbuf[0:CONV_HIST, :] = hist
    if emit_tail:
        tail_out_ref[...] = hist


def _proj_call(x, nw, w, cw, lbl, alog, dtb, tail, *, tm, chunk, emit_tail):
    B, T, _ = x.shape
    nt = T // tm
    tok = lambda width, dt: jax.ShapeDtypeStruct((B, T, width), dt)
    tok_spec = lambda width: pl.BlockSpec((1, tm, width), lambda b, j: (b, j, 0))
    const = lambda shape: pl.BlockSpec(shape, lambda b, j: (0,) * len(shape))
    out_shape = [tok(WIDTH, BF16), tok(WIDTH, BF16), tok(WIDTH, BF16), tok(WIDTH, F32),
                 tok(2 * WIDTH, BF16), tok(WIDTH, BF16), tok(WIDTH, BF16), tok(WIDTH, BF16),
                 tok(AB_PAD, F32),
                 jax.ShapeDtypeStruct((B, T // chunk, 2 * HEADS, chunk), F32)]
    out_specs = [tok_spec(WIDTH), tok_spec(WIDTH), tok_spec(WIDTH), tok_spec(WIDTH),
                 tok_spec(2 * WIDTH), tok_spec(WIDTH), tok_spec(WIDTH), tok_spec(WIDTH),
                 tok_spec(AB_PAD),
                 pl.BlockSpec((1, tm // chunk, 2 * HEADS, chunk), lambda b, j: (b, j, 0, 0))]
    if emit_tail:
        out_shape.append(jax.ShapeDtypeStruct((CONV_HIST, 3 * WIDTH), F32))
        out_specs.append(const((CONV_HIST, 3 * WIDTH)))
    return pl.pallas_call(
        functools.partial(_proj_kernel, tm=tm, chunk=chunk, emit_tail=emit_tail),
        grid=(B, nt),
        in_specs=[tok_spec(D_MODEL), const((1, D_MODEL)), const((D_MODEL, IN_PAD)),
                  const((CONV_W, 3 * WIDTH)), const(lbl.shape), const((1, AB_PAD)),
                  const((1, AB_PAD)), const((CONV_HIST, 3 * WIDTH))],
        out_specs=out_specs,
        out_shape=out_shape,
        scratch_shapes=[pltpu.VMEM((tm + CONV_HIST, 3 * WIDTH), F32),
                        pltpu.VMEM((tm, WIDTH), F32), pltpu.VMEM((tm, AB_PAD), F32)],
        compiler_params=pltpu.CompilerParams(
            dimension_semantics=("parallel", "arbitrary"), vmem_limit_bytes=VMEM_LIMIT),
        name="proj_meta" if emit_tail else "proj",
    )(x, nw, w, cw, lbl, alog, dtb, tail)


def _run_interleaved(*gens):
    live = list(gens)
    while live:
        for g in list(live):
            try:
                next(g)
            except StopIteration:
                live.remove(g)


def _scaled(x, decay):
    return x * decay.astype(BF16)


def _spread(gen, gap):
    for _ in gen:
        yield
        for _ in range(gap):
            yield


def _hgrn2_intra_stages(q, k, v, G, out, *, bounded):
    n = len(q)
    C = q[0].shape[0]
    row = lax.broadcasted_iota(jnp.int32, (C, C), 0)
    col = lax.broadcasted_iota(jnp.int32, (C, C), 1)
    if bounded:
        g_mid = [G[i][C // 2 - 1:C // 2, :] for i in range(n)]
        p = [_dot_nt(_scaled(q[i], jnp.exp(G[i] - g_mid[i])),
                     _scaled(k[i], jnp.exp(g_mid[i] - G[i]))) for i in range(n)]
        yield
        out["o"] = [_dot(jnp.where(row >= col, p[i], 0.0).astype(BF16), v[i]) for i in range(n)]
        return
    nb = C // SUB
    qf = [q[i].astype(F32) for i in range(n)]
    kf = [k[i].astype(F32) for i in range(n)]
    a = [jnp.zeros((C, C), F32) for _ in range(n)]
    g_end = [[G[i][(j + 1) * SUB - 1:(j + 1) * SUB, :] for j in range(nb)] for i in range(n)]
    if nb > 1:
        p = []
        for i in range(n):
            g_ref = jnp.concatenate(
                [jnp.broadcast_to(g_end[i][j], (SUB, DH)) for j in range(nb)], axis=0)
            kp = (kf[i] * jnp.exp(g_ref - G[i])).astype(BF16)
            qp = jnp.concatenate(
                [qf[i] * jnp.exp(jnp.minimum(G[i] - g_end[i][j], 0.0)) for j in range(nb - 1)],
                axis=0).astype(BF16)
            p.append(_dot_nt(qp, kp))
        yield
        for i in range(n):
            for j in range(nb - 1):
                m = (col // SUB == j) & (row // SUB > j)
                a[i] = jnp.where(m, p[i][j * C:(j + 1) * C], a[i])
    rowb = lax.broadcasted_iota(jnp.int32, (SUB, C), 0)
    colb = lax.broadcasted_iota(jnp.int32, (SUB, C), 1)
    o = []
    for i in range(n):
        for b in range(nb):
            gb = G[i][b * SUB:(b + 1) * SUB]
            qb = qf[i][b * SUB:(b + 1) * SUB]
            kb = kf[i][b * SUB:(b + 1) * SUB]
            d = jnp.zeros((SUB, C), F32)
            for s in range(SUB):
                es = jnp.exp(jnp.minimum(gb - gb[s:s + 1], 0.0))
                cs = jnp.sum(qb * es * kb[s:s + 1], axis=-1, keepdims=True)
                d = jnp.where((colb == b * SUB + s) & (rowb >= s), cs, d)
            if nb == 1:
                a[i] = d
            else:
                dfull = jnp.concatenate(
                    [d if r == b else jnp.zeros((SUB, C), F32) for r in range(nb)], axis=0)
                a[i] = jnp.where((row // SUB == b) & (col // SUB == b), dfull, a[i])
        o.append(_dot(a[i].astype(BF16), v[i]))
    out["o"] = o


def _hgrn2_state_stages(q, k, v, G, st, o_intra, out):
    n = len(k)
    C = k[0].shape[0]
    g_last = [G[i][C - 1:C, :] for i in range(n)]
    kk = [_scaled(k[i], jnp.exp(g_last[i] - G[i])) for i in range(n)]
    st_new = [_dot_tn(v[i], kk[i]) for i in range(n)]
    if st is not None:
        st_new = [st_new[i] + st[i] * jnp.exp(g_last[i]) for i in range(n)]
    out["s"] = st_new
    yield
    if o_intra is not None:
        out["o"] = [o_intra[i] + _dot_nt(_scaled(q[i], jnp.exp(G[i])), st[i].astype(BF16))
                    for i in range(n)]


def _gdn_prep_stages(q, k, gam_c, gam_r, beta_c, out, *, need_o):
    n = len(k)
    C = k[0].shape[0]
    rng = range(n)
    row = lax.broadcasted_iota(jnp.int32, (C, C), 0)
    col = lax.broadcasted_iota(jnp.int32, (C, C), 1)
    if need_o:
        pk = [_dot_nt(jnp.concatenate([q[i], k[i]], axis=0), k[i]) for i in rng]
        qk, kk = [x[:C] for x in pk], [x[C:] for x in pk]
    else:
        kk = [_dot_nt(k[i], k[i]) for i in rng]
    yield
    dec = [jnp.exp(jnp.minimum(gam_c[i] - gam_r[i], 0.0)) for i in rng]
    if need_o:
        out["qkd"] = [jnp.where(row >= col, qk[i] * dec[i], 0.0).astype(BF16) for i in rng]
    a = [jnp.where(row > col, beta_c[i] * kk[i] * dec[i], 0.0) for i in rng]
    eye = jnp.where(row == col, 1.0, 0.0)
    t = [eye - jnp.where(row // 2 == col // 2, a[i], 0.0) for i in rng]
    bs = 2
    while bs < C:
        lm = (row // (2 * bs) == col // (2 * bs)) & (row // bs > col // bs)
        tb = [t[i].astype(BF16) for i in rng]
        tl = [_dot(tb[i], jnp.where(lm, a[i], 0.0).astype(BF16)) for i in rng]
        yield
        t = [t[i] - _dot(tl[i].astype(BF16), tb[i]) for i in rng]
        yield
        bs *= 2
    out["t"] = [t[i].astype(BF16) for i in rng]


def _gdn_state_stages(q, k, v, gam_c, beta_c, t, qkd, s, out):
    n = len(k)
    C = k[0].shape[0]
    rng = range(n)
    vf = [v[i].astype(F32) for i in rng]
    qs = None
    if s is not None:
        egam = [jnp.exp(gam_c[i]) for i in rng]
        sb = [s[i].astype(BF16) for i in rng]
        if qkd is not None:
            qks = [_dot(jnp.concatenate([_scaled(q[i], egam[i]), _scaled(k[i], egam[i])],
                                        axis=0), sb[i]) for i in rng]
            qs, ks = [x[:C] for x in qks], [x[C:] for x in qks]
        else:
            ks = [_dot(_scaled(k[i], egam[i]), sb[i]) for i in rng]
        yield
        rhs = [beta_c[i] * (vf[i] - ks[i]) for i in rng]
    else:
        rhs = [beta_c[i] * vf[i] for i in rng]
    ub = [_dot(t[i], rhs[i].astype(BF16)).astype(BF16) for i in rng]
    yield
    g_last = [gam_c[i][C - 1:C, :] for i in rng]
    s_new = [_dot_tn(_scaled(k[i], jnp.exp(g_last[i] - gam_c[i])), ub[i]) for i in rng]
    if s is not None:
        s_new = [s_new[i] + s[i] * jnp.exp(g_last[i]) for i in rng]
    out["s"] = s_new
    if qkd is not None:
        o = [_dot(qkd[i], ub[i]) for i in rng]
        out["o"] = o if qs is None else [o[i] + qs[i] for i in rng]


def _gated_norm(o, w, z):
    return o * lax.rsqrt(jnp.mean(o * o, axis=-1, keepdims=True) + EPS) * w * z


def _mixer_kernel(hq_ref, hk_ref, hv_ref, hg_ref, zg_ref, gq_ref, gk_ref, gv_ref,
                  gcol_ref, grow_ref, sh0_ref, sg0_ref, hw_ref, gw_ref,
                  y_ref, sh_ref, sg_ref, t_buf, qkd_buf, oi_buf, bounded_ref, *, bb, tt, chunk):
    j = pl.program_id(1)
    nchunks = tt // chunk

    @pl.when(j == 0)
    def _():
        for b in range(bb):
            sh_ref[b] = sh0_ref[...]
            sg_ref[b] = sg0_ref[...]

    for c in range(nchunks):
        span = None
        for b in range(bb):
            g_mid = hg_ref[b, c * chunk + chunk // 2 - 1:c * chunk + chunk // 2, :]
            g_last = hg_ref[b, (c + 1) * chunk - 1:(c + 1) * chunk, :]
            s = jnp.maximum(-g_mid, g_mid - g_last)
            span = s if span is None else jnp.maximum(span, s)
        bounded_ref[c] = jnp.where(jnp.max(span) <= SAFE_RANGE, 1, 0)

    hw = hw_ref[...]
    gw = gw_ref[...]
    probs = [(b, h) for b in range(bb) for h in range(HEADS)]
    hs = lambda h: slice(h * DH, (h + 1) * DH)

    def chunk_rows(c):
        return pl.ds(pl.multiple_of(c * chunk, chunk), chunk)

    def prep_stages(c, pout, hout, bounded):
        rows = chunk_rows(c)
        ld = lambda ref: [ref[b, rows, hs(h)] for b, h in probs]
        gcol = [gcol_ref[b, rows, :] for b in range(bb)]
        grow = [grow_ref[b, c] for b in range(bb)]
        return (_gdn_prep_stages(ld(gq_ref), ld(gk_ref),
                                 [gcol[b][:, h:h + 1] for b, h in probs],
                                 [grow[b][h:h + 1, :] for b, h in probs],
                                 [gcol[b][:, HEADS + h:HEADS + h + 1] for b, h in probs],
                                 pout, need_o=True),
                _hgrn2_intra_stages(ld(hq_ref), ld(hk_ref), ld(hv_ref), ld(hg_ref), hout,
                                    bounded=bounded))

    def store_prep(c, pout, hout):
        for i in range(len(probs)):
            t_buf[c, i] = pout["t"][i]
            qkd_buf[c, i] = pout["qkd"][i]
            oi_buf[c, i] = hout["o"][i]

    def recur_stages(c, gout, hout):
        rows = chunk_rows(c)
        ld = lambda ref: [ref[b, rows, hs(h)] for b, h in probs]
        gcol = [gcol_ref[b, rows, :] for b in range(bb)]
        n = len(probs)
        return (_gdn_state_stages(ld(gq_ref), ld(gk_ref), ld(gv_ref),
                                  [gcol[b][:, h:h + 1] for b, h in probs],
                                  [gcol[b][:, HEADS + h:HEADS + h + 1] for b, h in probs],
                                  [t_buf[c, i] for i in range(n)],
                                  [qkd_buf[c, i] for i in range(n)],
                                  [sg_ref[b, h] for b, h in probs], gout),
                _hgrn2_state_stages(ld(hq_ref), ld(hk_ref), ld(hv_ref), ld(hg_ref),
                                    [sh_ref[b, h] for b, h in probs],
                                    [oi_buf[c, i] for i in range(n)], hout))

    def store_recur(c, gout, hout):
        rows = chunk_rows(c)
        for i, (b, h) in enumerate(probs):
            sh_ref[b, h] = hout["s"][i]
            sg_ref[b, h] = gout["s"][i]
            y_ref[b, rows, hs(h)] = _gated_norm(
                hout["o"][i], hw, zg_ref[b, rows, hs(h)].astype(F32)).astype(BF16)
            ys = slice(WIDTH + h * DH, WIDTH + (h + 1) * DH)
            y_ref[b, rows, ys] = _gated_norm(
                gout["o"][i], gw, zg_ref[b, rows, ys].astype(F32)).astype(BF16)

    def prep_only(c, bounded):
        pout, hout = {}, {}
        _run_interleaved(*prep_stages(c, pout, hout, bounded))
        store_prep(c, pout, hout)

    def recur_only(c):
        gout, hout = {}, {}
        _run_interleaved(*recur_stages(c, gout, hout))
        store_recur(c, gout, hout)

    def pipelined_step():
        prep_only(0, True)

        def body(c, carry):
            pout, phout, gout, hout = {}, {}, {}, {}
            gprep, hprep = prep_stages(c + 1, pout, phout, True)
            gstate, hstate = recur_stages(c, gout, hout)
            _run_interleaved(gprep, _spread(gstate, RECUR_GAP), _spread(hstate, RECUR_GAP),
                             hprep)
            store_prep(c + 1, pout, phout)
            store_recur(c, gout, hout)
            return carry

        lax.fori_loop(0, nchunks - 1, body, 0)
        recur_only(nchunks - 1)

    def plain_step():
        def body(c, carry):
            prep_only(c, False)
            recur_only(c)
            return carry

        lax.fori_loop(0, nchunks, body, 0)

    all_bounded = bounded_ref[0] == 1
    for c in range(1, nchunks):
        all_bounded = jnp.logical_and(all_bounded, bounded_ref[c] == 1)
    pl.when(all_bounded)(pipelined_step)
    pl.when(jnp.logical_not(all_bounded))(plain_step)


def _mixer_call(hq, hk, hv, hg, zg, gq, gk, gv, gcol, grow, sh0, sg0, hw, gw, *, bb, tt, chunk):
    B, T, _ = hq.shape
    nchunks = tt // chunk
    tok_spec = lambda width: pl.BlockSpec((bb, tt, width), lambda i, j: (i, j, 0))
    const = lambda shape: pl.BlockSpec(shape, lambda i, j: (0,) * len(shape))
    return pl.pallas_call(
        functools.partial(_mixer_kernel, bb=bb, tt=tt, chunk=chunk),
        grid=(B // bb, T // tt),
        in_specs=[tok_spec(WIDTH), tok_spec(WIDTH), tok_spec(WIDTH), tok_spec(WIDTH),
                  tok_spec(2 * WIDTH), tok_spec(WIDTH), tok_spec(WIDTH), tok_spec(WIDTH),
                  tok_spec(AB_PAD),
                  pl.BlockSpec((bb, nchunks, 2 * HEADS, chunk), lambda i, j: (i, j, 0, 0)),
                  const((HEADS, DH, DH)), const((HEADS, DH, DH)),
                  const((1, DH)), const((1, DH))],
        out_specs=tok_spec(2 * WIDTH),
        out_shape=jax.ShapeDtypeStruct((B, T, 2 * WIDTH), BF16),
        scratch_shapes=[pltpu.VMEM((bb, HEADS, DH, DH), F32),
                        pltpu.VMEM((bb, HEADS, DH, DH), F32),
                        pltpu.VMEM((nchunks, bb * HEADS, chunk, chunk), BF16),
                        pltpu.VMEM((nchunks, bb * HEADS, chunk, chunk), BF16),
                        pltpu.VMEM((nchunks, bb * HEADS, chunk, DH), F32),
                        pltpu.SMEM((nchunks,), jnp.int32)],
        compiler_params=pltpu.CompilerParams(
            dimension_semantics=("parallel", "arbitrary"), vmem_limit_bytes=VMEM_LIMIT),
        name="mixer",
    )(hq, hk, hv, hg, zg, gq, gk, gv, gcol, grow, sh0, sg0, hw, gw)


def _meta_state_kernel(hk_ref, hv_ref, hg_ref, gk_ref, gv_ref, gcol_ref, grow_ref,
                       sh_ref, sg_ref):
    gcol = gcol_ref[0]
    grow = grow_ref[0, 0]
    hs = lambda h: slice(h * DH, (h + 1) * DH)
    heads = range(HEADS)
    hk = [hk_ref[0, :, hs(h)] for h in heads]
    gk = [gk_ref[0, :, hs(h)] for h in heads]
    gam_c = [gcol[:, h:h + 1] for h in heads]
    beta_c = [gcol[:, HEADS + h:HEADS + h + 1] for h in heads]
    prep, hout, gout = {}, {}, {}
    _run_interleaved(
        _gdn_prep_stages(None, gk, gam_c, [grow[h:h + 1, :] for h in heads], beta_c, prep,
                         need_o=False),
        _hgrn2_state_stages(None, hk, [hv_ref[0, :, hs(h)] for h in heads],
                            [hg_ref[0, :, hs(h)] for h in heads], None, None, hout))
    _run_interleaved(
        _gdn_state_stages(None, gk, [gv_ref[0, :, hs(h)] for h in heads], gam_c, beta_c,
                          prep["t"], None, None, gout))
    for h in heads:
        sh_ref[h] = hout["s"][h]
        sg_ref[h] = gout["s"][h]


def _meta_state_call(hk, hv, hg, gk, gv, gcol, grow):
    state = jax.ShapeDtypeStruct((HEADS, DH, DH), F32)
    return pl.pallas_call(
        _meta_state_kernel,
        out_shape=[state, state],
        name="meta_state",
    )(hk, hv, hg, gk, gv, gcol, grow)


def _out_kernel(y_ref, x_ref, w_ref, fw_ref, o_ref):
    h = x_ref[...] + _dot(y_ref[...], w_ref[...])
    ms = jnp.mean(h * h, axis=-1, keepdims=True)
    o_ref[...] = h * lax.rsqrt(ms + EPS) * fw_ref[...]


def _out_call(y, x, w, fw, *, tm):
    R = x.shape[0]
    row_spec = pl.BlockSpec((tm, D_MODEL), lambda i: (i, 0))
    return pl.pallas_call(
        _out_kernel,
        grid=(R // tm,),
        in_specs=[row_spec, row_spec, pl.BlockSpec((2 * WIDTH, D_MODEL), lambda i: (0, 0)),
                  pl.BlockSpec((1, D_MODEL), lambda i: (0, 0))],
        out_specs=row_spec,
        out_shape=jax.ShapeDtypeStruct((R, D_MODEL), F32),
        compiler_params=pltpu.CompilerParams(
            dimension_semantics=("parallel",), vmem_limit_bytes=VMEM_LIMIT),
        name="out",
    )(y, x, w, fw)


def kernel(x, meta_tokens, norm_w, w_in, conv_w, hg_lb_logits, hg_norm_w, gdn_A_log,
           gdn_dt_bias, gdn_norm_w, w_out, final_norm_w):
    B, T, D = x.shape
    assert D == D_MODEL and norm_w.shape[0] == 1 and meta_tokens.shape == (N_META, D_MODEL)
    assert w_in.shape == (1, D_MODEL, 8 * WIDTH + 2 * HEADS)

    w = jnp.pad(w_in[0], ((0, 0), (0, AB_PAD - 2 * HEADS))).astype(BF16)
    pad = lambda v: jnp.zeros((1, AB_PAD), F32).at[0, :HEADS].set(v.astype(F32))
    alog, dtb = pad(gdn_A_log[0]), pad(gdn_dt_bias[0])
    nw = norm_w[0].reshape(1, D_MODEL)
    cw = conv_w[0]
    proj = functools.partial(_proj_call, nw=nw, w=w, cw=cw, lbl=hg_lb_logits, alog=alog,
                             dtb=dtb)

    zero_tail = jnp.zeros((CONV_HIST, 3 * WIDTH), F32)
    m = proj(meta_tokens[None], tail=zero_tail, tm=N_META, chunk=N_META, emit_tail=True)
    _, mhk, mhv, mhg, _, _, mgk, mgv, mgcol, mgrow, tail = m
    sh0, sg0 = _meta_state_call(mhk, mhv, mhg, mgk, mgv, mgcol, mgrow)

    hq, hk, hv, hg, zg, gq, gk, gv, gcol, grow = proj(
        x, tail=tail, tm=512, chunk=CHUNK, emit_tail=False)
    y = _mixer_call(hq, hk, hv, hg, zg, gq, gk, gv, gcol, grow, sh0, sg0,
                    hg_norm_w[0].reshape(1, DH), gdn_norm_w[0].reshape(1, DH),
                    bb=4, tt=256, chunk=CHUNK)
    out = _out_call(y.reshape(B * T, 2 * WIDTH), x.reshape(B * T, D_MODEL),
                    w_out[0].astype(BF16), final_norm_w.reshape(1, D_MODEL), tm=1024)
    return out.reshape(B, T, D_MODEL)
```

```python
import functools

import jax
import jax.numpy as jnp
from jax import lax
from jax.experimental import pallas as pl
from jax.experimental.pallas import tpu as pltpu

F32 = jnp.float32
BF16 = jnp.bfloat16

D_MODEL = 1024
N_META = 16
HEADS = 4
DH = 128
WIDTH = HEADS * DH
CONV_W = 4
EPS = 1e-6
AB_PAD = 128
IN_PAD = 8 * WIDTH + AB_PAD

CHUNK = 64
SUB = 16
SAFE_RANGE = 60.0
CONV_HIST = 8
CUM_ROWS = 256
ROW_BLOCK = 128
RECUR_GAP = 3
VMEM_LIMIT = 56 * 1024 * 1024


def _dot(a, b):
    return jnp.dot(a, b, preferred_element_type=F32)


def _dot_nt(a, b):
    return lax.dot_general(a, b, (((1,), (1,)), ((), ())), preferred_element_type=F32)


def _dot_tn(a, b):
    return lax.dot_general(a, b, (((0,), (0,)), ((), ())), preferred_element_type=F32)


def _split3(x):
    x1 = x.astype(BF16)
    r = x - x1.astype(F32)
    x2 = r.astype(BF16)
    x3 = (r - x2.astype(F32)).astype(BF16)
    return x1, x2, x3


def _silu(x):
    return x * jax.nn.sigmoid(x)


def _softplus(x):
    return jnp.maximum(x, 0.0) + jnp.log1p(jnp.exp(-jnp.abs(x)))


def _chunk_cumsum(x, chunk):
    rows = x.shape[0]
    blk = min(rows, CUM_ROWS)
    r = lax.broadcasted_iota(jnp.int32, (blk, blk), 0)
    c = lax.broadcasted_iota(jnp.int32, (blk, blk), 1)
    tri = jnp.where((r // chunk == c // chunk) & (c <= r), 1.0, 0.0).astype(BF16)
    outs = []
    for i in range(rows // blk):
        parts = _split3(x[i * blk:(i + 1) * blk])
        outs.append(_dot(tri, parts[0]) + _dot(tri, parts[1]) + _dot(tri, parts[2]))
    return outs[0] if len(outs) == 1 else jnp.concatenate(outs, axis=0)


def _proj_kernel(x_ref, nw_ref, w_ref, cw_ref, lbl_ref, alog_ref, dtb_ref, tail_ref,
                 hq_ref, hk_ref, hv_ref, hg_ref, gq_ref, gk_ref, gv_ref, gz_ref,
                 gcol_ref, grow_ref, *rest, tm, chunk, emit_tail):
    if emit_tail:
        tail_out_ref, cbuf, logf_buf, ab_buf = rest
    else:
        cbuf, logf_buf, ab_buf = rest
    j = pl.program_id(1)

    @pl.when(j == 0)
    def _():
        cbuf[0:CONV_HIST, :] = tail_ref[...]

    lbl = lbl_ref[...]
    e = jnp.exp(lbl - jnp.max(lbl, axis=0, keepdims=True))
    lb = e[0:1] / jnp.sum(e, axis=0, keepdims=True)

    rb = min(tm, ROW_BLOCK)
    for r0 in range(0, tm, rb):
        rows = slice(r0, r0 + rb)
        x = x_ref[0, rows, :]
        ms = jnp.mean(x * x, axis=-1, keepdims=True)
        u = (x * lax.rsqrt(ms + EPS) * nw_ref[...]).astype(BF16)

        def proj(a, b, u=u):
            return _dot(u, w_ref[:, a:b])

        cbuf[CONV_HIST + r0:CONV_HIST + r0 + rb, :] = proj(4 * WIDTH, 7 * WIDTH)
        xe = cbuf[r0:r0 + CONV_HIST + rb, :]
        acc = cw_ref[0:1, :] * xe
        for t in range(1, CONV_W):
            acc = pltpu.roll(acc, 1, 0) + cw_ref[t:t + 1, :] * xe
        qkv = _silu(acc[CONV_HIST:CONV_HIST + rb])
        for h in range(HEADS):
            q = qkv[:, h * DH:(h + 1) * DH]
            k = qkv[:, WIDTH + h * DH:WIDTH + (h + 1) * DH]
            q = q * lax.rsqrt(jnp.sum(q * q, axis=-1, keepdims=True) + EPS) * (DH ** -0.5)
            k = k * lax.rsqrt(jnp.sum(k * k, axis=-1, keepdims=True) + EPS)
            gq_ref[0, rows, h * DH:(h + 1) * DH] = q.astype(BF16)
            gk_ref[0, rows, h * DH:(h + 1) * DH] = k.astype(BF16)
        gv_ref[0, rows, :] = qkv[:, 2 * WIDTH:3 * WIDTH].astype(BF16)
        gz_ref[0, rows, :] = _silu(proj(7 * WIDTH, 8 * WIDTH)).astype(BF16)

        hq_ref[0, rows, :] = _silu(proj(0, WIDTH)).astype(BF16)
        f = lb + (1.0 - lb) * jax.nn.sigmoid(proj(WIDTH, 2 * WIDTH))
        hk_ref[0, rows, :] = (1.0 - f).astype(BF16)
        logf_buf[rows, :] = jnp.log(f)
        hv_ref[0, rows, :] = proj(2 * WIDTH, 3 * WIDTH).astype(BF16)

        ab_buf[rows, :] = proj(8 * WIDTH, 8 * WIDTH + AB_PAD)

    for r0 in range(0, tm, rb):
        rows = slice(r0, r0 + rb)
        hg_ref[0, rows, :] = _chunk_cumsum(logf_buf[rows, :], chunk)
        p = ab_buf[rows, :]
        lane = lax.broadcasted_iota(jnp.int32, p.shape, 1)
        g = -jnp.exp(alog_ref[...]) * _softplus(p + dtb_ref[...])
        gam = _chunk_cumsum(jnp.where(lane < HEADS, g, 0.0), chunk)
        comb = jnp.where(lane < HEADS, gam,
                         jnp.where(lane < 2 * HEADS, jax.nn.sigmoid(p), 0.0))
        gcol_ref[0, rows, :] = comb
        for c in range(rb // chunk):
            blk = comb[c * chunk:(c + 1) * chunk, :]
            grow_ref[0, r0 // chunk + c] = blk.T[0:2 * HEADS, :]

    hist = cbuf[tm:tm + CONV_HIST, :]
    cbuf[0:CONV_HIST, :] = hist
    if emit_tail:
        tail_out_ref[...] = hist


def _proj_call(x, nw, w, cw, lbl, alog, dtb, tail, *, tm, chunk, emit_tail):
    B, T, _ = x.shape
    nt = T // tm
    tok = lambda width, dt: jax.ShapeDtypeStruct((B, T, width), dt)
    tok_spec = lambda width: pl.BlockSpec((1, tm, width), lambda b, j: (b, j, 0))
    const = lambda shape: pl.BlockSpec(shape, lambda b, j: (0,) * len(shape))
    out_shape = [tok(WIDTH, BF16), tok(WIDTH, BF16), tok(WIDTH, BF16), tok(WIDTH, F32),
                 tok(WIDTH, BF16), tok(WIDTH, BF16), tok(WIDTH, BF16), tok(WIDTH, BF16),
                 tok(AB_PAD, F32),
                 jax.ShapeDtypeStruct((B, T // chunk, 2 * HEADS, chunk), F32)]
    out_specs = [tok_spec(WIDTH), tok_spec(WIDTH), tok_spec(WIDTH), tok_spec(WIDTH),
                 tok_spec(WIDTH), tok_spec(WIDTH), tok_spec(WIDTH), tok_spec(WIDTH),
                 tok_spec(AB_PAD),
                 pl.BlockSpec((1, tm // chunk, 2 * HEADS, chunk), lambda b, j: (b, j, 0, 0))]
    if emit_tail:
        out_shape.append(jax.ShapeDtypeStruct((CONV_HIST, 3 * WIDTH), F32))
        out_specs.append(const((CONV_HIST, 3 * WIDTH)))
    return pl.pallas_call(
        functools.partial(_proj_kernel, tm=tm, chunk=chunk, emit_tail=emit_tail),
        grid=(B, nt),
        in_specs=[tok_spec(D_MODEL), const((1, D_MODEL)), const((D_MODEL, IN_PAD)),
                  const((CONV_W, 3 * WIDTH)), const(lbl.shape), const((1, AB_PAD)),
                  const((1, AB_PAD)), const((CONV_HIST, 3 * WIDTH))],
        out_specs=out_specs,
        out_shape=out_shape,
        scratch_shapes=[pltpu.VMEM((tm + CONV_HIST, 3 * WIDTH), F32),
                        pltpu.VMEM((tm, WIDTH), F32), pltpu.VMEM((tm, AB_PAD), F32)],
        compiler_params=pltpu.CompilerParams(
            dimension_semantics=("parallel", "arbitrary"), vmem_limit_bytes=VMEM_LIMIT),
        name="proj_meta" if emit_tail else "proj",
    )(x, nw, w, cw, lbl, alog, dtb, tail)


def _run_interleaved(*gens):
    live = list(gens)
    while live:
        for g in list(live):
            try:
                next(g)
            except StopIteration:
                live.remove(g)


def _scaled(x, decay):
    return x * decay.astype(BF16)


def _spread(gen, gap):
    for _ in gen:
        yield
        for _ in range(gap):
            yield


def _hgrn2_intra_stages(q, k, v, G, out, *, bounded):
    n = len(q)
    C = q[0].shape[0]
    row = lax.broadcasted_iota(jnp.int32, (C, C), 0)
    col = lax.broadcasted_iota(jnp.int32, (C, C), 1)
    if bounded:
        g_mid = [G[i][C // 2 - 1:C // 2, :] for i in range(n)]
        p = [_dot_nt(_scaled(q[i], jnp.exp(G[i] - g_mid[i])),
                     _scaled(k[i], jnp.exp(g_mid[i] - G[i]))) for i in range(n)]
        yield
        out["o"] = [_dot(jnp.where(row >= col, p[i], 0.0).astype(BF16), v[i]) for i in range(n)]
        return
    nb = C // SUB
    qf = [q[i].astype(F32) for i in range(n)]
    kf = [k[i].astype(F32) for i in range(n)]
    a = [jnp.zeros((C, C), F32) for _ in range(n)]
    g_end = [[G[i][(j + 1) * SUB - 1:(j + 1) * SUB, :] for j in range(nb)] for i in range(n)]
    if nb > 1:
        p = []
        for i in range(n):
            g_ref = jnp.concatenate(
                [jnp.broadcast_to(g_end[i][j], (SUB, DH)) for j in range(nb)], axis=0)
            kp = (kf[i] * jnp.exp(g_ref - G[i])).astype(BF16)
            qp = jnp.concatenate(
                [qf[i] * jnp.exp(jnp.minimum(G[i] - g_end[i][j], 0.0)) for j in range(nb - 1)],
                axis=0).astype(BF16)
            p.append(_dot_nt(qp, kp))
        yield
        for i in range(n):
            for j in range(nb - 1):
                m = (col // SUB == j) & (row // SUB > j)
                a[i] = jnp.where(m, p[i][j * C:(j + 1) * C], a[i])
    rowb = lax.broadcasted_iota(jnp.int32, (SUB, C), 0)
    colb = lax.broadcasted_iota(jnp.int32, (SUB, C), 1)
    o = []
    for i in range(n):
        for b in range(nb):
            gb = G[i][b * SUB:(b + 1) * SUB]
            qb = qf[i][b * SUB:(b + 1) * SUB]
            kb = kf[i][b * SUB:(b + 1) * SUB]
            d = jnp.zeros((SUB, C), F32)
            for s in range(SUB):
                es = jnp.exp(jnp.minimum(gb - gb[s:s + 1], 0.0))
                cs = jnp.sum(qb * es * kb[s:s + 1], axis=-1, keepdims=True)
                d = jnp.where((colb == b * SUB + s) & (rowb >= s), cs, d)
            if nb == 1:
                a[i] = d
            else:
                dfull = jnp.concatenate(
                    [d if r == b else jnp.zeros((SUB, C), F32) for r in range(nb)], axis=0)
                a[i] = jnp.where((row // SUB == b) & (col // SUB == b), dfull, a[i])
        o.append(_dot(a[i].astype(BF16), v[i]))
    out["o"] = o


def _hgrn2_state_stages(q, k, v, G, st, o_intra, out):
    n = len(k)
    C = k[0].shape[0]
    g_last = [G[i][C - 1:C, :] for i in range(n)]
    kk = [_scaled(k[i], jnp.exp(g_last[i] - G[i])) for i in range(n)]
    st_new = [_dot_tn(v[i], kk[i]) for i in range(n)]
    if st is not None:
        st_new = [st_new[i] + st[i] * jnp.exp(g_last[i]) for i in range(n)]
    out["s"] = st_new
    yield
    if o_intra is not None:
        out["o"] = [o_intra[i] + _dot_nt(_scaled(q[i], jnp.exp(G[i])), st[i].astype(BF16))
                    for i in range(n)]


def _gdn_prep_stages(q, k, gam_c, gam_r, beta_c, out, *, need_o):
    n = len(k)
    C = k[0].shape[0]
    rng = range(n)
    row = lax.broadcasted_iota(jnp.int32, (C, C), 0)
    col = lax.broadcasted_iota(jnp.int32, (C, C), 1)
    if need_o:
        pk = [_dot_nt(jnp.concatenate([q[i], k[i]], axis=0), k[i]) for i in rng]
        qk, kk = [x[:C] for x in pk], [x[C:] for x in pk]
    else:
        kk = [_dot_nt(k[i], k[i]) for i in rng]
    yield
    dec = [jnp.exp(jnp.minimum(gam_c[i] - gam_r[i], 0.0)) for i in rng]
    if need_o:
        out["qkd"] = [jnp.where(row >= col, qk[i] * dec[i], 0.0).astype(BF16) for i in rng]
    a = [jnp.where(row > col, beta_c[i] * kk[i] * dec[i], 0.0) for i in rng]
    eye = jnp.where(row == col, 1.0, 0.0)
    t = [eye - jnp.where(row // 2 == col // 2, a[i], 0.0) for i in rng]
    bs = 2
    while bs < C:
        lm = (row // (2 * bs) == col // (2 * bs)) & (row // bs > col // bs)
        tb = [t[i].astype(BF16) for i in rng]
        tl = [_dot(tb[i], jnp.where(lm, a[i], 0.0).astype(BF16)) for i in rng]
        yield
        t = [t[i] - _dot(tl[i].astype(BF16), tb[i]) for i in rng]
        yield
        bs *= 2
    out["t"] = [t[i].astype(BF16) for i in rng]


def _gdn_state_stages(q, k, v, gam_c, beta_c, t, qkd, s, out):
    n = len(k)
    C = k[0].shape[0]
    rng = range(n)
    vf = [v[i].astype(F32) for i in rng]
    qs = None
    if s is not None:
        egam = [jnp.exp(gam_c[i]) for i in rng]
        sb = [s[i].astype(BF16) for i in rng]
        if qkd is not None:
            qks = [_dot(jnp.concatenate([_scaled(q[i], egam[i]), _scaled(k[i], egam[i])],
                                        axis=0), sb[i]) for i in rng]
            qs, ks = [x[:C] for x in qks], [x[C:] for x in qks]
        else:
            ks = [_dot(_scaled(k[i], egam[i]), sb[i]) for i in rng]
        yield
        rhs = [beta_c[i] * (vf[i] - ks[i]) for i in rng]
    else:
        rhs = [beta_c[i] * vf[i] for i in rng]
    ub = [_dot(t[i], rhs[i].astype(BF16)).astype(BF16) for i in rng]
    yield
    g_last = [gam_c[i][C - 1:C, :] for i in rng]
    s_new = [_dot_tn(_scaled(k[i], jnp.exp(g_last[i] - gam_c[i])), ub[i]) for i in rng]
    if s is not None:
        s_new = [s_new[i] + s[i] * jnp.exp(g_last[i]) for i in rng]
    out["s"] = s_new
    if qkd is not None:
        o = [_dot(qkd[i], ub[i]) for i in rng]
        out["o"] = o if qs is None else [o[i] + qs[i] for i in rng]


def _head_norm(o, w):
    return o * lax.rsqrt(jnp.mean(o * o, axis=-1, keepdims=True) + EPS) * w


def _mixer_kernel(hq_ref, hk_ref, hv_ref, hg_ref, gq_ref, gk_ref, gv_ref, gz_ref,
                  gcol_ref, grow_ref, sh0_ref, sg0_ref, hw_ref, gw_ref,
                  y_ref, sh_ref, sg_ref, t_buf, qkd_buf, oi_buf, bounded_ref, *, bb, tt, chunk):
    j = pl.program_id(1)
    nchunks = tt // chunk

    @pl.when(j == 0)
    def _():
        for b in range(bb):
            sh_ref[b] = sh0_ref[...]
            sg_ref[b] = sg0_ref[...]

    for c in range(nchunks):
        span = None
        for b in range(bb):
            g_mid = hg_ref[b, c * chunk + chunk // 2 - 1:c * chunk + chunk // 2, :]
            g_last = hg_ref[b, (c + 1) * chunk - 1:(c + 1) * chunk, :]
            s = jnp.maximum(-g_mid, g_mid - g_last)
            span = s if span is None else jnp.maximum(span, s)
        bounded_ref[c] = jnp.where(jnp.max(span) <= SAFE_RANGE, 1, 0)

    hw = hw_ref[...]
    gw = gw_ref[...]
    probs = [(b, h) for b in range(bb) for h in range(HEADS)]
    hs = lambda h: slice(h * DH, (h + 1) * DH)

    def chunk_rows(c):
        return pl.ds(pl.multiple_of(c * chunk, chunk), chunk)

    def prep_stages(c, pout, hout, bounded):
        rows = chunk_rows(c)
        ld = lambda ref: [ref[b, rows, hs(h)] for b, h in probs]
        gcol = [gcol_ref[b, rows, :] for b in range(bb)]
        grow = [grow_ref[b, c] for b in range(bb)]
        return (_gdn_prep_stages(ld(gq_ref), ld(gk_ref),
                                 [gcol[b][:, h:h + 1] for b, h in probs],
                                 [grow[b][h:h + 1, :] for b, h in probs],
                                 [gcol[b][:, HEADS + h:HEADS + h + 1] for b, h in probs],
                                 pout, need_o=True),
                _hgrn2_intra_stages(ld(hq_ref), ld(hk_ref), ld(hv_ref), ld(hg_ref), hout,
                                    bounded=bounded))

    def store_prep(c, pout, hout):
        for i in range(len(probs)):
            t_buf[c, i] = pout["t"][i]
            qkd_buf[c, i] = pout["qkd"][i]
            oi_buf[c, i] = hout["o"][i]

    def recur_stages(c, gout, hout):
        rows = chunk_rows(c)
        ld = lambda ref: [ref[b, rows, hs(h)] for b, h in probs]
        gcol = [gcol_ref[b, rows, :] for b in range(bb)]
        n = len(probs)
        return (_gdn_state_stages(ld(gq_ref), ld(gk_ref), ld(gv_ref),
                                  [gcol[b][:, h:h + 1] for b, h in probs],
                                  [gcol[b][:, HEADS + h:HEADS + h + 1] for b, h in probs],
                                  [t_buf[c, i] for i in range(n)],
                                  [qkd_buf[c, i] for i in range(n)],
                                  [sg_ref[b, h] for b, h in probs], gout),
                _hgrn2_state_stages(ld(hq_ref), ld(hk_ref), ld(hv_ref), ld(hg_ref),
                                    [sh_ref[b, h] for b, h in probs],
                                    [oi_buf[c, i] for i in range(n)], hout))

    def store_recur(c, gout, hout):
        rows = chunk_rows(c)
        for i, (b, h) in enumerate(probs):
            sh_ref[b, h] = hout["s"][i]
            sg_ref[b, h] = gout["s"][i]
            y_ref[b, rows, hs(h)] = _head_norm(hout["o"][i], hw).astype(BF16)
            ys = slice(WIDTH + h * DH, WIDTH + (h + 1) * DH)
            y_ref[b, rows, ys] = (_head_norm(gout["o"][i], gw)
                                  * gz_ref[b, rows, hs(h)].astype(F32)).astype(BF16)

    def prep_only(c, bounded):
        pout, hout = {}, {}
        _run_interleaved(*prep_stages(c, pout, hout, bounded))
        store_prep(c, pout, hout)

    def recur_only(c):
        gout, hout = {}, {}
        _run_interleaved(*recur_stages(c, gout, hout))
        store_recur(c, gout, hout)

    def pipelined_step():
        prep_only(0, True)

        def body(c, carry):
            pout, phout, gout, hout = {}, {}, {}, {}
            gprep, hprep = prep_stages(c + 1, pout, phout, True)
            gstate, hstate = recur_stages(c, gout, hout)
            _run_interleaved(gprep, _spread(gstate, RECUR_GAP), _spread(hstate, RECUR_GAP),
                             hprep)
            store_prep(c + 1, pout, phout)
            store_recur(c, gout, hout)
            return carry

        lax.fori_loop(0, nchunks - 1, body, 0)
        recur_only(nchunks - 1)

    def plain_step():
        def body(c, carry):
            prep_only(c, False)
            recur_only(c)
            return carry

        lax.fori_loop(0, nchunks, body, 0)

    all_bounded = bounded_ref[0] == 1
    for c in range(1, nchunks):
        all_bounded = jnp.logical_and(all_bounded, bounded_ref[c] == 1)
    pl.when(all_bounded)(pipelined_step)
    pl.when(jnp.logical_not(all_bounded))(plain_step)


def _mixer_call(hq, hk, hv, hg, gq, gk, gv, gz, gcol, grow, sh0, sg0, hw, gw, *, bb, tt, chunk):
    B, T, _ = hq.shape
    nchunks = tt // chunk
    tok_spec = lambda width: pl.BlockSpec((bb, tt, width), lambda i, j: (i, j, 0))
    const = lambda shape: pl.BlockSpec(shape, lambda i, j: (0,) * len(shape))
    return pl.pallas_call(
        functools.partial(_mixer_kernel, bb=bb, tt=tt, chunk=chunk),
        grid=(B // bb, T // tt),
        in_specs=[tok_spec(WIDTH), tok_spec(WIDTH), tok_spec(WIDTH), tok_spec(WIDTH),
                  tok_spec(WIDTH), tok_spec(WIDTH), tok_spec(WIDTH), tok_spec(WIDTH),
                  tok_spec(AB_PAD),
                  pl.BlockSpec((bb, nchunks, 2 * HEADS, chunk), lambda i, j: (i, j, 0, 0)),
                  const((HEADS, DH, DH)), const((HEADS, DH, DH)),
                  const((1, DH)), const((1, DH))],
        out_specs=tok_spec(2 * WIDTH),
        out_shape=jax.ShapeDtypeStruct((B, T, 2 * WIDTH), BF16),
        scratch_shapes=[pltpu.VMEM((bb, HEADS, DH, DH), F32),
                        pltpu.VMEM((bb, HEADS, DH, DH), F32),
                        pltpu.VMEM((nchunks, bb * HEADS, chunk, chunk), BF16),
                        pltpu.VMEM((nchunks, bb * HEADS, chunk, chunk), BF16),
                        pltpu.VMEM((nchunks, bb * HEADS, chunk, DH), F32),
                        pltpu.SMEM((nchunks,), jnp.int32)],
        compiler_params=pltpu.CompilerParams(
            dimension_semantics=("parallel", "arbitrary"), vmem_limit_bytes=VMEM_LIMIT),
        name="mixer",
    )(hq, hk, hv, hg, gq, gk, gv, gz, gcol, grow, sh0, sg0, hw, gw)


def _meta_state_kernel(hk_ref, hv_ref, hg_ref, gk_ref, gv_ref, gcol_ref, grow_ref,
                       sh_ref, sg_ref):
    gcol = gcol_ref[0]
    grow = grow_ref[0, 0]
    hs = lambda h: slice(h * DH, (h + 1) * DH)
    heads = range(HEADS)
    hk = [hk_ref[0, :, hs(h)] for h in heads]
    gk = [gk_ref[0, :, hs(h)] for h in heads]
    gam_c = [gcol[:, h:h + 1] for h in heads]
    beta_c = [gcol[:, HEADS + h:HEADS + h + 1] for h in heads]
    prep, hout, gout = {}, {}, {}
    _run_interleaved(
        _gdn_prep_stages(None, gk, gam_c, [grow[h:h + 1, :] for h in heads], beta_c, prep,
                         need_o=False),
        _hgrn2_state_stages(None, hk, [hv_ref[0, :, hs(h)] for h in heads],
                            [hg_ref[0, :, hs(h)] for h in heads], None, None, hout))
    _run_interleaved(
        _gdn_state_stages(None, gk, [gv_ref[0, :, hs(h)] for h in heads], gam_c, beta_c,
                          prep["t"], None, None, gout))
    for h in heads:
        sh_ref[h] = hout["s"][h]
        sg_ref[h] = gout["s"][h]


def _meta_state_call(hk, hv, hg, gk, gv, gcol, grow):
    state = jax.ShapeDtypeStruct((HEADS, DH, DH), F32)
    return pl.pallas_call(
        _meta_state_kernel,
        out_shape=[state, state],
        name="meta_state",
    )(hk, hv, hg, gk, gv, gcol, grow)


def _out_kernel(y_ref, x_ref, nw_ref, wz_ref, w_ref, fw_ref, o_ref):
    x = x_ref[...]
    ms = jnp.mean(x * x, axis=-1, keepdims=True)
    u = (x * lax.rsqrt(ms + EPS) * nw_ref[...]).astype(BF16)
    gate = _silu(_dot(u, wz_ref[...]))
    yh = (y_ref[:, 0:WIDTH].astype(F32) * gate).astype(BF16)
    h = x + _dot(yh, w_ref[0:WIDTH, :]) + _dot(y_ref[:, WIDTH:2 * WIDTH], w_ref[WIDTH:2 * WIDTH, :])
    ms = jnp.mean(h * h, axis=-1, keepdims=True)
    o_ref[...] = h * lax.rsqrt(ms + EPS) * fw_ref[...]


def _out_call(y, x, nw, wz, w, fw, *, tm):
    R = x.shape[0]
    row_spec = pl.BlockSpec((tm, D_MODEL), lambda i: (i, 0))
    const = lambda shape: pl.BlockSpec(shape, lambda i: (0, 0))
    return pl.pallas_call(
        _out_kernel,
        grid=(R // tm,),
        in_specs=[row_spec, row_spec, const((1, D_MODEL)), const((D_MODEL, WIDTH)),
                  const((2 * WIDTH, D_MODEL)), const((1, D_MODEL))],
        out_specs=row_spec,
        out_shape=jax.ShapeDtypeStruct((R, D_MODEL), F32),
        compiler_params=pltpu.CompilerParams(
            dimension_semantics=("parallel",), vmem_limit_bytes=VMEM_LIMIT),
        name="out",
    )(y, x, nw, wz, w, fw)


def kernel(x, meta_tokens, norm_w, w_in, conv_w, hg_lb_logits, hg_norm_w, gdn_A_log,
           gdn_dt_bias, gdn_norm_w, w_out, final_norm_w):
    B, T, D = x.shape
    assert D == D_MODEL and norm_w.shape[0] == 1 and meta_tokens.shape == (N_META, D_MODEL)
    assert w_in.shape == (1, D_MODEL, 8 * WIDTH + 2 * HEADS)

    w = jnp.pad(w_in[0], ((0, 0), (0, AB_PAD - 2 * HEADS))).astype(BF16)
    pad = lambda v: jnp.zeros((1, AB_PAD), F32).at[0, :HEADS].set(v.astype(F32))
    alog, dtb = pad(gdn_A_log[0]), pad(gdn_dt_bias[0])
    nw = norm_w[0].reshape(1, D_MODEL)
    cw = conv_w[0]
    proj = functools.partial(_proj_call, nw=nw, w=w, cw=cw, lbl=hg_lb_logits, alog=alog,
                             dtb=dtb)

    zero_tail = jnp.zeros((CONV_HIST, 3 * WIDTH), F32)
    m = proj(meta_tokens[None], tail=zero_tail, tm=N_META, chunk=N_META, emit_tail=True)
    _, mhk, mhv, mhg, _, mgk, mgv, _, mgcol, mgrow, tail = m
    sh0, sg0 = _meta_state_call(mhk, mhv, mhg, mgk, mgv, mgcol, mgrow)

    hq, hk, hv, hg, gq, gk, gv, gz, gcol, grow = proj(
        x, tail=tail, tm=512, chunk=CHUNK, emit_tail=False)
    y = _mixer_call(hq, hk, hv, hg, gq, gk, gv, gz, gcol, grow, sh0, sg0,
                    hg_norm_w[0].reshape(1, DH), gdn_norm_w[0].reshape(1, DH),
                    bb=4, tt=256, chunk=CHUNK)
    wz = w_in[0, :, 3 * WIDTH:4 * WIDTH].astype(BF16)
    out = _out_call(y.reshape(B * T, 2 * WIDTH), x.reshape(B * T, D_MODEL), nw, wz,
                    w_out[0].astype(BF16), final_norm_w.reshape(1, D_MODEL), tm=1024)
    return out.reshape(B, T, D_MODEL)
```

```python
import functools

import jax
import jax.numpy as jnp
from jax import lax
from jax.experimental import pallas as pl
from jax.experimental.pallas import tpu as pltpu

F32 = jnp.float32
BF16 = jnp.bfloat16

D_MODEL = 1024
N_META = 16
HEADS = 4
DH = 128
WIDTH = HEADS * DH
CONV_W = 4
EPS = 1e-6
AB_PAD = 128
IN_PAD = 8 * WIDTH + AB_PAD

CHUNK = 64
SUB = 16
SAFE_RANGE = 60.0
CONV_HIST = 8
CUM_ROWS = 256
ROW_BLOCK = 128
PROJ_ROWS = 1024
MIXER_BATCH = 4
MIXER_ROWS = 256
OUT_ROWS = 1024
RECUR_GAP = 3
VMEM_LIMIT = 56 * 1024 * 1024


def _dot(a, b):
    return jnp.dot(a, b, preferred_element_type=F32)


def _dot_nt(a, b):
    return lax.dot_general(a, b, (((1,), (1,)), ((), ())), preferred_element_type=F32)


def _dot_tn(a, b):
    return lax.dot_general(a, b, (((0,), (0,)), ((), ())), preferred_element_type=F32)


def _split3(x):
    x1 = x.astype(BF16)
    r = x - x1.astype(F32)
    x2 = r.astype(BF16)
    x3 = (r - x2.astype(F32)).astype(BF16)
    return x1, x2, x3


def _silu(x):
    return x * jax.nn.sigmoid(x)


def _softplus(x):
    return jnp.maximum(x, 0.0) + jnp.log1p(jnp.exp(-jnp.abs(x)))


def _chunk_cumsum(x, chunk):
    rows = x.shape[0]
    blk = min(rows, CUM_ROWS)
    r = lax.broadcasted_iota(jnp.int32, (blk, blk), 0)
    c = lax.broadcasted_iota(jnp.int32, (blk, blk), 1)
    tri = jnp.where((r // chunk == c // chunk) & (c <= r), 1.0, 0.0).astype(BF16)
    outs = []
    for i in range(rows // blk):
        parts = _split3(x[i * blk:(i + 1) * blk])
        outs.append(_dot(tri, parts[0]) + _dot(tri, parts[1]) + _dot(tri, parts[2]))
    return outs[0] if len(outs) == 1 else jnp.concatenate(outs, axis=0)


def _proj_kernel(x_ref, nw_ref, w_ref, cw_ref, lbl_ref, alog_ref, dtb_ref, tail_ref,
                 hq_ref, hk_ref, hv_ref, hg_ref, gq_ref, gk_ref, gv_ref, gz_ref,
                 gcol_ref, grow_ref, *rest, tm, chunk, emit_tail):
    if emit_tail:
        tail_out_ref, cbuf, logf_buf, ab_buf = rest
    else:
        cbuf, logf_buf, ab_buf = rest
    j = pl.program_id(1)

    @pl.when(j == 0)
    def _():
        cbuf[0:CONV_HIST, :] = tail_ref[...]

    lbl = lbl_ref[...]
    e = jnp.exp(lbl - jnp.max(lbl, axis=0, keepdims=True))
    lb = e[0:1] / jnp.sum(e, axis=0, keepdims=True)

    rb = min(tm, ROW_BLOCK)
    for r0 in range(0, tm, rb):
        rows = slice(r0, r0 + rb)
        x = x_ref[0, rows, :]
        ms = jnp.mean(x * x, axis=-1, keepdims=True)
        u = (x * lax.rsqrt(ms + EPS) * nw_ref[...]).astype(BF16)

        def proj(a, b, u=u):
            return _dot(u, w_ref[:, a:b])

        cbuf[CONV_HIST + r0:CONV_HIST + r0 + rb, :] = proj(4 * WIDTH, 7 * WIDTH)
        xe = cbuf[r0:r0 + CONV_HIST + rb, :]
        acc = cw_ref[0:1, :] * xe
        for t in range(1, CONV_W):
            acc = pltpu.roll(acc, 1, 0) + cw_ref[t:t + 1, :] * xe
        qkv = _silu(acc[CONV_HIST:CONV_HIST + rb])
        for h in range(HEADS):
            q = qkv[:, h * DH:(h + 1) * DH]
            k = qkv[:, WIDTH + h * DH:WIDTH + (h + 1) * DH]
            q = q * lax.rsqrt(jnp.sum(q * q, axis=-1, keepdims=True) + EPS) * (DH ** -0.5)
            k = k * lax.rsqrt(jnp.sum(k * k, axis=-1, keepdims=True) + EPS)
            gq_ref[0, rows, h * DH:(h + 1) * DH] = q.astype(BF16)
            gk_ref[0, rows, h * DH:(h + 1) * DH] = k.astype(BF16)
        gv_ref[0, rows, :] = qkv[:, 2 * WIDTH:3 * WIDTH].astype(BF16)
        gz_ref[0, rows, :] = _silu(proj(7 * WIDTH, 8 * WIDTH)).astype(BF16)

        hq_ref[0, rows, :] = _silu(proj(0, WIDTH)).astype(BF16)
        f = lb + (1.0 - lb) * jax.nn.sigmoid(proj(WIDTH, 2 * WIDTH))
        hk_ref[0, rows, :] = (1.0 - f).astype(BF16)
        logf_buf[rows, :] = jnp.log(f)
        hv_ref[0, rows, :] = proj(2 * WIDTH, 3 * WIDTH).astype(BF16)

        ab_buf[rows, :] = proj(8 * WIDTH, 8 * WIDTH + AB_PAD)

    for r0 in range(0, tm, rb):
        rows = slice(r0, r0 + rb)
        hg_ref[0, rows, :] = _chunk_cumsum(logf_buf[rows, :], chunk)
        p = ab_buf[rows, :]
        lane = lax.broadcasted_iota(jnp.int32, p.shape, 1)
        g = -jnp.exp(alog_ref[...]) * _softplus(p + dtb_ref[...])
        gam = _chunk_cumsum(jnp.where(lane < HEADS, g, 0.0), chunk)
        comb = jnp.where(lane < HEADS, gam,
                         jnp.where(lane < 2 * HEADS, jax.nn.sigmoid(p), 0.0))
        gcol_ref[0, rows, :] = comb
        for c in range(rb // chunk):
            blk = comb[c * chunk:(c + 1) * chunk, :]
            grow_ref[0, r0 // chunk + c] = blk.T[0:2 * HEADS, :]

    hist = cbuf[tm:tm + CONV_HIST, :]
    cbuf[0:CONV_HIST, :] = hist
    if emit_tail:
        tail_out_ref[...] = hist


def _proj_call(x, nw, w, cw, lbl, alog, dtb, tail, *, tm, chunk, emit_tail):
    B, T, _ = x.shape
    nt = T // tm
    tok = lambda width, dt: jax.ShapeDtypeStruct((B, T, width), dt)
    tok_spec = lambda width: pl.BlockSpec((1, tm, width), lambda b, j: (b, j, 0))
    const = lambda shape: pl.BlockSpec(shape, lambda b, j: (0,) * len(shape))
    out_shape = [tok(WIDTH, BF16), tok(WIDTH, BF16), tok(WIDTH, BF16), tok(WIDTH, F32),
                 tok(WIDTH, BF16), tok(WIDTH, BF16), tok(WIDTH, BF16), tok(WIDTH, BF16),
                 tok(AB_PAD, F32),
                 jax.ShapeDtypeStruct((B, T // chunk, 2 * HEADS, chunk), F32)]
    out_specs = [tok_spec(WIDTH), tok_spec(WIDTH), tok_spec(WIDTH), tok_spec(WIDTH),
                 tok_spec(WIDTH), tok_spec(WIDTH), tok_spec(WIDTH), tok_spec(WIDTH),
                 tok_spec(AB_PAD),
                 pl.BlockSpec((1, tm // chunk, 2 * HEADS, chunk), lambda b, j: (b, j, 0, 0))]
    if emit_tail:
        out_shape.append(jax.ShapeDtypeStruct((CONV_HIST, 3 * WIDTH), F32))
        out_specs.append(const((CONV_HIST, 3 * WIDTH)))
    return pl.pallas_call(
        functools.partial(_proj_kernel, tm=tm, chunk=chunk, emit_tail=emit_tail),
        grid=(B, nt),
        in_specs=[tok_spec(D_MODEL), const((1, D_MODEL)),
                  pl.BlockSpec((D_MODEL, IN_PAD), lambda b, j: (0, 0),
                               pipeline_mode=pl.Buffered(1)),
                  const((CONV_W, 3 * WIDTH)), const(lbl.shape), const((1, AB_PAD)),
                  const((1, AB_PAD)), const((CONV_HIST, 3 * WIDTH))],
        out_specs=out_specs,
        out_shape=out_shape,
        scratch_shapes=[pltpu.VMEM((tm + CONV_HIST, 3 * WIDTH), F32),
                        pltpu.VMEM((tm, WIDTH), F32), pltpu.VMEM((tm, AB_PAD), F32)],
        compiler_params=pltpu.CompilerParams(
            dimension_semantics=("parallel", "arbitrary"), vmem_limit_bytes=VMEM_LIMIT),
        name="proj_meta" if emit_tail else "proj",
    )(x, nw, w, cw, lbl, alog, dtb, tail)


def _run_interleaved(*gens):
    live = list(gens)
    while live:
        for g in list(live):
            try:
                next(g)
            except StopIteration:
                live.remove(g)


def _scaled(x, decay):
    return x * decay.astype(BF16)


def _spread(gen, gap):
    for _ in gen:
        yield
        for _ in range(gap):
            yield


def _hgrn2_intra_stages(q, k, v, G, out, *, bounded):
    n = len(q)
    C = q[0].shape[0]
    row = lax.broadcasted_iota(jnp.int32, (C, C), 0)
    col = lax.broadcasted_iota(jnp.int32, (C, C), 1)
    if bounded:
        g_mid = [G[i][C // 2 - 1:C // 2, :] for i in range(n)]
        p = [_dot_nt(_scaled(q[i], jnp.exp(G[i] - g_mid[i])),
                     _scaled(k[i], jnp.exp(g_mid[i] - G[i]))) for i in range(n)]
        yield
        out["o"] = [_dot(jnp.where(row >= col, p[i], 0.0).astype(BF16), v[i]) for i in range(n)]
        return
    nb = C // SUB
    qf = [q[i].astype(F32) for i in range(n)]
    kf = [k[i].astype(F32) for i in range(n)]
    a = [jnp.zeros((C, C), F32) for _ in range(n)]
    g_end = [[G[i][(j + 1) * SUB - 1:(j + 1) * SUB, :] for j in range(nb)] for i in range(n)]
    if nb > 1:
        p = []
        for i in range(n):
            g_ref = jnp.concatenate(
                [jnp.broadcast_to(g_end[i][j], (SUB, DH)) for j in range(nb)], axis=0)
            kp = (kf[i] * jnp.exp(g_ref - G[i])).astype(BF16)
            qp = jnp.concatenate(
                [qf[i] * jnp.exp(jnp.minimum(G[i] - g_end[i][j], 0.0)) for j in range(nb - 1)],
                axis=0).astype(BF16)
            p.append(_dot_nt(qp, kp))
        yield
        for i in range(n):
            for j in range(nb - 1):
                m = (col // SUB == j) & (row // SUB > j)
                a[i] = jnp.where(m, p[i][j * C:(j + 1) * C], a[i])
    rowb = lax.broadcasted_iota(jnp.int32, (SUB, C), 0)
    colb = lax.broadcasted_iota(jnp.int32, (SUB, C), 1)
    o = []
    for i in range(n):
        for b in range(nb):
            gb = G[i][b * SUB:(b + 1) * SUB]
            qb = qf[i][b * SUB:(b + 1) * SUB]
            kb = kf[i][b * SUB:(b + 1) * SUB]
            d = jnp.zeros((SUB, C), F32)
            for s in range(SUB):
                es = jnp.exp(jnp.minimum(gb - gb[s:s + 1], 0.0))
                cs = jnp.sum(qb * es * kb[s:s + 1], axis=-1, keepdims=True)
                d = jnp.where((colb == b * SUB + s) & (rowb >= s), cs, d)
            if nb == 1:
                a[i] = d
            else:
                dfull = jnp.concatenate(
                    [d if r == b else jnp.zeros((SUB, C), F32) for r in range(nb)], axis=0)
                a[i] = jnp.where((row // SUB == b) & (col // SUB == b), dfull, a[i])
        o.append(_dot(a[i].astype(BF16), v[i]))
    out["o"] = o


def _hgrn2_state_stages(q, k, v, G, st, o_intra, out):
    n = len(k)
    C = k[0].shape[0]
    g_last = [G[i][C - 1:C, :] for i in range(n)]
    kk = [_scaled(k[i], jnp.exp(g_last[i] - G[i])) for i in range(n)]
    st_new = [_dot_tn(v[i], kk[i]) for i in range(n)]
    if st is not None:
        st_new = [st_new[i] + st[i] * jnp.exp(g_last[i]) for i in range(n)]
    out["s"] = st_new
    yield
    if o_intra is not None:
        out["o"] = [o_intra[i] + _dot_nt(_scaled(q[i], jnp.exp(G[i])), st[i].astype(BF16))
                    for i in range(n)]


def _gdn_prep_stages(q, k, gam_c, gam_r, beta_c, out, *, need_o):
    n = len(k)
    C = k[0].shape[0]
    rng = range(n)
    row = lax.broadcasted_iota(jnp.int32, (C, C), 0)
    col = lax.broadcasted_iota(jnp.int32, (C, C), 1)
    if need_o:
        pk = [_dot_nt(jnp.concatenate([q[i], k[i]], axis=0), k[i]) for i in rng]
        qk, kk = [x[:C] for x in pk], [x[C:] for x in pk]
    else:
        kk = [_dot_nt(k[i], k[i]) for i in rng]
    yield
    dec = [jnp.exp(jnp.minimum(gam_c[i] - gam_r[i], 0.0)) for i in rng]
    if need_o:
        out["qkd"] = [jnp.where(row >= col, qk[i] * dec[i], 0.0).astype(BF16) for i in rng]
    a = [jnp.where(row > col, beta_c[i] * kk[i] * dec[i], 0.0) for i in rng]
    eye = jnp.where(row == col, 1.0, 0.0)
    t = [eye - jnp.where(row // 2 == col // 2, a[i], 0.0) for i in rng]
    bs = 2
    while bs < C:
        lm = (row // (2 * bs) == col // (2 * bs)) & (row // bs > col // bs)
        tb = [t[i].astype(BF16) for i in rng]
        tl = [_dot(tb[i], jnp.where(lm, a[i], 0.0).astype(BF16)) for i in rng]
        yield
        t = [t[i] - _dot(tl[i].astype(BF16), tb[i]) for i in rng]
        yield
        bs *= 2
    out["t"] = [t[i].astype(BF16) for i in rng]


def _gdn_state_stages(q, k, v, gam_c, beta_c, t, qkd, s, out):
    n = len(k)
    C = k[0].shape[0]
    rng = range(n)
    vf = [v[i].astype(F32) for i in rng]
    qs = None
    if s is not None:
        egam = [jnp.exp(gam_c[i]) for i in rng]
        sb = [s[i].astype(BF16) for i in rng]
        if qkd is not None:
            qks = [_dot(jnp.concatenate([_scaled(q[i], egam[i]), _scaled(k[i], egam[i])],
                                        axis=0), sb[i]) for i in rng]
            qs, ks = [x[:C] for x in qks], [x[C:] for x in qks]
        else:
            ks = [_dot(_scaled(k[i], egam[i]), sb[i]) for i in rng]
        yield
        rhs = [beta_c[i] * (vf[i] - ks[i]) for i in rng]
    else:
        rhs = [beta_c[i] * vf[i] for i in rng]
    ub = [_dot(t[i], rhs[i].astype(BF16)).astype(BF16) for i in rng]
    yield
    g_last = [gam_c[i][C - 1:C, :] for i in rng]
    s_new = [_dot_tn(_scaled(k[i], jnp.exp(g_last[i] - gam_c[i])), ub[i]) for i in rng]
    if s is not None:
        s_new = [s_new[i] + s[i] * jnp.exp(g_last[i]) for i in rng]
    out["s"] = s_new
    if qkd is not None:
        o = [_dot(qkd[i], ub[i]) for i in rng]
        out["o"] = o if qs is None else [o[i] + qs[i] for i in rng]


def _head_norm(o, w):
    return o * lax.rsqrt(jnp.mean(o * o, axis=-1, keepdims=True) + EPS) * w


def _mixer_kernel(hq_ref, hk_ref, hv_ref, hg_ref, gq_ref, gk_ref, gv_ref, gz_ref,
                  gcol_ref, grow_ref, sh0_ref, sg0_ref, hw_ref, gw_ref,
                  y_ref, sh_ref, sg_ref, t_buf, qkd_buf, oi_buf, bounded_ref, *, bb, tt, chunk):
    j = pl.program_id(1)
    nchunks = tt // chunk

    @pl.when(j == 0)
    def _():
        for b in range(bb):
            sh_ref[b] = sh0_ref[...]
            sg_ref[b] = sg0_ref[...]

    for c in range(nchunks):
        span = None
        for b in range(bb):
            g_mid = hg_ref[b, c * chunk + chunk // 2 - 1:c * chunk + chunk // 2, :]
            g_last = hg_ref[b, (c + 1) * chunk - 1:(c + 1) * chunk, :]
            s = jnp.maximum(-g_mid, g_mid - g_last)
            span = s if span is None else jnp.maximum(span, s)
        bounded_ref[c] = jnp.where(jnp.max(span) <= SAFE_RANGE, 1, 0)

    hw = hw_ref[...]
    gw = gw_ref[...]
    probs = [(b, h) for b in range(bb) for h in range(HEADS)]
    hs = lambda h: slice(h * DH, (h + 1) * DH)

    def chunk_rows(c):
        return pl.ds(pl.multiple_of(c * chunk, chunk), chunk)

    def prep_stages(c, pout, hout, bounded):
        rows = chunk_rows(c)
        ld = lambda ref: [ref[b, rows, hs(h)] for b, h in probs]
        gcol = [gcol_ref[b, rows, :] for b in range(bb)]
        grow = [grow_ref[b, c] for b in range(bb)]
        return (_gdn_prep_stages(ld(gq_ref), ld(gk_ref),
                                 [gcol[b][:, h:h + 1] for b, h in probs],
                                 [grow[b][h:h + 1, :] for b, h in probs],
                                 [gcol[b][:, HEADS + h:HEADS + h + 1] for b, h in probs],
                                 pout, need_o=True),
                _hgrn2_intra_stages(ld(hq_ref), ld(hk_ref), ld(hv_ref), ld(hg_ref), hout,
                                    bounded=bounded))

    def store_prep(c, pout, hout):
        for i in range(len(probs)):
            t_buf[c, i] = pout["t"][i]
            qkd_buf[c, i] = pout["qkd"][i]
            oi_buf[c, i] = hout["o"][i]

    def recur_stages(c, gout, hout):
        rows = chunk_rows(c)
        ld = lambda ref: [ref[b, rows, hs(h)] for b, h in probs]
        gcol = [gcol_ref[b, rows, :] for b in range(bb)]
        n = len(probs)
        return (_gdn_state_stages(ld(gq_ref), ld(gk_ref), ld(gv_ref),
                                  [gcol[b][:, h:h + 1] for b, h in probs],
                                  [gcol[b][:, HEADS + h:HEADS + h + 1] for b, h in probs],
                                  [t_buf[c, i] for i in range(n)],
                                  [qkd_buf[c, i] for i in range(n)],
                                  [sg_ref[b, h] for b, h in probs], gout),
                _hgrn2_state_stages(ld(hq_ref), ld(hk_ref), ld(hv_ref), ld(hg_ref),
                                    [sh_ref[b, h] for b, h in probs],
                                    [oi_buf[c, i] for i in range(n)], hout))

    def store_recur(c, gout, hout):
        rows = chunk_rows(c)
        for i, (b, h) in enumerate(probs):
            sh_ref[b, h] = hout["s"][i]
            sg_ref[b, h] = gout["s"][i]
            y_ref[b, rows, hs(h)] = _head_norm(hout["o"][i], hw).astype(BF16)
            ys = slice(WIDTH + h * DH, WIDTH + (h + 1) * DH)
            y_ref[b, rows, ys] = (_head_norm(gout["o"][i], gw)
                                  * gz_ref[b, rows, hs(h)].astype(F32)).astype(BF16)

    def prep_only(c, bounded):
        pout, hout = {}, {}
        _run_interleaved(*prep_stages(c, pout, hout, bounded))
        store_prep(c, pout, hout)

    def recur_only(c):
        gout, hout = {}, {}
        _run_interleaved(*recur_stages(c, gout, hout))
        store_recur(c, gout, hout)

    def pipelined_step():
        prep_only(0, True)

        def body(c, carry):
            pout, phout, gout, hout = {}, {}, {}, {}
            gprep, hprep = prep_stages(c + 1, pout, phout, True)
            gstate, hstate = recur_stages(c, gout, hout)
            _run_interleaved(gprep, _spread(gstate, RECUR_GAP), _spread(hstate, RECUR_GAP),
                             hprep)
            store_prep(c + 1, pout, phout)
            store_recur(c, gout, hout)
            return carry

        lax.fori_loop(0, nchunks - 1, body, 0)
        recur_only(nchunks - 1)

    def plain_step():
        def body(c, carry):
            prep_only(c, False)
            recur_only(c)
            return carry

        lax.fori_loop(0, nchunks, body, 0)

    all_bounded = bounded_ref[0] == 1
    for c in range(1, nchunks):
        all_bounded = jnp.logical_and(all_bounded, bounded_ref[c] == 1)
    pl.when(all_bounded)(pipelined_step)
    pl.when(jnp.logical_not(all_bounded))(plain_step)


def _mixer_call(hq, hk, hv, hg, gq, gk, gv, gz, gcol, grow, sh0, sg0, hw, gw, *, bb, tt, chunk):
    B, T, _ = hq.shape
    nchunks = tt // chunk
    tok_spec = lambda width: pl.BlockSpec((bb, tt, width), lambda i, j: (i, j, 0))
    const = lambda shape: pl.BlockSpec(shape, lambda i, j: (0,) * len(shape))
    return pl.pallas_call(
        functools.partial(_mixer_kernel, bb=bb, tt=tt, chunk=chunk),
        grid=(B // bb, T // tt),
        in_specs=[tok_spec(WIDTH), tok_spec(WIDTH), tok_spec(WIDTH), tok_spec(WIDTH),
                  tok_spec(WIDTH), tok_spec(WIDTH), tok_spec(WIDTH), tok_spec(WIDTH),
                  tok_spec(AB_PAD),
                  pl.BlockSpec((bb, nchunks, 2 * HEADS, chunk), lambda i, j: (i, j, 0, 0)),
                  const((HEADS, DH, DH)), const((HEADS, DH, DH)),
                  const((1, DH)), const((1, DH))],
        out_specs=tok_spec(2 * WIDTH),
        out_shape=jax.ShapeDtypeStruct((B, T, 2 * WIDTH), BF16),
        scratch_shapes=[pltpu.VMEM((bb, HEADS, DH, DH), F32),
                        pltpu.VMEM((bb, HEADS, DH, DH), F32),
                        pltpu.VMEM((nchunks, bb * HEADS, chunk, chunk), BF16),
                        pltpu.VMEM((nchunks, bb * HEADS, chunk, chunk), BF16),
                        pltpu.VMEM((nchunks, bb * HEADS, chunk, DH), F32),
                        pltpu.SMEM((nchunks,), jnp.int32)],
        compiler_params=pltpu.CompilerParams(
            dimension_semantics=("parallel", "arbitrary"), vmem_limit_bytes=VMEM_LIMIT),
        name="mixer",
    )(hq, hk, hv, hg, gq, gk, gv, gz, gcol, grow, sh0, sg0, hw, gw)


def _meta_state_kernel(hk_ref, hv_ref, hg_ref, gk_ref, gv_ref, gcol_ref, grow_ref,
                       sh_ref, sg_ref):
    gcol = gcol_ref[0]
    grow = grow_ref[0, 0]
    hs = lambda h: slice(h * DH, (h + 1) * DH)
    heads = range(HEADS)
    hk = [hk_ref[0, :, hs(h)] for h in heads]
    gk = [gk_ref[0, :, hs(h)] for h in heads]
    gam_c = [gcol[:, h:h + 1] for h in heads]
    beta_c = [gcol[:, HEADS + h:HEADS + h + 1] for h in heads]
    prep, hout, gout = {}, {}, {}
    _run_interleaved(
        _gdn_prep_stages(None, gk, gam_c, [grow[h:h + 1, :] for h in heads], beta_c, prep,
                         need_o=False),
        _hgrn2_state_stages(None, hk, [hv_ref[0, :, hs(h)] for h in heads],
                            [hg_ref[0, :, hs(h)] for h in heads], None, None, hout))
    _run_interleaved(
        _gdn_state_stages(None, gk, [gv_ref[0, :, hs(h)] for h in heads], gam_c, beta_c,
                          prep["t"], None, None, gout))
    for h in heads:
        sh_ref[h] = hout["s"][h]
        sg_ref[h] = gout["s"][h]


def _meta_state_call(hk, hv, hg, gk, gv, gcol, grow):
    state = jax.ShapeDtypeStruct((HEADS, DH, DH), F32)
    return pl.pallas_call(
        _meta_state_kernel,
        out_shape=[state, state],
        name="meta_state",
    )(hk, hv, hg, gk, gv, gcol, grow)


def _out_kernel(y_ref, x_ref, nw_ref, wz_ref, w_ref, fw_ref, o_ref):
    x = x_ref[...]
    ms = jnp.mean(x * x, axis=-1, keepdims=True)
    u = (x * lax.rsqrt(ms + EPS) * nw_ref[...]).astype(BF16)
    gate = _silu(_dot(u, wz_ref[...]))
    yh = (y_ref[:, 0:WIDTH].astype(F32) * gate).astype(BF16)
    h = x + _dot(yh, w_ref[0:WIDTH, :]) + _dot(y_ref[:, WIDTH:2 * WIDTH], w_ref[WIDTH:2 * WIDTH, :])
    ms = jnp.mean(h * h, axis=-1, keepdims=True)
    o_ref[...] = h * lax.rsqrt(ms + EPS) * fw_ref[...]


def _out_call(y, x, nw, wz, w, fw, *, tm):
    R = x.shape[0]
    row_spec = pl.BlockSpec((tm, D_MODEL), lambda i: (i, 0))
    const = lambda shape: pl.BlockSpec(shape, lambda i: (0, 0))
    return pl.pallas_call(
        _out_kernel,
        grid=(R // tm,),
        in_specs=[row_spec, row_spec, const((1, D_MODEL)), const((D_MODEL, WIDTH)),
                  const((2 * WIDTH, D_MODEL)), const((1, D_MODEL))],
        out_specs=row_spec,
        out_shape=jax.ShapeDtypeStruct((R, D_MODEL), F32),
        compiler_params=pltpu.CompilerParams(
            dimension_semantics=("parallel",), vmem_limit_bytes=VMEM_LIMIT),
        name="out",
    )(y, x, nw, wz, w, fw)


def kernel(x, meta_tokens, norm_w, w_in, conv_w, hg_lb_logits, hg_norm_w, gdn_A_log,
           gdn_dt_bias, gdn_norm_w, w_out, final_norm_w):
    B, T, D = x.shape
    assert D == D_MODEL and norm_w.shape[0] == 1 and meta_tokens.shape == (N_META, D_MODEL)
    assert w_in.shape == (1, D_MODEL, 8 * WIDTH + 2 * HEADS)

    w = jnp.pad(w_in[0], ((0, 0), (0, AB_PAD - 2 * HEADS))).astype(BF16)
    pad = lambda v: jnp.zeros((1, AB_PAD), F32).at[0, :HEADS].set(v.astype(F32))
    alog, dtb = pad(gdn_A_log[0]), pad(gdn_dt_bias[0])
    nw = norm_w[0].reshape(1, D_MODEL)
    cw = conv_w[0]
    proj = functools.partial(_proj_call, nw=nw, w=w, cw=cw, lbl=hg_lb_logits, alog=alog,
                             dtb=dtb)

    zero_tail = jnp.zeros((CONV_HIST, 3 * WIDTH), F32)
    m = proj(meta_tokens[None], tail=zero_tail, tm=N_META, chunk=N_META, emit_tail=True)
    _, mhk, mhv, mhg, _, mgk, mgv, _, mgcol, mgrow, tail = m
    sh0, sg0 = _meta_state_call(mhk, mhv, mhg, mgk, mgv, mgcol, mgrow)

    hq, hk, hv, hg, gq, gk, gv, gz, gcol, grow = proj(
        x, tail=tail, tm=PROJ_ROWS, chunk=CHUNK, emit_tail=False)
    y = _mixer_call(hq, hk, hv, hg, gq, gk, gv, gz, gcol, grow, sh0, sg0,
                    hg_norm_w[0].reshape(1, DH), gdn_norm_w[0].reshape(1, DH),
                    bb=MIXER_BATCH, tt=MIXER_ROWS, chunk=CHUNK)
    wz = w_in[0, :, 3 * WIDTH:4 * WIDTH].astype(BF16)
    out = _out_call(y.reshape(B * T, 2 * WIDTH), x.reshape(B * T, D_MODEL), nw, wz,
                    w_out[0].astype(BF16), final_norm_w.reshape(1, D_MODEL), tm=OUT_ROWS)
    return out.reshape(B, T, D_MODEL)
```

```python
import functools

import jax
import jax.numpy as jnp
from jax import lax
from jax.experimental import pallas as pl
from jax.experimental.pallas import tpu as pltpu

F32 = jnp.float32
BF16 = jnp.bfloat16

D_MODEL = 1024
N_META = 16
HEADS = 4
DH = 128
WIDTH = HEADS * DH
CONV_W = 4
EPS = 1e-6
AB_PAD = 128
IN_PAD = 8 * WIDTH + AB_PAD

CHUNK = 64
SUB = 16
SAFE_RANGE = 60.0
CONV_HIST = 8
CUM_ROWS = 256
ROW_BLOCK = 128
PROJ_ROWS = 1024
MIXER_BATCH = 8
MIXER_ROWS = 128
OUT_ROWS = 1024
RECUR_GAP = 3
VMEM_LIMIT = 56 * 1024 * 1024


def _dot(a, b):
    return jnp.dot(a, b, preferred_element_type=F32)


def _dot_nt(a, b):
    return lax.dot_general(a, b, (((1,), (1,)), ((), ())), preferred_element_type=F32)


def _dot_tn(a, b):
    return lax.dot_general(a, b, (((0,), (0,)), ((), ())), preferred_element_type=F32)


def _split3(x):
    x1 = x.astype(BF16)
    r = x - x1.astype(F32)
    x2 = r.astype(BF16)
    x3 = (r - x2.astype(F32)).astype(BF16)
    return x1, x2, x3


def _silu(x):
    return x * jax.nn.sigmoid(x)


def _softplus(x):
    return jnp.maximum(x, 0.0) + jnp.log1p(jnp.exp(-jnp.abs(x)))


def _chunk_cumsum(x, chunk):
    rows = x.shape[0]
    blk = min(rows, CUM_ROWS)
    r = lax.broadcasted_iota(jnp.int32, (blk, blk), 0)
    c = lax.broadcasted_iota(jnp.int32, (blk, blk), 1)
    tri = jnp.where((r // chunk == c // chunk) & (c <= r), 1.0, 0.0).astype(BF16)
    outs = []
    for i in range(rows // blk):
        parts = _split3(x[i * blk:(i + 1) * blk])
        outs.append(_dot(tri, parts[0]) + _dot(tri, parts[1]) + _dot(tri, parts[2]))
    return outs[0] if len(outs) == 1 else jnp.concatenate(outs, axis=0)


def _proj_kernel(x_ref, nw_ref, w_ref, cw_ref, lbl_ref, alog_ref, dtb_ref, tail_ref,
                 hq_ref, hk_ref, hv_ref, hg_ref, gq_ref, gk_ref, gv_ref, gz_ref,
                 gcol_ref, grow_ref, *rest, tm, chunk, emit_tail):
    if emit_tail:
        tail_out_ref, cbuf, logf_buf, ab_buf = rest
    else:
        cbuf, logf_buf, ab_buf = rest
    j = pl.program_id(1)

    @pl.when(j == 0)
    def _():
        cbuf[0:CONV_HIST, :] = tail_ref[...]

    lbl = lbl_ref[...]
    e = jnp.exp(lbl - jnp.max(lbl, axis=0, keepdims=True))
    lb = e[0:1] / jnp.sum(e, axis=0, keepdims=True)

    rb = min(tm, ROW_BLOCK)
    for r0 in range(0, tm, rb):
        rows = slice(r0, r0 + rb)
        x = x_ref[0, rows, :]
        ms = jnp.mean(x * x, axis=-1, keepdims=True)
        u = (x * lax.rsqrt(ms + EPS) * nw_ref[...]).astype(BF16)

        def proj(a, b, u=u):
            return _dot(u, w_ref[:, a:b])

        cbuf[CONV_HIST + r0:CONV_HIST + r0 + rb, :] = proj(4 * WIDTH, 7 * WIDTH)
        xe = cbuf[r0:r0 + CONV_HIST + rb, :]
        acc = cw_ref[0:1, :] * xe
        for t in range(1, CONV_W):
            acc = pltpu.roll(acc, 1, 0) + cw_ref[t:t + 1, :] * xe
        qkv = _silu(acc[CONV_HIST:CONV_HIST + rb])
        for h in range(HEADS):
            q = qkv[:, h * DH:(h + 1) * DH]
            k = qkv[:, WIDTH + h * DH:WIDTH + (h + 1) * DH]
            q = q * lax.rsqrt(jnp.sum(q * q, axis=-1, keepdims=True) + EPS) * (DH ** -0.5)
            k = k * lax.rsqrt(jnp.sum(k * k, axis=-1, keepdims=True) + EPS)
            gq_ref[0, rows, h * DH:(h + 1) * DH] = q.astype(BF16)
            gk_ref[0, rows, h * DH:(h + 1) * DH] = k.astype(BF16)
        gv_ref[0, rows, :] = qkv[:, 2 * WIDTH:3 * WIDTH].astype(BF16)
        gz_ref[0, rows, :] = _silu(proj(7 * WIDTH, 8 * WIDTH)).astype(BF16)

        hq_ref[0, rows, :] = _silu(proj(0, WIDTH)).astype(BF16)
        f = lb + (1.0 - lb) * jax.nn.sigmoid(proj(WIDTH, 2 * WIDTH))
        hk_ref[0, rows, :] = (1.0 - f).astype(BF16)
        logf_buf[rows, :] = jnp.log(f)
        hv_ref[0, rows, :] = proj(2 * WIDTH, 3 * WIDTH).astype(BF16)

        ab_buf[rows, :] = proj(8 * WIDTH, 8 * WIDTH + AB_PAD)

    for r0 in range(0, tm, rb):
        rows = slice(r0, r0 + rb)
        hg_ref[0, rows, :] = _chunk_cumsum(logf_buf[rows, :], chunk)
        p = ab_buf[rows, :]
        lane = lax.broadcasted_iota(jnp.int32, p.shape, 1)
        g = -jnp.exp(alog_ref[...]) * _softplus(p + dtb_ref[...])
        gam = _chunk_cumsum(jnp.where(lane < HEADS, g, 0.0), chunk)
        comb = jnp.where(lane < HEADS, gam,
                         jnp.where(lane < 2 * HEADS, jax.nn.sigmoid(p), 0.0))
        gcol_ref[0, rows, :] = comb
        for c in range(rb // chunk):
            blk = comb[c * chunk:(c + 1) * chunk, :]
            grow_ref[0, r0 // chunk + c] = blk.T[0:2 * HEADS, :]

    hist = cbuf[tm:tm + CONV_HIST, :]
    cbuf[0:CONV_HIST, :] = hist
    if emit_tail:
        tail_out_ref[...] = hist


def _proj_call(x, nw, w, cw, lbl, alog, dtb, tail, *, tm, chunk, emit_tail):
    B, T, _ = x.shape
    nt = T // tm
    tok = lambda width, dt: jax.ShapeDtypeStruct((B, T, width), dt)
    tok_spec = lambda width: pl.BlockSpec((1, tm, width), lambda b, j: (b, j, 0))
    const = lambda shape: pl.BlockSpec(shape, lambda b, j: (0,) * len(shape))
    out_shape = [tok(WIDTH, BF16), tok(WIDTH, BF16), tok(WIDTH, BF16), tok(WIDTH, F32),
                 tok(WIDTH, BF16), tok(WIDTH, BF16), tok(WIDTH, BF16), tok(WIDTH, BF16),
                 tok(AB_PAD, F32),
                 jax.ShapeDtypeStruct((B, T // chunk, 2 * HEADS, chunk), F32)]
    out_specs = [tok_spec(WIDTH), tok_spec(WIDTH), tok_spec(WIDTH), tok_spec(WIDTH),
                 tok_spec(WIDTH), tok_spec(WIDTH), tok_spec(WIDTH), tok_spec(WIDTH),
                 tok_spec(AB_PAD),
                 pl.BlockSpec((1, tm // chunk, 2 * HEADS, chunk), lambda b, j: (b, j, 0, 0))]
    if emit_tail:
        out_shape.append(jax.ShapeDtypeStruct((CONV_HIST, 3 * WIDTH), F32))
        out_specs.append(const((CONV_HIST, 3 * WIDTH)))
    return pl.pallas_call(
        functools.partial(_proj_kernel, tm=tm, chunk=chunk, emit_tail=emit_tail),
        grid=(B, nt),
        in_specs=[tok_spec(D_MODEL), const((1, D_MODEL)),
                  pl.BlockSpec((D_MODEL, IN_PAD), lambda b, j: (0, 0),
                               pipeline_mode=pl.Buffered(1)),
                  const((CONV_W, 3 * WIDTH)), const(lbl.shape), const((1, AB_PAD)),
                  const((1, AB_PAD)), const((CONV_HIST, 3 * WIDTH))],
        out_specs=out_specs,
        out_shape=out_shape,
        scratch_shapes=[pltpu.VMEM((tm + CONV_HIST, 3 * WIDTH), F32),
                        pltpu.VMEM((tm, WIDTH), F32), pltpu.VMEM((tm, AB_PAD), F32)],
        compiler_params=pltpu.CompilerParams(
            dimension_semantics=("parallel", "arbitrary"), vmem_limit_bytes=VMEM_LIMIT),
        name="proj_meta" if emit_tail else "proj",
    )(x, nw, w, cw, lbl, alog, dtb, tail)


def _run_interleaved(*gens):
    live = list(gens)
    while live:
        for g in list(live):
            try:
                next(g)
            except StopIteration:
                live.remove(g)


def _scaled(x, decay):
    return x * decay.astype(BF16)


def _spread(gen, gap):
    for _ in gen:
        yield
        for _ in range(gap):
            yield


def _hgrn2_intra_stages(q, k, v, G, out, *, bounded):
    n = len(q)
    C = q[0].shape[0]
    row = lax.broadcasted_iota(jnp.int32, (C, C), 0)
    col = lax.broadcasted_iota(jnp.int32, (C, C), 1)
    if bounded:
        g_mid = [G[i][C // 2 - 1:C // 2, :] for i in range(n)]
        p = [_dot_nt(_scaled(q[i], jnp.exp(G[i] - g_mid[i])),
                     _scaled(k[i], jnp.exp(g_mid[i] - G[i]))) for i in range(n)]
        yield
        out["o"] = [_dot(jnp.where(row >= col, p[i], 0.0).astype(BF16), v[i]) for i in range(n)]
        return
    nb = C // SUB
    qf = [q[i].astype(F32) for i in range(n)]
    kf = [k[i].astype(F32) for i in range(n)]
    a = [jnp.zeros((C, C), F32) for _ in range(n)]
    g_end = [[G[i][(j + 1) * SUB - 1:(j + 1) * SUB, :] for j in range(nb)] for i in range(n)]
    if nb > 1:
        p = []
        for i in range(n):
            g_ref = jnp.concatenate(
                [jnp.broadcast_to(g_end[i][j], (SUB, DH)) for j in range(nb)], axis=0)
            kp = (kf[i] * jnp.exp(g_ref - G[i])).astype(BF16)
            qp = jnp.concatenate(
                [qf[i] * jnp.exp(jnp.minimum(G[i] - g_end[i][j], 0.0)) for j in range(nb - 1)],
                axis=0).astype(BF16)
            p.append(_dot_nt(qp, kp))
        yield
        for i in range(n):
            for j in range(nb - 1):
                m = (col // SUB == j) & (row // SUB > j)
                a[i] = jnp.where(m, p[i][j * C:(j + 1) * C], a[i])
    rowb = lax.broadcasted_iota(jnp.int32, (SUB, C), 0)
    colb = lax.broadcasted_iota(jnp.int32, (SUB, C), 1)
    o = []
    for i in range(n):
        for b in range(nb):
            gb = G[i][b * SUB:(b + 1) * SUB]
            qb = qf[i][b * SUB:(b + 1) * SUB]
            kb = kf[i][b * SUB:(b + 1) * SUB]
            d = jnp.zeros((SUB, C), F32)
            for s in range(SUB):
                es = jnp.exp(jnp.minimum(gb - gb[s:s + 1], 0.0))
                cs = jnp.sum(qb * es * kb[s:s + 1], axis=-1, keepdims=True)
                d = jnp.where((colb == b * SUB + s) & (rowb >= s), cs, d)
            if nb == 1:
                a[i] = d
            else:
                dfull = jnp.concatenate(
                    [d if r == b else jnp.zeros((SUB, C), F32) for r in range(nb)], axis=0)
                a[i] = jnp.where((row // SUB == b) & (col // SUB == b), dfull, a[i])
        o.append(_dot(a[i].astype(BF16), v[i]))
    out["o"] = o


def _hgrn2_state_stages(q, k, v, G, st, o_intra, out):
    n = len(k)
    C = k[0].shape[0]
    g_last = [G[i][C - 1:C, :] for i in range(n)]
    kk = [_scaled(k[i], jnp.exp(g_last[i] - G[i])) for i in range(n)]
    st_new = [_dot_tn(v[i], kk[i]) for i in range(n)]
    if st is not None:
        st_new = [st_new[i] + st[i] * jnp.exp(g_last[i]) for i in range(n)]
    out["s"] = st_new
    yield
    if o_intra is not None:
        out["o"] = [o_intra[i] + _dot_nt(_scaled(q[i], jnp.exp(G[i])), st[i].astype(BF16))
                    for i in range(n)]


def _gdn_prep_stages(q, k, gam_c, gam_r, beta_c, out, *, need_o):
    n = len(k)
    C = k[0].shape[0]
    rng = range(n)
    row = lax.broadcasted_iota(jnp.int32, (C, C), 0)
    col = lax.broadcasted_iota(jnp.int32, (C, C), 1)
    if need_o:
        pk = [_dot_nt(jnp.concatenate([q[i], k[i]], axis=0), k[i]) for i in rng]
        qk, kk = [x[:C] for x in pk], [x[C:] for x in pk]
    else:
        kk = [_dot_nt(k[i], k[i]) for i in rng]
    yield
    dec = [jnp.exp(jnp.minimum(gam_c[i] - gam_r[i], 0.0)) for i in rng]
    if need_o:
        out["qkd"] = [jnp.where(row >= col, qk[i] * dec[i], 0.0).astype(BF16) for i in rng]
    a = [jnp.where(row > col, beta_c[i] * kk[i] * dec[i], 0.0) for i in rng]
    eye = jnp.where(row == col, 1.0, 0.0)
    t = [eye - jnp.where(row // 2 == col // 2, a[i], 0.0) for i in rng]
    bs = 2
    while bs < C:
        lm = (row // (2 * bs) == col // (2 * bs)) & (row // bs > col // bs)
        tb = [t[i].astype(BF16) for i in rng]
        tl = [_dot(tb[i], jnp.where(lm, a[i], 0.0).astype(BF16)) for i in rng]
        yield
        t = [t[i] - _dot(tl[i].astype(BF16), tb[i]) for i in rng]
        yield
        bs *= 2
    out["t"] = [t[i].astype(BF16) for i in rng]


def _gdn_state_stages(q, k, v, gam_c, beta_c, t, qkd, s, out):
    n = len(k)
    C = k[0].shape[0]
    rng = range(n)
    vf = [v[i].astype(F32) for i in rng]
    qs = None
    if s is not None:
        egam = [jnp.exp(gam_c[i]) for i in rng]
        sb = [s[i].astype(BF16) for i in rng]
        if qkd is not None:
            qks = [_dot(jnp.concatenate([_scaled(q[i], egam[i]), _scaled(k[i], egam[i])],
                                        axis=0), sb[i]) for i in rng]
            qs, ks = [x[:C] for x in qks], [x[C:] for x in qks]
        else:
            ks = [_dot(_scaled(k[i], egam[i]), sb[i]) for i in rng]
        yield
        rhs = [beta_c[i] * (vf[i] - ks[i]) for i in rng]
    else:
        rhs = [beta_c[i] * vf[i] for i in rng]
    ub = [_dot(t[i], rhs[i].astype(BF16)).astype(BF16) for i in rng]
    yield
    g_last = [gam_c[i][C - 1:C, :] for i in rng]
    s_new = [_dot_tn(_scaled(k[i], jnp.exp(g_last[i] - gam_c[i])), ub[i]) for i in rng]
    if s is not None:
        s_new = [s_new[i] + s[i] * jnp.exp(g_last[i]) for i in rng]
    out["s"] = s_new
    if qkd is not None:
        o = [_dot(qkd[i], ub[i]) for i in rng]
        out["o"] = o if qs is None else [o[i] + qs[i] for i in rng]


def _head_norm(o, w):
    return o * lax.rsqrt(jnp.mean(o * o, axis=-1, keepdims=True) + EPS) * w


def _mixer_kernel(hq_ref, hk_ref, hv_ref, hg_ref, gq_ref, gk_ref, gv_ref, gz_ref,
                  gcol_ref, grow_ref, sh0_ref, sg0_ref, hw_ref, gw_ref,
                  y_ref, sh_ref, sg_ref, t_buf, qkd_buf, oi_buf, bounded_ref, *, bb, tt, chunk):
    j = pl.program_id(1)
    nchunks = tt // chunk

    @pl.when(j == 0)
    def _():
        for b in range(bb):
            sh_ref[b] = sh0_ref[...]
            sg_ref[b] = sg0_ref[...]

    for c in range(nchunks):
        span = None
        for b in range(bb):
            g_mid = hg_ref[b, c * chunk + chunk // 2 - 1:c * chunk + chunk // 2, :]
            g_last = hg_ref[b, (c + 1) * chunk - 1:(c + 1) * chunk, :]
            s = jnp.maximum(-g_mid, g_mid - g_last)
            span = s if span is None else jnp.maximum(span, s)
        bounded_ref[c] = jnp.where(jnp.max(span) <= SAFE_RANGE, 1, 0)

    hw = hw_ref[...]
    gw = gw_ref[...]
    probs = [(b, h) for b in range(bb) for h in range(HEADS)]
    hs = lambda h: slice(h * DH, (h + 1) * DH)

    def chunk_rows(c):
        return pl.ds(pl.multiple_of(c * chunk, chunk), chunk)

    def prep_stages(c, pout, hout, bounded):
        rows = chunk_rows(c)
        ld = lambda ref: [ref[b, rows, hs(h)] for b, h in probs]
        gcol = [gcol_ref[b, rows, :] for b in range(bb)]
        grow = [grow_ref[b, c] for b in range(bb)]
        return (_gdn_prep_stages(ld(gq_ref), ld(gk_ref),
                                 [gcol[b][:, h:h + 1] for b, h in probs],
                                 [grow[b][h:h + 1, :] for b, h in probs],
                                 [gcol[b][:, HEADS + h:HEADS + h + 1] for b, h in probs],
                                 pout, need_o=True),
                _hgrn2_intra_stages(ld(hq_ref), ld(hk_ref), ld(hv_ref), ld(hg_ref), hout,
                                    bounded=bounded))

    def store_prep(c, pout, hout):
        for i in range(len(probs)):
            t_buf[c, i] = pout["t"][i]
            qkd_buf[c, i] = pout["qkd"][i]
            oi_buf[c, i] = hout["o"][i]

    def recur_stages(c, gout, hout):
        rows = chunk_rows(c)
        ld = lambda ref: [ref[b, rows, hs(h)] for b, h in probs]
        gcol = [gcol_ref[b, rows, :] for b in range(bb)]
        n = len(probs)
        return (_gdn_state_stages(ld(gq_ref), ld(gk_ref), ld(gv_ref),
                                  [gcol[b][:, h:h + 1] for b, h in probs],
                                  [gcol[b][:, HEADS + h:HEADS + h + 1] for b, h in probs],
                                  [t_buf[c, i] for i in range(n)],
                                  [qkd_buf[c, i] for i in range(n)],
                                  [sg_ref[b, h] for b, h in probs], gout),
                _hgrn2_state_stages(ld(hq_ref), ld(hk_ref), ld(hv_ref), ld(hg_ref),
                                    [sh_ref[b, h] for b, h in probs],
                                    [oi_buf[c, i] for i in range(n)], hout))

    def store_recur(c, gout, hout):
        rows = chunk_rows(c)
        for i, (b, h) in enumerate(probs):
            sh_ref[b, h] = hout["s"][i]
            sg_ref[b, h] = gout["s"][i]
            y_ref[b, rows, hs(h)] = _head_norm(hout["o"][i], hw).astype(BF16)
            ys = slice(WIDTH + h * DH, WIDTH + (h + 1) * DH)
            y_ref[b, rows, ys] = (_head_norm(gout["o"][i], gw)
                                  * gz_ref[b, rows, hs(h)].astype(F32)).astype(BF16)

    def prep_only(c, bounded):
        pout, hout = {}, {}
        _run_interleaved(*prep_stages(c, pout, hout, bounded))
        store_prep(c, pout, hout)

    def recur_only(c):
        gout, hout = {}, {}
        _run_interleaved(*recur_stages(c, gout, hout))
        store_recur(c, gout, hout)

    def pipelined_step():
        prep_only(0, True)

        def body(c, carry):
            pout, phout, gout, hout = {}, {}, {}, {}
            gprep, hprep = prep_stages(c + 1, pout, phout, True)
            gstate, hstate = recur_stages(c, gout, hout)
            _run_interleaved(gprep, _spread(gstate, RECUR_GAP), _spread(hstate, RECUR_GAP),
                             hprep)
            store_prep(c + 1, pout, phout)
            store_recur(c, gout, hout)
            return carry

        lax.fori_loop(0, nchunks - 1, body, 0)
        recur_only(nchunks - 1)

    def plain_step():
        def body(c, carry):
            prep_only(c, False)
            recur_only(c)
            return carry

        lax.fori_loop(0, nchunks, body, 0)

    all_bounded = bounded_ref[0] == 1
    for c in range(1, nchunks):
        all_bounded = jnp.logical_and(all_bounded, bounded_ref[c] == 1)
    pl.when(all_bounded)(pipelined_step)
    pl.when(jnp.logical_not(all_bounded))(plain_step)


def _mixer_call(hq, hk, hv, hg, gq, gk, gv, gz, gcol, grow, sh0, sg0, hw, gw, *, bb, tt, chunk):
    B, T, _ = hq.shape
    nchunks = tt // chunk
    tok_spec = lambda width: pl.BlockSpec((bb, tt, width), lambda i, j: (i, j, 0))
    const = lambda shape: pl.BlockSpec(shape, lambda i, j: (0,) * len(shape))
    return pl.pallas_call(
        functools.partial(_mixer_kernel, bb=bb, tt=tt, chunk=chunk),
        grid=(B // bb, T // tt),
        in_specs=[tok_spec(WIDTH), tok_spec(WIDTH), tok_spec(WIDTH), tok_spec(WIDTH),
                  tok_spec(WIDTH), tok_spec(WIDTH), tok_spec(WIDTH), tok_spec(WIDTH),
                  tok_spec(AB_PAD),
                  pl.BlockSpec((bb, nchunks, 2 * HEADS, chunk), lambda i, j: (i, j, 0, 0)),
                  const((HEADS, DH, DH)), const((HEADS, DH, DH)),
                  const((1, DH)), const((1, DH))],
        out_specs=tok_spec(2 * WIDTH),
        out_shape=jax.ShapeDtypeStruct((B, T, 2 * WIDTH), BF16),
        scratch_shapes=[pltpu.VMEM((bb, HEADS, DH, DH), F32),
                        pltpu.VMEM((bb, HEADS, DH, DH), F32),
                        pltpu.VMEM((nchunks, bb * HEADS, chunk, chunk), BF16),
                        pltpu.VMEM((nchunks, bb * HEADS, chunk, chunk), BF16),
                        pltpu.VMEM((nchunks, bb * HEADS, chunk, DH), F32),
                        pltpu.SMEM((nchunks,), jnp.int32)],
        compiler_params=pltpu.CompilerParams(
            dimension_semantics=("parallel", "arbitrary"), vmem_limit_bytes=VMEM_LIMIT),
        name="mixer",
    )(hq, hk, hv, hg, gq, gk, gv, gz, gcol, grow, sh0, sg0, hw, gw)


def _meta_state_kernel(hk_ref, hv_ref, hg_ref, gk_ref, gv_ref, gcol_ref, grow_ref,
                       sh_ref, sg_ref):
    gcol = gcol_ref[0]
    grow = grow_ref[0, 0]
    hs = lambda h: slice(h * DH, (h + 1) * DH)
    heads = range(HEADS)
    hk = [hk_ref[0, :, hs(h)] for h in heads]
    gk = [gk_ref[0, :, hs(h)] for h in heads]
    gam_c = [gcol[:, h:h + 1] for h in heads]
    beta_c = [gcol[:, HEADS + h:HEADS + h + 1] for h in heads]
    prep, hout, gout = {}, {}, {}
    _run_interleaved(
        _gdn_prep_stages(None, gk, gam_c, [grow[h:h + 1, :] for h in heads], beta_c, prep,
                         need_o=False),
        _hgrn2_state_stages(None, hk, [hv_ref[0, :, hs(h)] for h in heads],
                            [hg_ref[0, :, hs(h)] for h in heads], None, None, hout))
    _run_interleaved(
        _gdn_state_stages(None, gk, [gv_ref[0, :, hs(h)] for h in heads], gam_c, beta_c,
                          prep["t"], None, None, gout))
    for h in heads:
        sh_ref[h] = hout["s"][h]
        sg_ref[h] = gout["s"][h]


def _meta_state_call(hk, hv, hg, gk, gv, gcol, grow):
    state = jax.ShapeDtypeStruct((HEADS, DH, DH), F32)
    return pl.pallas_call(
        _meta_state_kernel,
        out_shape=[state, state],
        name="meta_state",
    )(hk, hv, hg, gk, gv, gcol, grow)


def _out_kernel(y_ref, x_ref, nw_ref, wz_ref, w_ref, fw_ref, o_ref):
    x = x_ref[...]
    ms = jnp.mean(x * x, axis=-1, keepdims=True)
    u = (x * lax.rsqrt(ms + EPS) * nw_ref[...]).astype(BF16)
    gate = _silu(_dot(u, wz_ref[...]))
    yh = (y_ref[:, 0:WIDTH].astype(F32) * gate).astype(BF16)
    h = x + _dot(yh, w_ref[0:WIDTH, :]) + _dot(y_ref[:, WIDTH:2 * WIDTH], w_ref[WIDTH:2 * WIDTH, :])
    ms = jnp.mean(h * h, axis=-1, keepdims=True)
    o_ref[...] = h * lax.rsqrt(ms + EPS) * fw_ref[...]


def _out_call(y, x, nw, wz, w, fw, *, tm):
    R = x.shape[0]
    row_spec = pl.BlockSpec((tm, D_MODEL), lambda i: (i, 0))
    const = lambda shape: pl.BlockSpec(shape, lambda i: (0, 0))
    return pl.pallas_call(
        _out_kernel,
        grid=(R // tm,),
        in_specs=[row_spec, row_spec, const((1, D_MODEL)), const((D_MODEL, WIDTH)),
                  const((2 * WIDTH, D_MODEL)), const((1, D_MODEL))],
        out_specs=row_spec,
        out_shape=jax.ShapeDtypeStruct((R, D_MODEL), F32),
        compiler_params=pltpu.CompilerParams(
            dimension_semantics=("parallel",), vmem_limit_bytes=VMEM_LIMIT),
        name="out",
    )(y, x, nw, wz, w, fw)


def kernel(x, meta_tokens, norm_w, w_in, conv_w, hg_lb_logits, hg_norm_w, gdn_A_log,
           gdn_dt_bias, gdn_norm_w, w_out, final_norm_w):
    B, T, D = x.shape
    assert D == D_MODEL and norm_w.shape[0] == 1 and meta_tokens.shape == (N_META, D_MODEL)
    assert w_in.shape == (1, D_MODEL, 8 * WIDTH + 2 * HEADS)

    w = jnp.pad(w_in[0], ((0, 0), (0, AB_PAD - 2 * HEADS))).astype(BF16)
    pad = lambda v: jnp.zeros((1, AB_PAD), F32).at[0, :HEADS].set(v.astype(F32))
    alog, dtb = pad(gdn_A_log[0]), pad(gdn_dt_bias[0])
    nw = norm_w[0].reshape(1, D_MODEL)
    cw = conv_w[0]
    proj = functools.partial(_proj_call, nw=nw, w=w, cw=cw, lbl=hg_lb_logits, alog=alog,
                             dtb=dtb)

    zero_tail = jnp.zeros((CONV_HIST, 3 * WIDTH), F32)
    m = proj(meta_tokens[None], tail=zero_tail, tm=N_META, chunk=N_META, emit_tail=True)
    _, mhk, mhv, mhg, _, mgk, mgv, _, mgcol, mgrow, tail = m
    sh0, sg0 = _meta_state_call(mhk, mhv, mhg, mgk, mgv, mgcol, mgrow)

    hq, hk, hv, hg, gq, gk, gv, gz, gcol, grow = proj(
        x, tail=tail, tm=PROJ_ROWS, chunk=CHUNK, emit_tail=False)
    y = _mixer_call(hq, hk, hv, hg, gq, gk, gv, gz, gcol, grow, sh0, sg0,
                    hg_norm_w[0].reshape(1, DH), gdn_norm_w[0].reshape(1, DH),
                    bb=MIXER_BATCH, tt=MIXER_ROWS, chunk=CHUNK)
    wz = w_in[0, :, 3 * WIDTH:4 * WIDTH].astype(BF16)
    out = _out_call(y.reshape(B * T, 2 * WIDTH), x.reshape(B * T, D_MODEL), nw, wz,
                    w_out[0].astype(BF16), final_norm_w.reshape(1, D_MODEL), tm=OUT_ROWS)
    return out.reshape(B, T, D_MODEL)
```

```python
import functools

import jax
import jax.numpy as jnp
from jax import lax
from jax.experimental import pallas as pl
from jax.experimental.pallas import tpu as pltpu

F32 = jnp.float32
BF16 = jnp.bfloat16

D_MODEL = 1024
N_META = 16
HEADS = 4
DH = 128
WIDTH = HEADS * DH
CONV_W = 4
EPS = 1e-6
AB_PAD = 128
IN_PAD = 8 * WIDTH + AB_PAD

CHUNK = 64
SUB = 16
SAFE_RANGE = 60.0
CONV_HIST = 8
CUM_ROWS = 256
ROW_BLOCK = 128
PROJ_ROWS = 1024
MIXER_BATCH = 8
MIXER_ROWS = 128
OUT_ROWS = 1024
OUT_ROW_BLOCK = 256
RECUR_GAP = 3
VMEM_LIMIT = 56 * 1024 * 1024


def _dot(a, b):
    return jnp.dot(a, b, preferred_element_type=F32)


def _dot_nt(a, b):
    return lax.dot_general(a, b, (((1,), (1,)), ((), ())), preferred_element_type=F32)


def _dot_tn(a, b):
    return lax.dot_general(a, b, (((0,), (0,)), ((), ())), preferred_element_type=F32)


def _split3(x):
    x1 = x.astype(BF16)
    r = x - x1.astype(F32)
    x2 = r.astype(BF16)
    x3 = (r - x2.astype(F32)).astype(BF16)
    return x1, x2, x3


def _silu(x):
    return x * jax.nn.sigmoid(x)


def _softplus(x):
    return jnp.maximum(x, 0.0) + jnp.log1p(jnp.exp(-jnp.abs(x)))


def _chunk_cumsum(x, chunk):
    rows = x.shape[0]
    blk = min(rows, CUM_ROWS)
    r = lax.broadcasted_iota(jnp.int32, (blk, blk), 0)
    c = lax.broadcasted_iota(jnp.int32, (blk, blk), 1)
    tri = jnp.where((r // chunk == c // chunk) & (c <= r), 1.0, 0.0).astype(BF16)
    outs = []
    for i in range(rows // blk):
        parts = _split3(x[i * blk:(i + 1) * blk])
        outs.append(_dot(tri, parts[0]) + _dot(tri, parts[1]) + _dot(tri, parts[2]))
    return outs[0] if len(outs) == 1 else jnp.concatenate(outs, axis=0)


def _proj_kernel(x_ref, nw_ref, w_ref, cw_ref, lbl_ref, alog_ref, dtb_ref, tail_ref,
                 hq_ref, hk_ref, hv_ref, hg_ref, gq_ref, gk_ref, gv_ref, gz_ref,
                 gcol_ref, grow_ref, *rest, tm, chunk, emit_tail):
    if emit_tail:
        tail_out_ref, cbuf, logf_buf, ab_buf = rest
    else:
        cbuf, logf_buf, ab_buf = rest
    j = pl.program_id(1)

    @pl.when(j == 0)
    def _():
        cbuf[0:CONV_HIST, :] = tail_ref[...]

    lbl = lbl_ref[...]
    e = jnp.exp(lbl - jnp.max(lbl, axis=0, keepdims=True))
    lb = e[0:1] / jnp.sum(e, axis=0, keepdims=True)

    rb = min(tm, ROW_BLOCK)
    for r0 in range(0, tm, rb):
        rows = slice(r0, r0 + rb)
        x = x_ref[0, rows, :]
        ms = jnp.mean(x * x, axis=-1, keepdims=True)
        u = (x * lax.rsqrt(ms + EPS) * nw_ref[...]).astype(BF16)

        def proj(a, b, u=u):
            return _dot(u, w_ref[:, a:b])

        cbuf[CONV_HIST + r0:CONV_HIST + r0 + rb, :] = proj(4 * WIDTH, 7 * WIDTH)
        xe = cbuf[r0:r0 + CONV_HIST + rb, :]
        acc = cw_ref[0:1, :] * xe
        for t in range(1, CONV_W):
            acc = pltpu.roll(acc, 1, 0) + cw_ref[t:t + 1, :] * xe
        qkv = _silu(acc[CONV_HIST:CONV_HIST + rb])
        for h in range(HEADS):
            q = qkv[:, h * DH:(h + 1) * DH]
            k = qkv[:, WIDTH + h * DH:WIDTH + (h + 1) * DH]
            q = q * lax.rsqrt(jnp.sum(q * q, axis=-1, keepdims=True) + EPS) * (DH ** -0.5)
            k = k * lax.rsqrt(jnp.sum(k * k, axis=-1, keepdims=True) + EPS)
            gq_ref[0, rows, h * DH:(h + 1) * DH] = q.astype(BF16)
            gk_ref[0, rows, h * DH:(h + 1) * DH] = k.astype(BF16)
        gv_ref[0, rows, :] = qkv[:, 2 * WIDTH:3 * WIDTH].astype(BF16)
        gz_ref[0, rows, :] = _silu(proj(7 * WIDTH, 8 * WIDTH)).astype(BF16)

        hq_ref[0, rows, :] = _silu(proj(0, WIDTH)).astype(BF16)
        f = lb + (1.0 - lb) * jax.nn.sigmoid(proj(WIDTH, 2 * WIDTH))
        hk_ref[0, rows, :] = (1.0 - f).astype(BF16)
        logf_buf[rows, :] = jnp.log(f)
        hv_ref[0, rows, :] = proj(2 * WIDTH, 3 * WIDTH).astype(BF16)

        ab_buf[rows, :] = proj(8 * WIDTH, 8 * WIDTH + AB_PAD)

    for r0 in range(0, tm, rb):
        rows = slice(r0, r0 + rb)
        hg_ref[0, rows, :] = _chunk_cumsum(logf_buf[rows, :], chunk)
        p = ab_buf[rows, :]
        lane = lax.broadcasted_iota(jnp.int32, p.shape, 1)
        g = -jnp.exp(alog_ref[...]) * _softplus(p + dtb_ref[...])
        gam = _chunk_cumsum(jnp.where(lane < HEADS, g, 0.0), chunk)
        comb = jnp.where(lane < HEADS, gam,
                         jnp.where(lane < 2 * HEADS, jax.nn.sigmoid(p), 0.0))
        gcol_ref[0, rows, :] = comb
        for c in range(rb // chunk):
            blk = comb[c * chunk:(c + 1) * chunk, :]
            grow_ref[0, r0 // chunk + c] = blk.T[0:2 * HEADS, :]

    hist = cbuf[tm:tm + CONV_HIST, :]
    cbuf[0:CONV_HIST, :] = hist
    if emit_tail:
        tail_out_ref[...] = hist


def _proj_call(x, nw, w, cw, lbl, alog, dtb, tail, *, tm, chunk, emit_tail):
    B, T, _ = x.shape
    nt = T // tm
    tok = lambda width, dt: jax.ShapeDtypeStruct((B, T, width), dt)
    tok_spec = lambda width: pl.BlockSpec((1, tm, width), lambda b, j: (b, j, 0))
    const = lambda shape: pl.BlockSpec(shape, lambda b, j: (0,) * len(shape))
    out_shape = [tok(WIDTH, BF16), tok(WIDTH, BF16), tok(WIDTH, BF16), tok(WIDTH, F32),
                 tok(WIDTH, BF16), tok(WIDTH, BF16), tok(WIDTH, BF16), tok(WIDTH, BF16),
                 tok(AB_PAD, F32),
                 jax.ShapeDtypeStruct((B, T // chunk, 2 * HEADS, chunk), F32)]
    out_specs = [tok_spec(WIDTH), tok_spec(WIDTH), tok_spec(WIDTH), tok_spec(WIDTH),
                 tok_spec(WIDTH), tok_spec(WIDTH), tok_spec(WIDTH), tok_spec(WIDTH),
                 tok_spec(AB_PAD),
                 pl.BlockSpec((1, tm // chunk, 2 * HEADS, chunk), lambda b, j: (b, j, 0, 0))]
    if emit_tail:
        out_shape.append(jax.ShapeDtypeStruct((CONV_HIST, 3 * WIDTH), F32))
        out_specs.append(const((CONV_HIST, 3 * WIDTH)))
    return pl.pallas_call(
        functools.partial(_proj_kernel, tm=tm, chunk=chunk, emit_tail=emit_tail),
        grid=(B, nt),
        in_specs=[tok_spec(D_MODEL), const((1, D_MODEL)),
                  pl.BlockSpec((D_MODEL, IN_PAD), lambda b, j: (0, 0),
                               pipeline_mode=pl.Buffered(1)),
                  const((CONV_W, 3 * WIDTH)), const(lbl.shape), const((1, AB_PAD)),
                  const((1, AB_PAD)), const((CONV_HIST, 3 * WIDTH))],
        out_specs=out_specs,
        out_shape=out_shape,
        scratch_shapes=[pltpu.VMEM((tm + CONV_HIST, 3 * WIDTH), F32),
                        pltpu.VMEM((tm, WIDTH), F32), pltpu.VMEM((tm, AB_PAD), F32)],
        compiler_params=pltpu.CompilerParams(
            dimension_semantics=("parallel", "arbitrary"), vmem_limit_bytes=VMEM_LIMIT),
        name="proj_meta" if emit_tail else "proj",
    )(x, nw, w, cw, lbl, alog, dtb, tail)


def _run_interleaved(*gens):
    live = list(gens)
    while live:
        for g in list(live):
            try:
                next(g)
            except StopIteration:
                live.remove(g)


def _scaled(x, decay):
    return x * decay.astype(BF16)


def _spread(gen, gap):
    for _ in gen:
        yield
        for _ in range(gap):
            yield


def _hgrn2_intra_stages(q, k, v, G, out, *, bounded):
    n = len(q)
    C = q[0].shape[0]
    row = lax.broadcasted_iota(jnp.int32, (C, C), 0)
    col = lax.broadcasted_iota(jnp.int32, (C, C), 1)
    if bounded:
        g_mid = [G[i][C // 2 - 1:C // 2, :] for i in range(n)]
        p = [_dot_nt(_scaled(q[i], jnp.exp(G[i] - g_mid[i])),
                     _scaled(k[i], jnp.exp(g_mid[i] - G[i]))) for i in range(n)]
        yield
        out["o"] = [_dot(jnp.where(row >= col, p[i], 0.0).astype(BF16), v[i]) for i in range(n)]
        return
    nb = C // SUB
    qf = [q[i].astype(F32) for i in range(n)]
    kf = [k[i].astype(F32) for i in range(n)]
    a = [jnp.zeros((C, C), F32) for _ in range(n)]
    g_end = [[G[i][(j + 1) * SUB - 1:(j + 1) * SUB, :] for j in range(nb)] for i in range(n)]
    if nb > 1:
        p = []
        for i in range(n):
            g_ref = jnp.concatenate(
                [jnp.broadcast_to(g_end[i][j], (SUB, DH)) for j in range(nb)], axis=0)
            kp = (kf[i] * jnp.exp(g_ref - G[i])).astype(BF16)
            qp = jnp.concatenate(
                [qf[i] * jnp.exp(jnp.minimum(G[i] - g_end[i][j], 0.0)) for j in range(nb - 1)],
                axis=0).astype(BF16)
            p.append(_dot_nt(qp, kp))
        yield
        for i in range(n):
            for j in range(nb - 1):
                m = (col // SUB == j) & (row // SUB > j)
                a[i] = jnp.where(m, p[i][j * C:(j + 1) * C], a[i])
    rowb = lax.broadcasted_iota(jnp.int32, (SUB, C), 0)
    colb = lax.broadcasted_iota(jnp.int32, (SUB, C), 1)
    o = []
    for i in range(n):
        for b in range(nb):
            gb = G[i][b * SUB:(b + 1) * SUB]
            qb = qf[i][b * SUB:(b + 1) * SUB]
            kb = kf[i][b * SUB:(b + 1) * SUB]
            d = jnp.zeros((SUB, C), F32)
            for s in range(SUB):
                es = jnp.exp(jnp.minimum(gb - gb[s:s + 1], 0.0))
                cs = jnp.sum(qb * es * kb[s:s + 1], axis=-1, keepdims=True)
                d = jnp.where((colb == b * SUB + s) & (rowb >= s), cs, d)
            if nb == 1:
                a[i] = d
            else:
                dfull = jnp.concatenate(
                    [d if r == b else jnp.zeros((SUB, C), F32) for r in range(nb)], axis=0)
                a[i] = jnp.where((row // SUB == b) & (col // SUB == b), dfull, a[i])
        o.append(_dot(a[i].astype(BF16), v[i]))
    out["o"] = o


def _hgrn2_state_stages(q, k, v, G, st, o_intra, out):
    n = len(k)
    C = k[0].shape[0]
    g_last = [G[i][C - 1:C, :] for i in range(n)]
    kk = [_scaled(k[i], jnp.exp(g_last[i] - G[i])) for i in range(n)]
    st_new = [_dot_tn(v[i], kk[i]) for i in range(n)]
    if st is not None:
        st_new = [st_new[i] + st[i] * jnp.exp(g_last[i]) for i in range(n)]
    out["s"] = st_new
    yield
    if o_intra is not None:
        out["o"] = [o_intra[i] + _dot_nt(_scaled(q[i], jnp.exp(G[i])), st[i].astype(BF16))
                    for i in range(n)]


def _gdn_prep_stages(q, k, gam_c, gam_r, beta_c, out, *, need_o):
    n = len(k)
    C = k[0].shape[0]
    rng = range(n)
    row = lax.broadcasted_iota(jnp.int32, (C, C), 0)
    col = lax.broadcasted_iota(jnp.int32, (C, C), 1)
    if need_o:
        pk = [_dot_nt(jnp.concatenate([q[i], k[i]], axis=0), k[i]) for i in rng]
        qk, kk = [x[:C] for x in pk], [x[C:] for x in pk]
    else:
        kk = [_dot_nt(k[i], k[i]) for i in rng]
    yield
    dec = [jnp.exp(jnp.minimum(gam_c[i] - gam_r[i], 0.0)) for i in rng]
    if need_o:
        out["qkd"] = [jnp.where(row >= col, qk[i] * dec[i], 0.0).astype(BF16) for i in rng]
    a = [jnp.where(row > col, beta_c[i] * kk[i] * dec[i], 0.0) for i in rng]
    eye = jnp.where(row == col, 1.0, 0.0)
    t = [eye - jnp.where(row // 2 == col // 2, a[i], 0.0) for i in rng]
    bs = 2
    while bs < C:
        lm = (row // (2 * bs) == col // (2 * bs)) & (row // bs > col // bs)
        tb = [t[i].astype(BF16) for i in rng]
        tl = [_dot(tb[i], jnp.where(lm, a[i], 0.0).astype(BF16)) for i in rng]
        yield
        t = [t[i] - _dot(tl[i].astype(BF16), tb[i]) for i in rng]
        yield
        bs *= 2
    out["t"] = [t[i].astype(BF16) for i in rng]


def _gdn_state_stages(q, k, v, gam_c, beta_c, t, qkd, s, out):
    n = len(k)
    C = k[0].shape[0]
    rng = range(n)
    vf = [v[i].astype(F32) for i in rng]
    qs = None
    if s is not None:
        egam = [jnp.exp(gam_c[i]) for i in rng]
        sb = [s[i].astype(BF16) for i in rng]
        if qkd is not None:
            qks = [_dot(jnp.concatenate([_scaled(q[i], egam[i]), _scaled(k[i], egam[i])],
                                        axis=0), sb[i]) for i in rng]
            qs, ks = [x[:C] for x in qks], [x[C:] for x in qks]
        else:
            ks = [_dot(_scaled(k[i], egam[i]), sb[i]) for i in rng]
        yield
        rhs = [beta_c[i] * (vf[i] - ks[i]) for i in rng]
    else:
        rhs = [beta_c[i] * vf[i] for i in rng]
    ub = [_dot(t[i], rhs[i].astype(BF16)).astype(BF16) for i in rng]
    yield
    g_last = [gam_c[i][C - 1:C, :] for i in rng]
    s_new = [_dot_tn(_scaled(k[i], jnp.exp(g_last[i] - gam_c[i])), ub[i]) for i in rng]
    if s is not None:
        s_new = [s_new[i] + s[i] * jnp.exp(g_last[i]) for i in rng]
    out["s"] = s_new
    if qkd is not None:
        o = [_dot(qkd[i], ub[i]) for i in rng]
        out["o"] = o if qs is None else [o[i] + qs[i] for i in rng]


def _head_norm(o, w):
    return o * lax.rsqrt(jnp.mean(o * o, axis=-1, keepdims=True) + EPS) * w


def _mixer_kernel(hq_ref, hk_ref, hv_ref, hg_ref, gq_ref, gk_ref, gv_ref, gz_ref,
                  gcol_ref, grow_ref, sh0_ref, sg0_ref, hw_ref, gw_ref,
                  y_ref, sh_ref, sg_ref, t_buf, qkd_buf, oi_buf, bounded_ref, *, bb, tt, chunk):
    j = pl.program_id(1)
    nchunks = tt // chunk

    @pl.when(j == 0)
    def _():
        for b in range(bb):
            sh_ref[b] = sh0_ref[...]
            sg_ref[b] = sg0_ref[...]

    for c in range(nchunks):
        span = None
        for b in range(bb):
            g_mid = hg_ref[b, c * chunk + chunk // 2 - 1:c * chunk + chunk // 2, :]
            g_last = hg_ref[b, (c + 1) * chunk - 1:(c + 1) * chunk, :]
            s = jnp.maximum(-g_mid, g_mid - g_last)
            span = s if span is None else jnp.maximum(span, s)
        bounded_ref[c] = jnp.where(jnp.max(span) <= SAFE_RANGE, 1, 0)

    hw = hw_ref[...]
    gw = gw_ref[...]
    probs = [(b, h) for b in range(bb) for h in range(HEADS)]
    hs = lambda h: slice(h * DH, (h + 1) * DH)

    def chunk_rows(c):
        return pl.ds(pl.multiple_of(c * chunk, chunk), chunk)

    def prep_stages(c, pout, hout, bounded):
        rows = chunk_rows(c)
        ld = lambda ref: [ref[b, rows, hs(h)] for b, h in probs]
        gcol = [gcol_ref[b, rows, :] for b in range(bb)]
        grow = [grow_ref[b, c] for b in range(bb)]
        return (_gdn_prep_stages(ld(gq_ref), ld(gk_ref),
                                 [gcol[b][:, h:h + 1] for b, h in probs],
                                 [grow[b][h:h + 1, :] for b, h in probs],
                                 [gcol[b][:, HEADS + h:HEADS + h + 1] for b, h in probs],
                                 pout, need_o=True),
                _hgrn2_intra_stages(ld(hq_ref), ld(hk_ref), ld(hv_ref), ld(hg_ref), hout,
                                    bounded=bounded))

    def store_prep(c, pout, hout):
        for i in range(len(probs)):
            t_buf[c, i] = pout["t"][i]
            qkd_buf[c, i] = pout["qkd"][i]
            oi_buf[c, i] = hout["o"][i]

    def recur_stages(c, gout, hout):
        rows = chunk_rows(c)
        ld = lambda ref: [ref[b, rows, hs(h)] for b, h in probs]
        gcol = [gcol_ref[b, rows, :] for b in range(bb)]
        n = len(probs)
        return (_gdn_state_stages(ld(gq_ref), ld(gk_ref), ld(gv_ref),
                                  [gcol[b][:, h:h + 1] for b, h in probs],
                                  [gcol[b][:, HEADS + h:HEADS + h + 1] for b, h in probs],
                                  [t_buf[c, i] for i in range(n)],
                                  [qkd_buf[c, i] for i in range(n)],
                                  [sg_ref[b, h] for b, h in probs], gout),
                _hgrn2_state_stages(ld(hq_ref), ld(hk_ref), ld(hv_ref), ld(hg_ref),
                                    [sh_ref[b, h] for b, h in probs],
                                    [oi_buf[c, i] for i in range(n)], hout))

    def store_recur(c, gout, hout):
        rows = chunk_rows(c)
        for i, (b, h) in enumerate(probs):
            sh_ref[b, h] = hout["s"][i]
            sg_ref[b, h] = gout["s"][i]
            y_ref[b, rows, hs(h)] = _head_norm(hout["o"][i], hw).astype(BF16)
            ys = slice(WIDTH + h * DH, WIDTH + (h + 1) * DH)
            y_ref[b, rows, ys] = (_head_norm(gout["o"][i], gw)
                                  * gz_ref[b, rows, hs(h)].astype(F32)).astype(BF16)

    def prep_only(c, bounded):
        pout, hout = {}, {}
        _run_interleaved(*prep_stages(c, pout, hout, bounded))
        store_prep(c, pout, hout)

    def recur_only(c):
        gout, hout = {}, {}
        _run_interleaved(*recur_stages(c, gout, hout))
        store_recur(c, gout, hout)

    def pipelined_step():
        prep_only(0, True)

        def body(c, carry):
            pout, phout, gout, hout = {}, {}, {}, {}
            gprep, hprep = prep_stages(c + 1, pout, phout, True)
            gstate, hstate = recur_stages(c, gout, hout)
            _run_interleaved(gprep, _spread(gstate, RECUR_GAP), _spread(hstate, RECUR_GAP),
                             hprep)
            store_prep(c + 1, pout, phout)
            store_recur(c, gout, hout)
            return carry

        lax.fori_loop(0, nchunks - 1, body, 0)
        recur_only(nchunks - 1)

    def plain_step():
        def body(c, carry):
            prep_only(c, False)
            recur_only(c)
            return carry

        lax.fori_loop(0, nchunks, body, 0)

    all_bounded = bounded_ref[0] == 1
    for c in range(1, nchunks):
        all_bounded = jnp.logical_and(all_bounded, bounded_ref[c] == 1)
    pl.when(all_bounded)(pipelined_step)
    pl.when(jnp.logical_not(all_bounded))(plain_step)


def _mixer_call(hq, hk, hv, hg, gq, gk, gv, gz, gcol, grow, sh0, sg0, hw, gw, *, bb, tt, chunk):
    B, T, _ = hq.shape
    nchunks = tt // chunk
    tok_spec = lambda width: pl.BlockSpec((bb, tt, width), lambda i, j: (i, j, 0))
    const = lambda shape: pl.BlockSpec(shape, lambda i, j: (0,) * len(shape))
    return pl.pallas_call(
        functools.partial(_mixer_kernel, bb=bb, tt=tt, chunk=chunk),
        grid=(B // bb, T // tt),
        in_specs=[tok_spec(WIDTH), tok_spec(WIDTH), tok_spec(WIDTH), tok_spec(WIDTH),
                  tok_spec(WIDTH), tok_spec(WIDTH), tok_spec(WIDTH), tok_spec(WIDTH),
                  tok_spec(AB_PAD),
                  pl.BlockSpec((bb, nchunks, 2 * HEADS, chunk), lambda i, j: (i, j, 0, 0)),
                  const((HEADS, DH, DH)), const((HEADS, DH, DH)),
                  const((1, DH)), const((1, DH))],
        out_specs=tok_spec(2 * WIDTH),
        out_shape=jax.ShapeDtypeStruct((B, T, 2 * WIDTH), BF16),
        scratch_shapes=[pltpu.VMEM((bb, HEADS, DH, DH), F32),
                        pltpu.VMEM((bb, HEADS, DH, DH), F32),
                        pltpu.VMEM((nchunks, bb * HEADS, chunk, chunk), BF16),
                        pltpu.VMEM((nchunks, bb * HEADS, chunk, chunk), BF16),
                        pltpu.VMEM((nchunks, bb * HEADS, chunk, DH), F32),
                        pltpu.SMEM((nchunks,), jnp.int32)],
        compiler_params=pltpu.CompilerParams(
            dimension_semantics=("parallel", "arbitrary"), vmem_limit_bytes=VMEM_LIMIT),
        name="mixer",
    )(hq, hk, hv, hg, gq, gk, gv, gz, gcol, grow, sh0, sg0, hw, gw)


def _meta_state_kernel(hk_ref, hv_ref, hg_ref, gk_ref, gv_ref, gcol_ref, grow_ref,
                       sh_ref, sg_ref):
    gcol = gcol_ref[0]
    grow = grow_ref[0, 0]
    hs = lambda h: slice(h * DH, (h + 1) * DH)
    heads = range(HEADS)
    hk = [hk_ref[0, :, hs(h)] for h in heads]
    gk = [gk_ref[0, :, hs(h)] for h in heads]
    gam_c = [gcol[:, h:h + 1] for h in heads]
    beta_c = [gcol[:, HEADS + h:HEADS + h + 1] for h in heads]
    prep, hout, gout = {}, {}, {}
    _run_interleaved(
        _gdn_prep_stages(None, gk, gam_c, [grow[h:h + 1, :] for h in heads], beta_c, prep,
                         need_o=False),
        _hgrn2_state_stages(None, hk, [hv_ref[0, :, hs(h)] for h in heads],
                            [hg_ref[0, :, hs(h)] for h in heads], None, None, hout))
    _run_interleaved(
        _gdn_state_stages(None, gk, [gv_ref[0, :, hs(h)] for h in heads], gam_c, beta_c,
                          prep["t"], None, None, gout))
    for h in heads:
        sh_ref[h] = hout["s"][h]
        sg_ref[h] = gout["s"][h]


def _meta_state_call(hk, hv, hg, gk, gv, gcol, grow):
    state = jax.ShapeDtypeStruct((HEADS, DH, DH), F32)
    return pl.pallas_call(
        _meta_state_kernel,
        out_shape=[state, state],
        name="meta_state",
    )(hk, hv, hg, gk, gv, gcol, grow)


def _out_kernel(y_ref, x_ref, nw_ref, wz_ref, w_ref, fw_ref, o_ref):
    tm = x_ref.shape[0]
    rb = min(tm, OUT_ROW_BLOCK)
    blocks = [slice(r0, r0 + rb) for r0 in range(0, tm, rb)]
    gates = []
    for rows in blocks:
        x = x_ref[rows, :]
        ms = jnp.mean(x * x, axis=-1, keepdims=True)
        u = (x * lax.rsqrt(ms + EPS) * nw_ref[...]).astype(BF16)
        gates.append(_silu(_dot(u, wz_ref[...])))
    for rows, gate in zip(blocks, gates):
        yh = (y_ref[rows, 0:WIDTH].astype(F32) * gate).astype(BF16)
        h = (x_ref[rows, :] + _dot(yh, w_ref[0:WIDTH, :])
             + _dot(y_ref[rows, WIDTH:2 * WIDTH], w_ref[WIDTH:2 * WIDTH, :]))
        ms = jnp.mean(h * h, axis=-1, keepdims=True)
        o_ref[rows, :] = h * lax.rsqrt(ms + EPS) * fw_ref[...]


def _out_call(y, x, nw, wz, w, fw, *, tm):
    R = x.shape[0]
    row_spec = pl.BlockSpec((tm, D_MODEL), lambda i: (i, 0))
    const = lambda shape: pl.BlockSpec(shape, lambda i: (0, 0))
    return pl.pallas_call(
        _out_kernel,
        grid=(R // tm,),
        in_specs=[row_spec, row_spec, const((1, D_MODEL)), const((D_MODEL, WIDTH)),
                  const((2 * WIDTH, D_MODEL)), const((1, D_MODEL))],
        out_specs=row_spec,
        out_shape=jax.ShapeDtypeStruct((R, D_MODEL), F32),
        compiler_params=pltpu.CompilerParams(
            dimension_semantics=("parallel",), vmem_limit_bytes=VMEM_LIMIT),
        name="out",
    )(y, x, nw, wz, w, fw)


def kernel(x, meta_tokens, norm_w, w_in, conv_w, hg_lb_logits, hg_norm_w, gdn_A_log,
           gdn_dt_bias, gdn_norm_w, w_out, final_norm_w):
    B, T, D = x.shape
    assert D == D_MODEL and norm_w.shape[0] == 1 and meta_tokens.shape == (N_META, D_MODEL)
    assert w_in.shape == (1, D_MODEL, 8 * WIDTH + 2 * HEADS)

    w = jnp.pad(w_in[0], ((0, 0), (0, AB_PAD - 2 * HEADS))).astype(BF16)
    pad = lambda v: jnp.zeros((1, AB_PAD), F32).at[0, :HEADS].set(v.astype(F32))
    alog, dtb = pad(gdn_A_log[0]), pad(gdn_dt_bias[0])
    nw = norm_w[0].reshape(1, D_MODEL)
    cw = conv_w[0]
    proj = functools.partial(_proj_call, nw=nw, w=w, cw=cw, lbl=hg_lb_logits, alog=alog,
                             dtb=dtb)

    zero_tail = jnp.zeros((CONV_HIST, 3 * WIDTH), F32)
    m = proj(meta_tokens[None], tail=zero_tail, tm=N_META, chunk=N_META, emit_tail=True)
    _, mhk, mhv, mhg, _, mgk, mgv, _, mgcol, mgrow, tail = m
    sh0, sg0 = _meta_state_call(mhk, mhv, mhg, mgk, mgv, mgcol, mgrow)

    hq, hk, hv, hg, gq, gk, gv, gz, gcol, grow = proj(
        x, tail=tail, tm=PROJ_ROWS, chunk=CHUNK, emit_tail=False)
    y = _mixer_call(hq, hk, hv, hg, gq, gk, gv, gz, gcol, grow, sh0, sg0,
                    hg_norm_w[0].reshape(1, DH), gdn_norm_w[0].reshape(1, DH),
                    bb=MIXER_BATCH, tt=MIXER_ROWS, chunk=CHUNK)
    wz = w_in[0, :, 3 * WIDTH:4 * WIDTH].astype(BF16)
    out = _out_call(y.reshape(B * T, 2 * WIDTH), x.reshape(B * T, D_MODEL), nw, wz,
                    w_out[0].astype(BF16), final_norm_w.reshape(1, D_MODEL), tm=OUT_ROWS)
    return out.reshape(B, T, D_MODEL)
```

```python
import functools

import jax
import jax.numpy as jnp
from jax import lax
from jax.experimental import pallas as pl
from jax.experimental.pallas import tpu as pltpu

F32 = jnp.float32
BF16 = jnp.bfloat16

D_MODEL = 1024
N_META = 16
HEADS = 4
DH = 128
WIDTH = HEADS * DH
CONV_W = 4
EPS = 1e-6
AB_PAD = 128
IN_PAD = 8 * WIDTH + AB_PAD

CHUNK = 64
SUB = 16
SAFE_RANGE = 60.0
CONV_HIST = 8
CUM_ROWS = 256
ROW_BLOCK = 128
PROJ_ROWS = 1024
MIXER_BATCH = 8
MIXER_ROWS = 128
OUT_ROWS = 1024
OUT_ROW_BLOCK = 256
RECUR_GAP = 3
V7X_VMEM_BYTES = 64 * 1024 * 1024
VMEM_LIMIT = V7X_VMEM_BYTES * 7 // 8


def _dot(a, b):
    return jnp.dot(a, b, preferred_element_type=F32)


def _dot_nt(a, b):
    return lax.dot_general(a, b, (((1,), (1,)), ((), ())), preferred_element_type=F32)


def _dot_tn(a, b):
    return lax.dot_general(a, b, (((0,), (0,)), ((), ())), preferred_element_type=F32)


def _split3(x):
    x1 = x.astype(BF16)
    r = x - x1.astype(F32)
    x2 = r.astype(BF16)
    x3 = (r - x2.astype(F32)).astype(BF16)
    return x1, x2, x3


def _silu(x):
    return x * jax.nn.sigmoid(x)


def _softplus(x):
    return jnp.maximum(x, 0.0) + jnp.log1p(jnp.exp(-jnp.abs(x)))


def _chunk_cumsum(x, chunk):
    rows = x.shape[0]
    blk = min(rows, CUM_ROWS)
    r = lax.broadcasted_iota(jnp.int32, (blk, blk), 0)
    c = lax.broadcasted_iota(jnp.int32, (blk, blk), 1)
    tri = jnp.where((r // chunk == c // chunk) & (c <= r), 1.0, 0.0).astype(BF16)
    outs = []
    for i in range(rows // blk):
        parts = _split3(x[i * blk:(i + 1) * blk])
        outs.append(_dot(tri, parts[0]) + _dot(tri, parts[1]) + _dot(tri, parts[2]))
    return outs[0] if len(outs) == 1 else jnp.concatenate(outs, axis=0)


def _chunk_cumsum_lanes(x, chunk):
    n = x.shape[1]
    r = lax.broadcasted_iota(jnp.int32, (n, n), 0)
    c = lax.broadcasted_iota(jnp.int32, (n, n), 1)
    tri = jnp.where((r // chunk == c // chunk) & (r <= c), 1.0, 0.0).astype(BF16)
    parts = _split3(x)
    return _dot(parts[0], tri) + _dot(parts[1], tri) + _dot(parts[2], tri)


def _proj_kernel(x_ref, nw_ref, w_ref, cw_ref, lbl_ref, alog_ref, dtb_ref, tail_ref,
                 hq_ref, hk_ref, hv_ref, hg_ref, gq_ref, gk_ref, gv_ref, gz_ref,
                 gcol_ref, grow_ref, *rest, tm, chunk, emit_tail):
    if emit_tail:
        tail_out_ref, cbuf, logf_buf, ab_buf = rest
    else:
        cbuf, logf_buf, ab_buf = rest
    j = pl.program_id(1)

    @pl.when(j == 0)
    def _():
        cbuf[0:CONV_HIST, :] = tail_ref[...]

    lbl = lbl_ref[...]
    e = jnp.exp(lbl - jnp.max(lbl, axis=0, keepdims=True))
    lb = e[0:1] / jnp.sum(e, axis=0, keepdims=True)

    rb = min(tm, ROW_BLOCK)
    for r0 in range(0, tm, rb):
        rows = slice(r0, r0 + rb)
        x = x_ref[0, rows, :]
        ms = jnp.mean(x * x, axis=-1, keepdims=True)
        u = (x * lax.rsqrt(ms + EPS) * nw_ref[...]).astype(BF16)

        def proj(a, b, u=u):
            return _dot(u, w_ref[:, a:b])

        cbuf[CONV_HIST + r0:CONV_HIST + r0 + rb, :] = proj(4 * WIDTH, 7 * WIDTH)
        xe = cbuf[r0:r0 + CONV_HIST + rb, :]
        acc = cw_ref[0:1, :] * xe
        for t in range(1, CONV_W):
            acc = pltpu.roll(acc, 1, 0) + cw_ref[t:t + 1, :] * xe
        qkv = _silu(acc[CONV_HIST:CONV_HIST + rb])
        for h in range(HEADS):
            q = qkv[:, h * DH:(h + 1) * DH]
            k = qkv[:, WIDTH + h * DH:WIDTH + (h + 1) * DH]
            q = q * lax.rsqrt(jnp.sum(q * q, axis=-1, keepdims=True) + EPS) * (DH ** -0.5)
            k = k * lax.rsqrt(jnp.sum(k * k, axis=-1, keepdims=True) + EPS)
            gq_ref[0, rows, h * DH:(h + 1) * DH] = q.astype(BF16)
            gk_ref[0, rows, h * DH:(h + 1) * DH] = k.astype(BF16)
        gv_ref[0, rows, :] = qkv[:, 2 * WIDTH:3 * WIDTH].astype(BF16)
        gz_ref[0, rows, :] = _silu(proj(7 * WIDTH, 8 * WIDTH)).astype(BF16)

        hq_ref[0, rows, :] = _silu(proj(0, WIDTH)).astype(BF16)
        f = lb + (1.0 - lb) * jax.nn.sigmoid(proj(WIDTH, 2 * WIDTH))
        hk_ref[0, rows, :] = (1.0 - f).astype(BF16)
        logf_buf[rows, :] = jnp.log(f)
        hv_ref[0, rows, :] = proj(2 * WIDTH, 3 * WIDTH).astype(BF16)

        ab_buf[rows, :] = proj(8 * WIDTH, 8 * WIDTH + AB_PAD)

    for r0 in range(0, tm, rb):
        rows = slice(r0, r0 + rb)
        hg_ref[0, rows, :] = _chunk_cumsum(logf_buf[rows, :], chunk)
        p = ab_buf[rows, :]
        lane = lax.broadcasted_iota(jnp.int32, p.shape, 1)
        g = jnp.where(lane < HEADS, -jnp.exp(alog_ref[...]) * _softplus(p + dtb_ref[...]), 0.0)
        gam_rows = _chunk_cumsum_lanes(g.T[0:2 * HEADS, :], chunk)
        for c in range(rb // chunk):
            grow_ref[0, r0 // chunk + c] = gam_rows[:, c * chunk:(c + 1) * chunk]
        gam = jnp.concatenate(
            [gam_rows, jnp.zeros((AB_PAD - 2 * HEADS, rb), F32)], axis=0).T
        gcol_ref[0, rows, :] = jnp.where(
            lane < HEADS, gam, jnp.where(lane < 2 * HEADS, jax.nn.sigmoid(p), 0.0))

    hist = cbuf[tm:tm + CONV_HIST, :]
    cbuf[0:CONV_HIST, :] = hist
    if emit_tail:
        tail_out_ref[...] = hist


def _proj_call(x, nw, w, cw, lbl, alog, dtb, tail, *, tm, chunk, emit_tail):
    B, T, _ = x.shape
    assert T % tm == 0 and tm % min(tm, ROW_BLOCK) == 0 and min(tm, ROW_BLOCK) % chunk == 0
    nt = T // tm
    tok = lambda width, dt: jax.ShapeDtypeStruct((B, T, width), dt)
    tok_spec = lambda width: pl.BlockSpec((1, tm, width), lambda b, j: (b, j, 0))
    const = lambda shape: pl.BlockSpec(shape, lambda b, j: (0,) * len(shape))
    out_shape = [tok(WIDTH, BF16), tok(WIDTH, BF16), tok(WIDTH, BF16), tok(WIDTH, F32),
                 tok(WIDTH, BF16), tok(WIDTH, BF16), tok(WIDTH, BF16), tok(WIDTH, BF16),
                 tok(AB_PAD, F32),
                 jax.ShapeDtypeStruct((B, T // chunk, 2 * HEADS, chunk), F32)]
    out_specs = [tok_spec(WIDTH), tok_spec(WIDTH), tok_spec(WIDTH), tok_spec(WIDTH),
                 tok_spec(WIDTH), tok_spec(WIDTH), tok_spec(WIDTH), tok_spec(WIDTH),
                 tok_spec(AB_PAD),
                 pl.BlockSpec((1, tm // chunk, 2 * HEADS, chunk), lambda b, j: (b, j, 0, 0))]
    if emit_tail:
        out_shape.append(jax.ShapeDtypeStruct((CONV_HIST, 3 * WIDTH), F32))
        out_specs.append(const((CONV_HIST, 3 * WIDTH)))
    return pl.pallas_call(
        functools.partial(_proj_kernel, tm=tm, chunk=chunk, emit_tail=emit_tail),
        grid=(B, nt),
        in_specs=[tok_spec(D_MODEL), const((1, D_MODEL)),
                  pl.BlockSpec((D_MODEL, IN_PAD), lambda b, j: (0, 0),
                               pipeline_mode=pl.Buffered(1)),
                  const((CONV_W, 3 * WIDTH)), const(lbl.shape), const((1, AB_PAD)),
                  const((1, AB_PAD)), const((CONV_HIST, 3 * WIDTH))],
        out_specs=out_specs,
        out_shape=out_shape,
        scratch_shapes=[pltpu.VMEM((tm + CONV_HIST, 3 * WIDTH), F32),
                        pltpu.VMEM((tm, WIDTH), F32), pltpu.VMEM((tm, AB_PAD), F32)],
        compiler_params=pltpu.CompilerParams(
            dimension_semantics=("parallel", "arbitrary"), vmem_limit_bytes=VMEM_LIMIT),
        name="proj_meta" if emit_tail else "proj",
    )(x, nw, w, cw, lbl, alog, dtb, tail)


def _run_interleaved(*gens):
    live = list(gens)
    while live:
        for g in list(live):
            try:
                next(g)
            except StopIteration:
                live.remove(g)


def _scaled(x, decay):
    return x * decay.astype(BF16)


def _spread(gen, gap):
    for _ in gen:
        yield
        for _ in range(gap):
            yield


def _hgrn2_intra_stages(q, k, v, G, out, *, bounded):
    n = len(q)
    C = q[0].shape[0]
    row = lax.broadcasted_iota(jnp.int32, (C, C), 0)
    col = lax.broadcasted_iota(jnp.int32, (C, C), 1)
    if bounded:
        g_mid = [G[i][C // 2 - 1:C // 2, :] for i in range(n)]
        p = [_dot_nt(_scaled(q[i], jnp.exp(G[i] - g_mid[i])),
                     _scaled(k[i], jnp.exp(g_mid[i] - G[i]))) for i in range(n)]
        yield
        out["o"] = [_dot(jnp.where(row >= col, p[i], 0.0).astype(BF16), v[i]) for i in range(n)]
        return
    nb = C // SUB
    qf = [q[i].astype(F32) for i in range(n)]
    kf = [k[i].astype(F32) for i in range(n)]
    a = [jnp.zeros((C, C), F32) for _ in range(n)]
    g_end = [[G[i][(j + 1) * SUB - 1:(j + 1) * SUB, :] for j in range(nb)] for i in range(n)]
    if nb > 1:
        p = []
        for i in range(n):
            g_ref = jnp.concatenate(
                [jnp.broadcast_to(g_end[i][j], (SUB, DH)) for j in range(nb)], axis=0)
            kp = (kf[i] * jnp.exp(g_ref - G[i])).astype(BF16)
            qp = jnp.concatenate(
                [qf[i] * jnp.exp(jnp.minimum(G[i] - g_end[i][j], 0.0)) for j in range(nb - 1)],
                axis=0).astype(BF16)
            p.append(_dot_nt(qp, kp))
        yield
        for i in range(n):
            for j in range(nb - 1):
                m = (col // SUB == j) & (row // SUB > j)
                a[i] = jnp.where(m, p[i][j * C:(j + 1) * C], a[i])
    rowb = lax.broadcasted_iota(jnp.int32, (SUB, C), 0)
    colb = lax.broadcasted_iota(jnp.int32, (SUB, C), 1)
    o = []
    for i in range(n):
        for b in range(nb):
            gb = G[i][b * SUB:(b + 1) * SUB]
            qb = qf[i][b * SUB:(b + 1) * SUB]
            kb = kf[i][b * SUB:(b + 1) * SUB]
            d = jnp.zeros((SUB, C), F32)
            for s in range(SUB):
                es = jnp.exp(jnp.minimum(gb - gb[s:s + 1], 0.0))
                cs = jnp.sum(qb * es * kb[s:s + 1], axis=-1, keepdims=True)
                d = jnp.where((colb == b * SUB + s) & (rowb >= s), cs, d)
            if nb == 1:
                a[i] = d
            else:
                dfull = jnp.concatenate(
                    [d if r == b else jnp.zeros((SUB, C), F32) for r in range(nb)], axis=0)
                a[i] = jnp.where((row // SUB == b) & (col // SUB == b), dfull, a[i])
        o.append(_dot(a[i].astype(BF16), v[i]))
    out["o"] = o


def _hgrn2_state_stages(q, k, v, G, st, o_intra, out):
    n = len(k)
    C = k[0].shape[0]
    g_last = [G[i][C - 1:C, :] for i in range(n)]
    kk = [_scaled(k[i], jnp.exp(g_last[i] - G[i])) for i in range(n)]
    st_new = [_dot_tn(v[i], kk[i]) for i in range(n)]
    if st is not None:
        st_new = [st_new[i] + st[i] * jnp.exp(g_last[i]) for i in range(n)]
    out["s"] = st_new
    yield
    if o_intra is not None:
        out["o"] = [o_intra[i] + _dot_nt(_scaled(q[i], jnp.exp(G[i])), st[i].astype(BF16))
                    for i in range(n)]


def _gdn_prep_stages(q, k, gam_c, gam_r, beta_c, out, *, need_o):
    n = len(k)
    C = k[0].shape[0]
    rng = range(n)
    row = lax.broadcasted_iota(jnp.int32, (C, C), 0)
    col = lax.broadcasted_iota(jnp.int32, (C, C), 1)
    if need_o:
        pk = [_dot_nt(jnp.concatenate([q[i], k[i]], axis=0), k[i]) for i in rng]
        qk, kk = [x[:C] for x in pk], [x[C:] for x in pk]
    else:
        kk = [_dot_nt(k[i], k[i]) for i in rng]
    yield
    dec = [jnp.exp(jnp.minimum(gam_c[i] - gam_r[i], 0.0)) for i in rng]
    if need_o:
        out["qkd"] = [jnp.where(row >= col, qk[i] * dec[i], 0.0).astype(BF16) for i in rng]
    a = [jnp.where(row > col, beta_c[i] * kk[i] * dec[i], 0.0) for i in rng]
    eye = jnp.where(row == col, 1.0, 0.0)
    t = [eye - jnp.where(row // 2 == col // 2, a[i], 0.0) for i in rng]
    bs = 2
    while bs < C:
        lm = (row // (2 * bs) == col // (2 * bs)) & (row // bs > col // bs)
        tb = [t[i].astype(BF16) for i in rng]
        tl = [_dot(tb[i], jnp.where(lm, a[i], 0.0).astype(BF16)) for i in rng]
        yield
        t = [t[i] - _dot(tl[i].astype(BF16), tb[i]) for i in rng]
        yield
        bs *= 2
    out["t"] = [t[i].astype(BF16) for i in rng]


def _gdn_state_stages(q, k, v, gam_c, beta_c, t, qkd, s, out):
    n = len(k)
    C = k[0].shape[0]
    rng = range(n)
    vf = [v[i].astype(F32) for i in rng]
    qs = None
    if s is not None:
        egam = [jnp.exp(gam_c[i]) for i in rng]
        sb = [s[i].astype(BF16) for i in rng]
        if qkd is not None:
            qks = [_dot(jnp.concatenate([_scaled(q[i], egam[i]), _scaled(k[i], egam[i])],
                                        axis=0), sb[i]) for i in rng]
            qs, ks = [x[:C] for x in qks], [x[C:] for x in qks]
        else:
            ks = [_dot(_scaled(k[i], egam[i]), sb[i]) for i in rng]
        yield
        rhs = [beta_c[i] * (vf[i] - ks[i]) for i in rng]
    else:
        rhs = [beta_c[i] * vf[i] for i in rng]
    ub = [_dot(t[i], rhs[i].astype(BF16)).astype(BF16) for i in rng]
    yield
    g_last = [gam_c[i][C - 1:C, :] for i in rng]
    s_new = [_dot_tn(_scaled(k[i], jnp.exp(g_last[i] - gam_c[i])), ub[i]) for i in rng]
    if s is not None:
        s_new = [s_new[i] + s[i] * jnp.exp(g_last[i]) for i in rng]
    out["s"] = s_new
    if qkd is not None:
        o = [_dot(qkd[i], ub[i]) for i in rng]
        out["o"] = o if qs is None else [o[i] + qs[i] for i in rng]


def _head_norm(o, w):
    return o * lax.rsqrt(jnp.mean(o * o, axis=-1, keepdims=True) + EPS) * w


def _mixer_kernel(hq_ref, hk_ref, hv_ref, hg_ref, gq_ref, gk_ref, gv_ref, gz_ref,
                  gcol_ref, grow_ref, sh0_ref, sg0_ref, hw_ref, gw_ref,
                  y_ref, sh_ref, sg_ref, t_buf, qkd_buf, oi_buf, bounded_ref, *, bb, tt, chunk):
    j = pl.program_id(1)
    nchunks = tt // chunk

    @pl.when(j == 0)
    def _():
        for b in range(bb):
            sh_ref[b] = sh0_ref[...]
            sg_ref[b] = sg0_ref[...]

    for c in range(nchunks):
        span = None
        for b in range(bb):
            g_mid = hg_ref[b, c * chunk + chunk // 2 - 1:c * chunk + chunk // 2, :]
            g_last = hg_ref[b, (c + 1) * chunk - 1:(c + 1) * chunk, :]
            s = jnp.maximum(-g_mid, g_mid - g_last)
            span = s if span is None else jnp.maximum(span, s)
        bounded_ref[c] = jnp.where(jnp.max(span) <= SAFE_RANGE, 1, 0)

    hw = hw_ref[...]
    gw = gw_ref[...]
    probs = [(b, h) for b in range(bb) for h in range(HEADS)]
    hs = lambda h: slice(h * DH, (h + 1) * DH)

    def chunk_rows(c):
        return pl.ds(pl.multiple_of(c * chunk, chunk), chunk)

    def prep_stages(c, pout, hout, bounded):
        rows = chunk_rows(c)
        ld = lambda ref: [ref[b, rows, hs(h)] for b, h in probs]
        gcol = [gcol_ref[b, rows, :] for b in range(bb)]
        grow = [grow_ref[b, c] for b in range(bb)]
        return (_gdn_prep_stages(ld(gq_ref), ld(gk_ref),
                                 [gcol[b][:, h:h + 1] for b, h in probs],
                                 [grow[b][h:h + 1, :] for b, h in probs],
                                 [gcol[b][:, HEADS + h:HEADS + h + 1] for b, h in probs],
                                 pout, need_o=True),
                _hgrn2_intra_stages(ld(hq_ref), ld(hk_ref), ld(hv_ref), ld(hg_ref), hout,
                                    bounded=bounded))

    def store_prep(c, pout, hout):
        for i in range(len(probs)):
            t_buf[c, i] = pout["t"][i]
            qkd_buf[c, i] = pout["qkd"][i]
            oi_buf[c, i] = hout["o"][i]

    def recur_stages(c, gout, hout):
        rows = chunk_rows(c)
        ld = lambda ref: [ref[b, rows, hs(h)] for b, h in probs]
        gcol = [gcol_ref[b, rows, :] for b in range(bb)]
        n = len(probs)
        return (_gdn_state_stages(ld(gq_ref), ld(gk_ref), ld(gv_ref),
                                  [gcol[b][:, h:h + 1] for b, h in probs],
                                  [gcol[b][:, HEADS + h:HEADS + h + 1] for b, h in probs],
                                  [t_buf[c, i] for i in range(n)],
                                  [qkd_buf[c, i] for i in range(n)],
                                  [sg_ref[b, h] for b, h in probs], gout),
                _hgrn2_state_stages(ld(hq_ref), ld(hk_ref), ld(hv_ref), ld(hg_ref),
                                    [sh_ref[b, h] for b, h in probs],
                                    [oi_buf[c, i] for i in range(n)], hout))

    def store_recur(c, gout, hout):
        rows = chunk_rows(c)
        for i, (b, h) in enumerate(probs):
            sh_ref[b, h] = hout["s"][i]
            sg_ref[b, h] = gout["s"][i]
            y_ref[b, rows, hs(h)] = _head_norm(hout["o"][i], hw).astype(BF16)
            ys = slice(WIDTH + h * DH, WIDTH + (h + 1) * DH)
            y_ref[b, rows, ys] = (_head_norm(gout["o"][i], gw)
                                  * gz_ref[b, rows, hs(h)].astype(F32)).astype(BF16)

    def prep_only(c, bounded):
        pout, hout = {}, {}
        _run_interleaved(*prep_stages(c, pout, hout, bounded))
        store_prep(c, pout, hout)

    def recur_only(c):
        gout, hout = {}, {}
        _run_interleaved(*recur_stages(c, gout, hout))
        store_recur(c, gout, hout)

    def pipelined_step():
        prep_only(0, True)

        def body(c, carry):
            pout, phout, gout, hout = {}, {}, {}, {}
            gprep, hprep = prep_stages(c + 1, pout, phout, True)
            gstate, hstate = recur_stages(c, gout, hout)
            _run_interleaved(gprep, _spread(gstate, RECUR_GAP), _spread(hstate, RECUR_GAP),
                             hprep)
            store_prep(c + 1, pout, phout)
            store_recur(c, gout, hout)
            return carry

        lax.fori_loop(0, nchunks - 1, body, 0)
        recur_only(nchunks - 1)

    def plain_step():
        def body(c, carry):
            prep_only(c, False)
            recur_only(c)
            return carry

        lax.fori_loop(0, nchunks, body, 0)

    all_bounded = bounded_ref[0] == 1
    for c in range(1, nchunks):
        all_bounded = jnp.logical_and(all_bounded, bounded_ref[c] == 1)
    pl.when(all_bounded)(pipelined_step)
    pl.when(jnp.logical_not(all_bounded))(plain_step)


def _mixer_call(hq, hk, hv, hg, gq, gk, gv, gz, gcol, grow, sh0, sg0, hw, gw, *, bb, tt, chunk):
    B, T, _ = hq.shape
    assert B % bb == 0 and T % tt == 0 and tt % chunk == 0
    nchunks = tt // chunk
    tok_spec = lambda width: pl.BlockSpec((bb, tt, width), lambda i, j: (i, j, 0))
    const = lambda shape: pl.BlockSpec(shape, lambda i, j: (0,) * len(shape))
    return pl.pallas_call(
        functools.partial(_mixer_kernel, bb=bb, tt=tt, chunk=chunk),
        grid=(B // bb, T // tt),
        in_specs=[tok_spec(WIDTH), tok_spec(WIDTH), tok_spec(WIDTH), tok_spec(WIDTH),
                  tok_spec(WIDTH), tok_spec(WIDTH), tok_spec(WIDTH), tok_spec(WIDTH),
                  tok_spec(AB_PAD),
                  pl.BlockSpec((bb, nchunks, 2 * HEADS, chunk), lambda i, j: (i, j, 0, 0)),
                  const((HEADS, DH, DH)), const((HEADS, DH, DH)),
                  const((1, DH)), const((1, DH))],
        out_specs=tok_spec(2 * WIDTH),
        out_shape=jax.ShapeDtypeStruct((B, T, 2 * WIDTH), BF16),
        scratch_shapes=[pltpu.VMEM((bb, HEADS, DH, DH), F32),
                        pltpu.VMEM((bb, HEADS, DH, DH), F32),
                        pltpu.VMEM((nchunks, bb * HEADS, chunk, chunk), BF16),
                        pltpu.VMEM((nchunks, bb * HEADS, chunk, chunk), BF16),
                        pltpu.VMEM((nchunks, bb * HEADS, chunk, DH), F32),
                        pltpu.SMEM((nchunks,), jnp.int32)],
        compiler_params=pltpu.CompilerParams(
            dimension_semantics=("parallel", "arbitrary"), vmem_limit_bytes=VMEM_LIMIT),
        name="mixer",
    )(hq, hk, hv, hg, gq, gk, gv, gz, gcol, grow, sh0, sg0, hw, gw)


def _meta_state_kernel(hk_ref, hv_ref, hg_ref, gk_ref, gv_ref, gcol_ref, grow_ref,
                       sh_ref, sg_ref):
    gcol = gcol_ref[0]
    grow = grow_ref[0, 0]
    hs = lambda h: slice(h * DH, (h + 1) * DH)
    heads = range(HEADS)
    hk = [hk_ref[0, :, hs(h)] for h in heads]
    gk = [gk_ref[0, :, hs(h)] for h in heads]
    gam_c = [gcol[:, h:h + 1] for h in heads]
    beta_c = [gcol[:, HEADS + h:HEADS + h + 1] for h in heads]
    prep, hout, gout = {}, {}, {}
    _run_interleaved(
        _gdn_prep_stages(None, gk, gam_c, [grow[h:h + 1, :] for h in heads], beta_c, prep,
                         need_o=False),
        _hgrn2_state_stages(None, hk, [hv_ref[0, :, hs(h)] for h in heads],
                            [hg_ref[0, :, hs(h)] for h in heads], None, None, hout))
    _run_interleaved(
        _gdn_state_stages(None, gk, [gv_ref[0, :, hs(h)] for h in heads], gam_c, beta_c,
                          prep["t"], None, None, gout))
    for h in heads:
        sh_ref[h] = hout["s"][h]
        sg_ref[h] = gout["s"][h]


def _meta_state_call(hk, hv, hg, gk, gv, gcol, grow):
    state = jax.ShapeDtypeStruct((HEADS, DH, DH), F32)
    return pl.pallas_call(
        _meta_state_kernel,
        out_shape=[state, state],
        name="meta_state",
    )(hk, hv, hg, gk, gv, gcol, grow)


def _out_kernel(y_ref, x_ref, nw_ref, wz_ref, w_ref, fw_ref, o_ref):
    tm = x_ref.shape[0]
    rb = min(tm, OUT_ROW_BLOCK)
    blocks = [slice(r0, r0 + rb) for r0 in range(0, tm, rb)]
    gates = []
    for rows in blocks:
        x = x_ref[rows, :]
        ms = jnp.mean(x * x, axis=-1, keepdims=True)
        u = (x * lax.rsqrt(ms + EPS) * nw_ref[...]).astype(BF16)
        gates.append(_silu(_dot(u, wz_ref[...])))
    for rows, gate in zip(blocks, gates):
        yh = (y_ref[rows, 0:WIDTH].astype(F32) * gate).astype(BF16)
        h = (x_ref[rows, :] + _dot(yh, w_ref[0:WIDTH, :])
             + _dot(y_ref[rows, WIDTH:2 * WIDTH], w_ref[WIDTH:2 * WIDTH, :]))
        ms = jnp.mean(h * h, axis=-1, keepdims=True)
        o_ref[rows, :] = h * lax.rsqrt(ms + EPS) * fw_ref[...]


def _out_call(y, x, nw, wz, w, fw, *, tm):
    R = x.shape[0]
    assert R % tm == 0 and tm % min(tm, OUT_ROW_BLOCK) == 0
    row_spec = pl.BlockSpec((tm, D_MODEL), lambda i: (i, 0))
    const = lambda shape: pl.BlockSpec(shape, lambda i: (0, 0))
    return pl.pallas_call(
        _out_kernel,
        grid=(R // tm,),
        in_specs=[row_spec, row_spec, const((1, D_MODEL)), const((D_MODEL, WIDTH)),
                  const((2 * WIDTH, D_MODEL)), const((1, D_MODEL))],
        out_specs=row_spec,
        out_shape=jax.ShapeDtypeStruct((R, D_MODEL), F32),
        compiler_params=pltpu.CompilerParams(
            dimension_semantics=("parallel",), vmem_limit_bytes=VMEM_LIMIT),
        name="out",
    )(y, x, nw, wz, w, fw)


def kernel(x, meta_tokens, norm_w, w_in, conv_w, hg_lb_logits, hg_norm_w, gdn_A_log,
           gdn_dt_bias, gdn_norm_w, w_out, final_norm_w):
    B, T, D = x.shape
    assert D == D_MODEL and norm_w.shape[0] == 1 and meta_tokens.shape == (N_META, D_MODEL)
    assert w_in.shape == (1, D_MODEL, 8 * WIDTH + 2 * HEADS)

    w = jnp.pad(w_in[0], ((0, 0), (0, AB_PAD - 2 * HEADS))).astype(BF16)
    pad = lambda v: jnp.zeros((1, AB_PAD), F32).at[0, :HEADS].set(v.astype(F32))
    alog, dtb = pad(gdn_A_log[0]), pad(gdn_dt_bias[0])
    nw = norm_w[0].reshape(1, D_MODEL)
    cw = conv_w[0]
    proj = functools.partial(_proj_call, nw=nw, w=w, cw=cw, lbl=hg_lb_logits, alog=alog,
                             dtb=dtb)

    zero_tail = jnp.zeros((CONV_HIST, 3 * WIDTH), F32)
    m = proj(meta_tokens[None], tail=zero_tail, tm=N_META, chunk=N_META, emit_tail=True)
    _, mhk, mhv, mhg, _, mgk, mgv, _, mgcol, mgrow, tail = m
    sh0, sg0 = _meta_state_call(mhk, mhv, mhg, mgk, mgv, mgcol, mgrow)

    hq, hk, hv, hg, gq, gk, gv, gz, gcol, grow = proj(
        x, tail=tail, tm=PROJ_ROWS, chunk=CHUNK, emit_tail=False)
    y = _mixer_call(hq, hk, hv, hg, gq, gk, gv, gz, gcol, grow, sh0, sg0,
                    hg_norm_w[0].reshape(1, DH), gdn_norm_w[0].reshape(1, DH),
                    bb=MIXER_BATCH, tt=MIXER_ROWS, chunk=CHUNK)
    wz = w_in[0, :, 3 * WIDTH:4 * WIDTH].astype(BF16)
    out = _out_call(y.reshape(B * T, 2 * WIDTH), x.reshape(B * T, D_MODEL), nw, wz,
                    w_out[0].astype(BF16), final_norm_w.reshape(1, D_MODEL), tm=OUT_ROWS)
    return out.reshape(B, T, D_MODEL)
```

```python
import functools

import jax
import jax.numpy as jnp
from jax import lax
from jax.experimental import pallas as pl
from jax.experimental.pallas import tpu as pltpu

F32 = jnp.float32
BF16 = jnp.bfloat16

D_MODEL = 1024
N_META = 16
HEADS = 4
DH = 128
WIDTH = HEADS * DH
CONV_W = 4
EPS = 1e-6
AB_PAD = 128
IN_PAD = 8 * WIDTH + AB_PAD

CHUNK = 64
SUB = 16
SAFE_RANGE = 60.0
CONV_HIST = 8
CUM_ROWS = 256
CUM_PASSES = 2
ROW_BLOCK = 128
PROJ_ROWS = 1024
MIXER_BATCH = 8
MIXER_ROWS = 128
OUT_ROWS = 1024
OUT_ROW_BLOCK = 256
RECUR_GAP = 3
V7X_VMEM_BYTES = 64 * 1024 * 1024
VMEM_LIMIT = V7X_VMEM_BYTES * 7 // 8


def _dot(a, b):
    return jnp.dot(a, b, preferred_element_type=F32)


def _dot_nt(a, b):
    return lax.dot_general(a, b, (((1,), (1,)), ((), ())), preferred_element_type=F32)


def _dot_tn(a, b):
    return lax.dot_general(a, b, (((0,), (0,)), ((), ())), preferred_element_type=F32)


def _split3(x):
    x1 = x.astype(BF16)
    r = x - x1.astype(F32)
    x2 = r.astype(BF16)
    x3 = (r - x2.astype(F32)).astype(BF16)
    return x1, x2, x3


def _silu(x):
    return x * jax.nn.sigmoid(x)


def _softplus(x):
    return jnp.maximum(x, 0.0) + jnp.log1p(jnp.exp(-jnp.abs(x)))


def _chunk_cumsum(x, chunk):
    rows = x.shape[0]
    blk = min(rows, CUM_ROWS)
    r = lax.broadcasted_iota(jnp.int32, (blk, blk), 0)
    c = lax.broadcasted_iota(jnp.int32, (blk, blk), 1)
    tri = jnp.where((r // chunk == c // chunk) & (c <= r), 1.0, 0.0).astype(BF16)
    outs = []
    for i in range(rows // blk):
        parts = _split3(x[i * blk:(i + 1) * blk])[:CUM_PASSES]
        acc = _dot(tri, parts[0])
        for part in parts[1:]:
            acc = acc + _dot(tri, part)
        outs.append(acc)
    return outs[0] if len(outs) == 1 else jnp.concatenate(outs, axis=0)


def _chunk_cumsum_lanes(x, chunk):
    n = x.shape[1]
    r = lax.broadcasted_iota(jnp.int32, (n, n), 0)
    c = lax.broadcasted_iota(jnp.int32, (n, n), 1)
    tri = jnp.where((r // chunk == c // chunk) & (r <= c), 1.0, 0.0).astype(BF16)
    parts = _split3(x)
    return _dot(parts[0], tri) + _dot(parts[1], tri) + _dot(parts[2], tri)


def _proj_kernel(x_ref, nw_ref, w_ref, cw_ref, lbl_ref, alog_ref, dtb_ref, tail_ref,
                 hq_ref, hk_ref, hv_ref, hg_ref, gq_ref, gk_ref, gv_ref, gz_ref,
                 gcol_ref, grow_ref, *rest, tm, chunk, emit_tail):
    if emit_tail:
        tail_out_ref, cbuf, logf_buf, ab_buf = rest
    else:
        cbuf, logf_buf, ab_buf = rest
    j = pl.program_id(1)

    @pl.when(j == 0)
    def _():
        cbuf[0:CONV_HIST, :] = tail_ref[...]

    lbl = lbl_ref[...]
    e = jnp.exp(lbl - jnp.max(lbl, axis=0, keepdims=True))
    lb = e[0:1] / jnp.sum(e, axis=0, keepdims=True)

    rb = min(tm, ROW_BLOCK)
    for r0 in range(0, tm, rb):
        rows = slice(r0, r0 + rb)
        x = x_ref[0, rows, :]
        ms = jnp.mean(x * x, axis=-1, keepdims=True)
        u = (x * lax.rsqrt(ms + EPS) * nw_ref[...]).astype(BF16)

        def proj(a, b, u=u):
            return _dot(u, w_ref[:, a:b])

        cbuf[CONV_HIST + r0:CONV_HIST + r0 + rb, :] = proj(4 * WIDTH, 7 * WIDTH)
        xe = cbuf[r0:r0 + CONV_HIST + rb, :]
        acc = cw_ref[0:1, :] * xe
        for t in range(1, CONV_W):
            acc = pltpu.roll(acc, 1, 0) + cw_ref[t:t + 1, :] * xe
        qkv = _silu(acc[CONV_HIST:CONV_HIST + rb])
        for h in range(HEADS):
            q = qkv[:, h * DH:(h + 1) * DH]
            k = qkv[:, WIDTH + h * DH:WIDTH + (h + 1) * DH]
            q = q * lax.rsqrt(jnp.sum(q * q, axis=-1, keepdims=True) + EPS) * (DH ** -0.5)
            k = k * lax.rsqrt(jnp.sum(k * k, axis=-1, keepdims=True) + EPS)
            gq_ref[0, rows, h * DH:(h + 1) * DH] = q.astype(BF16)
            gk_ref[0, rows, h * DH:(h + 1) * DH] = k.astype(BF16)
        gv_ref[0, rows, :] = qkv[:, 2 * WIDTH:3 * WIDTH].astype(BF16)
        gz_ref[0, rows, :] = _silu(proj(7 * WIDTH, 8 * WIDTH)).astype(BF16)

        hq_ref[0, rows, :] = _silu(proj(0, WIDTH)).astype(BF16)
        f = lb + (1.0 - lb) * jax.nn.sigmoid(proj(WIDTH, 2 * WIDTH))
        hk_ref[0, rows, :] = (1.0 - f).astype(BF16)
        logf_buf[rows, :] = jnp.log(f)
        hv_ref[0, rows, :] = proj(2 * WIDTH, 3 * WIDTH).astype(BF16)

        ab_buf[rows, :] = proj(8 * WIDTH, 8 * WIDTH + AB_PAD)

    for r0 in range(0, tm, rb):
        rows = slice(r0, r0 + rb)
        hg_ref[0, rows, :] = _chunk_cumsum(logf_buf[rows, :], chunk)
        p = ab_buf[rows, :]
        lane = lax.broadcasted_iota(jnp.int32, p.shape, 1)
        g = jnp.where(lane < HEADS, -jnp.exp(alog_ref[...]) * _softplus(p + dtb_ref[...]), 0.0)
        gam_rows = _chunk_cumsum_lanes(g.T[0:2 * HEADS, :], chunk)
        for c in range(rb // chunk):
            grow_ref[0, r0 // chunk + c] = gam_rows[:, c * chunk:(c + 1) * chunk]
        gam = jnp.concatenate(
            [gam_rows, jnp.zeros((AB_PAD - 2 * HEADS, rb), F32)], axis=0).T
        gcol_ref[0, rows, :] = jnp.where(
            lane < HEADS, gam, jnp.where(lane < 2 * HEADS, jax.nn.sigmoid(p), 0.0))

    hist = cbuf[tm:tm + CONV_HIST, :]
    cbuf[0:CONV_HIST, :] = hist
    if emit_tail:
        tail_out_ref[...] = hist


def _proj_call(x, nw, w, cw, lbl, alog, dtb, tail, *, tm, chunk, emit_tail):
    B, T, _ = x.shape
    assert T % tm == 0 and tm % min(tm, ROW_BLOCK) == 0 and min(tm, ROW_BLOCK) % chunk == 0
    nt = T // tm
    tok = lambda width, dt: jax.ShapeDtypeStruct((B, T, width), dt)
    tok_spec = lambda width: pl.BlockSpec((1, tm, width), lambda b, j: (b, j, 0))
    const = lambda shape: pl.BlockSpec(shape, lambda b, j: (0,) * len(shape))
    out_shape = [tok(WIDTH, BF16), tok(WIDTH, BF16), tok(WIDTH, BF16), tok(WIDTH, F32),
                 tok(WIDTH, BF16), tok(WIDTH, BF16), tok(WIDTH, BF16), tok(WIDTH, BF16),
                 tok(AB_PAD, F32),
                 jax.ShapeDtypeStruct((B, T // chunk, 2 * HEADS, chunk), F32)]
    out_specs = [tok_spec(WIDTH), tok_spec(WIDTH), tok_spec(WIDTH), tok_spec(WIDTH),
                 tok_spec(WIDTH), tok_spec(WIDTH), tok_spec(WIDTH), tok_spec(WIDTH),
                 tok_spec(AB_PAD),
                 pl.BlockSpec((1, tm // chunk, 2 * HEADS, chunk), lambda b, j: (b, j, 0, 0))]
    if emit_tail:
        out_shape.append(jax.ShapeDtypeStruct((CONV_HIST, 3 * WIDTH), F32))
        out_specs.append(const((CONV_HIST, 3 * WIDTH)))
    return pl.pallas_call(
        functools.partial(_proj_kernel, tm=tm, chunk=chunk, emit_tail=emit_tail),
        grid=(B, nt),
        in_specs=[tok_spec(D_MODEL), const((1, D_MODEL)),
                  pl.BlockSpec((D_MODEL, IN_PAD), lambda b, j: (0, 0),
                               pipeline_mode=pl.Buffered(1)),
                  const((CONV_W, 3 * WIDTH)), const(lbl.shape), const((1, AB_PAD)),
                  const((1, AB_PAD)), const((CONV_HIST, 3 * WIDTH))],
        out_specs=out_specs,
        out_shape=out_shape,
        scratch_shapes=[pltpu.VMEM((tm + CONV_HIST, 3 * WIDTH), F32),
                        pltpu.VMEM((tm, WIDTH), F32), pltpu.VMEM((tm, AB_PAD), F32)],
        compiler_params=pltpu.CompilerParams(
            dimension_semantics=("parallel", "arbitrary"), vmem_limit_bytes=VMEM_LIMIT),
        name="proj_meta" if emit_tail else "proj",
    )(x, nw, w, cw, lbl, alog, dtb, tail)


def _run_interleaved(*gens):
    live = list(gens)
    while live:
        for g in list(live):
            try:
                next(g)
            except StopIteration:
                live.remove(g)


def _scaled(x, decay):
    return x * decay.astype(BF16)


def _spread(gen, gap):
    for _ in gen:
        yield
        for _ in range(gap):
            yield


def _hgrn2_intra_stages(q, k, v, G, out, *, bounded):
    n = len(q)
    C = q[0].shape[0]
    row = lax.broadcasted_iota(jnp.int32, (C, C), 0)
    col = lax.broadcasted_iota(jnp.int32, (C, C), 1)
    if bounded:
        g_mid = [G[i][C // 2 - 1:C // 2, :] for i in range(n)]
        p = [_dot_nt(_scaled(q[i], jnp.exp(G[i] - g_mid[i])),
                     _scaled(k[i], jnp.exp(g_mid[i] - G[i]))) for i in range(n)]
        yield
        out["o"] = [_dot(jnp.where(row >= col, p[i], 0.0).astype(BF16), v[i]) for i in range(n)]
        return
    nb = C // SUB
    qf = [q[i].astype(F32) for i in range(n)]
    kf = [k[i].astype(F32) for i in range(n)]
    a = [jnp.zeros((C, C), F32) for _ in range(n)]
    g_end = [[G[i][(j + 1) * SUB - 1:(j + 1) * SUB, :] for j in range(nb)] for i in range(n)]
    if nb > 1:
        p = []
        for i in range(n):
            g_ref = jnp.concatenate(
                [jnp.broadcast_to(g_end[i][j], (SUB, DH)) for j in range(nb)], axis=0)
            kp = (kf[i] * jnp.exp(g_ref - G[i])).astype(BF16)
            qp = jnp.concatenate(
                [qf[i] * jnp.exp(jnp.minimum(G[i] - g_end[i][j], 0.0)) for j in range(nb - 1)],
                axis=0).astype(BF16)
            p.append(_dot_nt(qp, kp))
        yield
        for i in range(n):
            for j in range(nb - 1):
                m = (col // SUB == j) & (row // SUB > j)
                a[i] = jnp.where(m, p[i][j * C:(j + 1) * C], a[i])
    rowb = lax.broadcasted_iota(jnp.int32, (SUB, C), 0)
    colb = lax.broadcasted_iota(jnp.int32, (SUB, C), 1)
    o = []
    for i in range(n):
        for b in range(nb):
            gb = G[i][b * SUB:(b + 1) * SUB]
            qb = qf[i][b * SUB:(b + 1) * SUB]
            kb = kf[i][b * SUB:(b + 1) * SUB]
            d = jnp.zeros((SUB, C), F32)
            for s in range(SUB):
                es = jnp.exp(jnp.minimum(gb - gb[s:s + 1], 0.0))
                cs = jnp.sum(qb * es * kb[s:s + 1], axis=-1, keepdims=True)
                d = jnp.where((colb == b * SUB + s) & (rowb >= s), cs, d)
            if nb == 1:
                a[i] = d
            else:
                dfull = jnp.concatenate(
                    [d if r == b else jnp.zeros((SUB, C), F32) for r in range(nb)], axis=0)
                a[i] = jnp.where((row // SUB == b) & (col // SUB == b), dfull, a[i])
        o.append(_dot(a[i].astype(BF16), v[i]))
    out["o"] = o


def _hgrn2_state_stages(q, k, v, G, st, o_intra, out):
    n = len(k)
    C = k[0].shape[0]
    g_last = [G[i][C - 1:C, :] for i in range(n)]
    kk = [_scaled(k[i], jnp.exp(g_last[i] - G[i])) for i in range(n)]
    st_new = [_dot_tn(v[i], kk[i]) for i in range(n)]
    if st is not None:
        st_new = [st_new[i] + st[i] * jnp.exp(g_last[i]) for i in range(n)]
    out["s"] = st_new
    yield
    if o_intra is not None:
        out["o"] = [o_intra[i] + _dot_nt(_scaled(q[i], jnp.exp(G[i])), st[i].astype(BF16))
                    for i in range(n)]


def _gdn_prep_stages(q, k, gam_c, gam_r, beta_c, out, *, need_o):
    n = len(k)
    C = k[0].shape[0]
    rng = range(n)
    row = lax.broadcasted_iota(jnp.int32, (C, C), 0)
    col = lax.broadcasted_iota(jnp.int32, (C, C), 1)
    if need_o:
        pk = [_dot_nt(jnp.concatenate([q[i], k[i]], axis=0), k[i]) for i in rng]
        qk, kk = [x[:C] for x in pk], [x[C:] for x in pk]
    else:
        kk = [_dot_nt(k[i], k[i]) for i in rng]
    yield
    dec = [jnp.exp(jnp.minimum(gam_c[i] - gam_r[i], 0.0)) for i in rng]
    if need_o:
        out["qkd"] = [jnp.where(row >= col, qk[i] * dec[i], 0.0).astype(BF16) for i in rng]
    a = [jnp.where(row > col, beta_c[i] * kk[i] * dec[i], 0.0) for i in rng]
    eye = jnp.where(row == col, 1.0, 0.0)
    t = [eye - jnp.where(row // 2 == col // 2, a[i], 0.0) for i in rng]
    bs = 2
    while bs < C:
        lm = (row // (2 * bs) == col // (2 * bs)) & (row // bs > col // bs)
        tb = [t[i].astype(BF16) for i in rng]
        tl = [_dot(tb[i], jnp.where(lm, a[i], 0.0).astype(BF16)) for i in rng]
        yield
        t = [t[i] - _dot(tl[i].astype(BF16), tb[i]) for i in rng]
        yield
        bs *= 2
    out["t"] = [t[i].astype(BF16) for i in rng]


def _gdn_state_stages(q, k, v, gam_c, beta_c, t, qkd, s, out):
    n = len(k)
    C = k[0].shape[0]
    rng = range(n)
    vf = [v[i].astype(F32) for i in rng]
    qs = None
    if s is not None:
        egam = [jnp.exp(gam_c[i]) for i in rng]
        sb = [s[i].astype(BF16) for i in rng]
        if qkd is not None:
            qks = [_dot(jnp.concatenate([_scaled(q[i], egam[i]), _scaled(k[i], egam[i])],
                                        axis=0), sb[i]) for i in rng]
            qs, ks = [x[:C] for x in qks], [x[C:] for x in qks]
        else:
            ks = [_dot(_scaled(k[i], egam[i]), sb[i]) for i in rng]
        yield
        rhs = [beta_c[i] * (vf[i] - ks[i]) for i in rng]
    else:
        rhs = [beta_c[i] * vf[i] for i in rng]
    ub = [_dot(t[i], rhs[i].astype(BF16)).astype(BF16) for i in rng]
    yield
    g_last = [gam_c[i][C - 1:C, :] for i in rng]
    s_new = [_dot_tn(_scaled(k[i], jnp.exp(g_last[i] - gam_c[i])), ub[i]) for i in rng]
    if s is not None:
        s_new = [s_new[i] + s[i] * jnp.exp(g_last[i]) for i in rng]
    out["s"] = s_new
    if qkd is not None:
        o = [_dot(qkd[i], ub[i]) for i in rng]
        out["o"] = o if qs is None else [o[i] + qs[i] for i in rng]


def _head_norm(o, w):
    return o * lax.rsqrt(jnp.mean(o * o, axis=-1, keepdims=True) + EPS) * w


def _mixer_kernel(hq_ref, hk_ref, hv_ref, hg_ref, gq_ref, gk_ref, gv_ref, gz_ref,
                  gcol_ref, grow_ref, sh0_ref, sg0_ref, hw_ref, gw_ref,
                  y_ref, sh_ref, sg_ref, t_buf, qkd_buf, oi_buf, bounded_ref, *, bb, tt, chunk):
    j = pl.program_id(1)
    nchunks = tt // chunk

    @pl.when(j == 0)
    def _():
        for b in range(bb):
            sh_ref[b] = sh0_ref[...]
            sg_ref[b] = sg0_ref[...]

    for c in range(nchunks):
        span = None
        for b in range(bb):
            g_mid = hg_ref[b, c * chunk + chunk // 2 - 1:c * chunk + chunk // 2, :]
            g_last = hg_ref[b, (c + 1) * chunk - 1:(c + 1) * chunk, :]
            s = jnp.maximum(-g_mid, g_mid - g_last)
            span = s if span is None else jnp.maximum(span, s)
        bounded_ref[c] = jnp.where(jnp.max(span) <= SAFE_RANGE, 1, 0)

    hw = hw_ref[...]
    gw = gw_ref[...]
    probs = [(b, h) for b in range(bb) for h in range(HEADS)]
    hs = lambda h: slice(h * DH, (h + 1) * DH)

    def chunk_rows(c):
        return pl.ds(pl.multiple_of(c * chunk, chunk), chunk)

    def prep_stages(c, pout, hout, bounded):
        rows = chunk_rows(c)
        ld = lambda ref: [ref[b, rows, hs(h)] for b, h in probs]
        gcol = [gcol_ref[b, rows, :] for b in range(bb)]
        grow = [grow_ref[b, c] for b in range(bb)]
        return (_gdn_prep_stages(ld(gq_ref), ld(gk_ref),
                                 [gcol[b][:, h:h + 1] for b, h in probs],
                                 [grow[b][h:h + 1, :] for b, h in probs],
                                 [gcol[b][:, HEADS + h:HEADS + h + 1] for b, h in probs],
                                 pout, need_o=True),
                _hgrn2_intra_stages(ld(hq_ref), ld(hk_ref), ld(hv_ref), ld(hg_ref), hout,
                                    bounded=bounded))

    def store_prep(c, pout, hout):
        for i in range(len(probs)):
            t_buf[c, i] = pout["t"][i]
            qkd_buf[c, i] = pout["qkd"][i]
            oi_buf[c, i] = hout["o"][i]

    def recur_stages(c, gout, hout):
        rows = chunk_rows(c)
        ld = lambda ref: [ref[b, rows, hs(h)] for b, h in probs]
        gcol = [gcol_ref[b, rows, :] for b in range(bb)]
        n = len(probs)
        return (_gdn_state_stages(ld(gq_ref), ld(gk_ref), ld(gv_ref),
                                  [gcol[b][:, h:h + 1] for b, h in probs],
                                  [gcol[b][:, HEADS + h:HEADS + h + 1] for b, h in probs],
                                  [t_buf[c, i] for i in range(n)],
                                  [qkd_buf[c, i] for i in range(n)],
                                  [sg_ref[b, h] for b, h in probs], gout),
                _hgrn2_state_stages(ld(hq_ref), ld(hk_ref), ld(hv_ref), ld(hg_ref),
                                    [sh_ref[b, h] for b, h in probs],
                                    [oi_buf[c, i] for i in range(n)], hout))

    def store_recur(c, gout, hout):
        rows = chunk_rows(c)
        for i, (b, h) in enumerate(probs):
            sh_ref[b, h] = hout["s"][i]
            sg_ref[b, h] = gout["s"][i]
            y_ref[b, rows, hs(h)] = _head_norm(hout["o"][i], hw).astype(BF16)
            ys = slice(WIDTH + h * DH, WIDTH + (h + 1) * DH)
            y_ref[b, rows, ys] = (_head_norm(gout["o"][i], gw)
                                  * gz_ref[b, rows, hs(h)].astype(F32)).astype(BF16)

    def prep_only(c, bounded):
        pout, hout = {}, {}
        _run_interleaved(*prep_stages(c, pout, hout, bounded))
        store_prep(c, pout, hout)

    def recur_only(c):
        gout, hout = {}, {}
        _run_interleaved(*recur_stages(c, gout, hout))
        store_recur(c, gout, hout)

    def pipelined_step():
        prep_only(0, True)

        def body(c, carry):
            pout, phout, gout, hout = {}, {}, {}, {}
            gprep, hprep = prep_stages(c + 1, pout, phout, True)
            gstate, hstate = recur_stages(c, gout, hout)
            _run_interleaved(gprep, _spread(gstate, RECUR_GAP), _spread(hstate, RECUR_GAP),
                             hprep)
            store_prep(c + 1, pout, phout)
            store_recur(c, gout, hout)
            return carry

        lax.fori_loop(0, nchunks - 1, body, 0)
        recur_only(nchunks - 1)

    def plain_step():
        def body(c, carry):
            prep_only(c, False)
            recur_only(c)
            return carry

        lax.fori_loop(0, nchunks, body, 0)

    all_bounded = bounded_ref[0] == 1
    for c in range(1, nchunks):
        all_bounded = jnp.logical_and(all_bounded, bounded_ref[c] == 1)
    pl.when(all_bounded)(pipelined_step)
    pl.when(jnp.logical_not(all_bounded))(plain_step)


def _mixer_call(hq, hk, hv, hg, gq, gk, gv, gz, gcol, grow, sh0, sg0, hw, gw, *, bb, tt, chunk):
    B, T, _ = hq.shape
    assert B % bb == 0 and T % tt == 0 and tt % chunk == 0
    nchunks = tt // chunk
    tok_spec = lambda width: pl.BlockSpec((bb, tt, width), lambda i, j: (i, j, 0))
    const = lambda shape: pl.BlockSpec(shape, lambda i, j: (0,) * len(shape))
    return pl.pallas_call(
        functools.partial(_mixer_kernel, bb=bb, tt=tt, chunk=chunk),
        grid=(B // bb, T // tt),
        in_specs=[tok_spec(WIDTH), tok_spec(WIDTH), tok_spec(WIDTH), tok_spec(WIDTH),
                  tok_spec(WIDTH), tok_spec(WIDTH), tok_spec(WIDTH), tok_spec(WIDTH),
                  tok_spec(AB_PAD),
                  pl.BlockSpec((bb, nchunks, 2 * HEADS, chunk), lambda i, j: (i, j, 0, 0)),
                  const((HEADS, DH, DH)), const((HEADS, DH, DH)),
                  const((1, DH)), const((1, DH))],
        out_specs=tok_spec(2 * WIDTH),
        out_shape=jax.ShapeDtypeStruct((B, T, 2 * WIDTH), BF16),
        scratch_shapes=[pltpu.VMEM((bb, HEADS, DH, DH), F32),
                        pltpu.VMEM((bb, HEADS, DH, DH), F32),
                        pltpu.VMEM((nchunks, bb * HEADS, chunk, chunk), BF16),
                        pltpu.VMEM((nchunks, bb * HEADS, chunk, chunk), BF16),
                        pltpu.VMEM((nchunks, bb * HEADS, chunk, DH), F32),
                        pltpu.SMEM((nchunks,), jnp.int32)],
        compiler_params=pltpu.CompilerParams(
            dimension_semantics=("parallel", "arbitrary"), vmem_limit_bytes=VMEM_LIMIT),
        name="mixer",
    )(hq, hk, hv, hg, gq, gk, gv, gz, gcol, grow, sh0, sg0, hw, gw)


def _meta_state_kernel(hk_ref, hv_ref, hg_ref, gk_ref, gv_ref, gcol_ref, grow_ref,
                       sh_ref, sg_ref):
    gcol = gcol_ref[0]
    grow = grow_ref[0, 0]
    hs = lambda h: slice(h * DH, (h + 1) * DH)
    heads = range(HEADS)
    hk = [hk_ref[0, :, hs(h)] for h in heads]
    gk = [gk_ref[0, :, hs(h)] for h in heads]
    gam_c = [gcol[:, h:h + 1] for h in heads]
    beta_c = [gcol[:, HEADS + h:HEADS + h + 1] for h in heads]
    prep, hout, gout = {}, {}, {}
    _run_interleaved(
        _gdn_prep_stages(None, gk, gam_c, [grow[h:h + 1, :] for h in heads], beta_c, prep,
                         need_o=False),
        _hgrn2_state_stages(None, hk, [hv_ref[0, :, hs(h)] for h in heads],
                            [hg_ref[0, :, hs(h)] for h in heads], None, None, hout))
    _run_interleaved(
        _gdn_state_stages(None, gk, [gv_ref[0, :, hs(h)] for h in heads], gam_c, beta_c,
                          prep["t"], None, None, gout))
    for h in heads:
        sh_ref[h] = hout["s"][h]
        sg_ref[h] = gout["s"][h]


def _meta_state_call(hk, hv, hg, gk, gv, gcol, grow):
    state = jax.ShapeDtypeStruct((HEADS, DH, DH), F32)
    return pl.pallas_call(
        _meta_state_kernel,
        out_shape=[state, state],
        name="meta_state",
    )(hk, hv, hg, gk, gv, gcol, grow)


def _out_kernel(y_ref, x_ref, nw_ref, wz_ref, w_ref, fw_ref, o_ref):
    tm = x_ref.shape[0]
    rb = min(tm, OUT_ROW_BLOCK)
    blocks = [slice(r0, r0 + rb) for r0 in range(0, tm, rb)]
    gates = []
    for rows in blocks:
        x = x_ref[rows, :]
        ms = jnp.mean(x * x, axis=-1, keepdims=True)
        u = (x * lax.rsqrt(ms + EPS) * nw_ref[...]).astype(BF16)
        gates.append(_silu(_dot(u, wz_ref[...])))
    for rows, gate in zip(blocks, gates):
        yh = (y_ref[rows, 0:WIDTH].astype(F32) * gate).astype(BF16)
        h = (x_ref[rows, :] + _dot(yh, w_ref[0:WIDTH, :])
             + _dot(y_ref[rows, WIDTH:2 * WIDTH], w_ref[WIDTH:2 * WIDTH, :]))
        ms = jnp.mean(h * h, axis=-1, keepdims=True)
        o_ref[rows, :] = h * lax.rsqrt(ms + EPS) * fw_ref[...]


def _out_call(y, x, nw, wz, w, fw, *, tm):
    R = x.shape[0]
    assert R % tm == 0 and tm % min(tm, OUT_ROW_BLOCK) == 0
    row_spec = pl.BlockSpec((tm, D_MODEL), lambda i: (i, 0))
    const = lambda shape: pl.BlockSpec(shape, lambda i: (0, 0))
    return pl.pallas_call(
        _out_kernel,
        grid=(R // tm,),
        in_specs=[row_spec, row_spec, const((1, D_MODEL)), const((D_MODEL, WIDTH)),
                  const((2 * WIDTH, D_MODEL)), const((1, D_MODEL))],
        out_specs=row_spec,
        out_shape=jax.ShapeDtypeStruct((R, D_MODEL), F32),
        compiler_params=pltpu.CompilerParams(
            dimension_semantics=("parallel",), vmem_limit_bytes=VMEM_LIMIT),
        name="out",
    )(y, x, nw, wz, w, fw)


def kernel(x, meta_tokens, norm_w, w_in, conv_w, hg_lb_logits, hg_norm_w, gdn_A_log,
           gdn_dt_bias, gdn_norm_w, w_out, final_norm_w):
    B, T, D = x.shape
    assert D == D_MODEL and norm_w.shape[0] == 1 and meta_tokens.shape == (N_META, D_MODEL)
    assert w_in.shape == (1, D_MODEL, 8 * WIDTH + 2 * HEADS)

    w = jnp.pad(w_in[0], ((0, 0), (0, AB_PAD - 2 * HEADS))).astype(BF16)
    pad = lambda v: jnp.zeros((1, AB_PAD), F32).at[0, :HEADS].set(v.astype(F32))
    alog, dtb = pad(gdn_A_log[0]), pad(gdn_dt_bias[0])
    nw = norm_w[0].reshape(1, D_MODEL)
    cw = conv_w[0]
    proj = functools.partial(_proj_call, nw=nw, w=w, cw=cw, lbl=hg_lb_logits, alog=alog,
                             dtb=dtb)

    zero_tail = jnp.zeros((CONV_HIST, 3 * WIDTH), F32)
    m = proj(meta_tokens[None], tail=zero_tail, tm=N_META, chunk=N_META, emit_tail=True)
    _, mhk, mhv, mhg, _, mgk, mgv, _, mgcol, mgrow, tail = m
    sh0, sg0 = _meta_state_call(mhk, mhv, mhg, mgk, mgv, mgcol, mgrow)

    hq, hk, hv, hg, gq, gk, gv, gz, gcol, grow = proj(
        x, tail=tail, tm=PROJ_ROWS, chunk=CHUNK, emit_tail=False)
    y = _mixer_call(hq, hk, hv, hg, gq, gk, gv, gz, gcol, grow, sh0, sg0,
                    hg_norm_w[0].reshape(1, DH), gdn_norm_w[0].reshape(1, DH),
                    bb=MIXER_BATCH, tt=MIXER_ROWS, chunk=CHUNK)
    wz = w_in[0, :, 3 * WIDTH:4 * WIDTH].astype(BF16)
    out = _out_call(y.reshape(B * T, 2 * WIDTH), x.reshape(B * T, D_MODEL), nw, wz,
                    w_out[0].astype(BF16), final_norm_w.reshape(1, D_MODEL), tm=OUT_ROWS)
    return out.reshape(B, T, D_MODEL)
```

```python
import functools

import jax
import jax.numpy as jnp
from jax import lax
from jax.experimental import pallas as pl
from jax.experimental.pallas import tpu as pltpu

F32 = jnp.float32
BF16 = jnp.bfloat16

D_MODEL = 1024
N_META = 16
HEADS = 4
DH = 128
WIDTH = HEADS * DH
CONV_W = 4
EPS = 1e-6
AB_PAD = 128
IN_PAD = 8 * WIDTH + AB_PAD

CHUNK = 64
SUB = 16
SAFE_RANGE = 60.0
CONV_HIST = 8
CUM_ROWS = 256
ROW_BLOCK = 128
PROJ_ROWS = 1024
MIXER_BATCH = 8
MIXER_ROWS = 128
OUT_ROWS = 1024
OUT_ROW_BLOCK = 256
RECUR_GAP = 3
V7X_VMEM_BYTES = 64 * 1024 * 1024
VMEM_LIMIT = V7X_VMEM_BYTES * 7 // 8


def _dot(a, b):
    return jnp.dot(a, b, preferred_element_type=F32)


def _dot_nt(a, b):
    return lax.dot_general(a, b, (((1,), (1,)), ((), ())), preferred_element_type=F32)


def _dot_tn(a, b):
    return lax.dot_general(a, b, (((0,), (0,)), ((), ())), preferred_element_type=F32)


def _split3(x):
    x1 = x.astype(BF16)
    r = x - x1.astype(F32)
    x2 = r.astype(BF16)
    x3 = (r - x2.astype(F32)).astype(BF16)
    return x1, x2, x3


def _silu(x):
    return x * jax.nn.sigmoid(x)


def _softplus(x):
    return jnp.maximum(x, 0.0) + jnp.log1p(jnp.exp(-jnp.abs(x)))


def _chunk_cumsum(x, chunk):
    rows = x.shape[0]
    blk = min(rows, CUM_ROWS)
    r = lax.broadcasted_iota(jnp.int32, (blk, blk), 0)
    c = lax.broadcasted_iota(jnp.int32, (blk, blk), 1)
    tri = jnp.where((r // chunk == c // chunk) & (c <= r), 1.0, 0.0).astype(BF16)
    outs = []
    for i in range(rows // blk):
        parts = _split3(x[i * blk:(i + 1) * blk])
        outs.append(_dot(tri, parts[0]) + _dot(tri, parts[1]) + _dot(tri, parts[2]))
    return outs[0] if len(outs) == 1 else jnp.concatenate(outs, axis=0)


def _chunk_cumsum_lanes(x, chunk):
    n = x.shape[1]
    r = lax.broadcasted_iota(jnp.int32, (n, n), 0)
    c = lax.broadcasted_iota(jnp.int32, (n, n), 1)
    tri = jnp.where((r // chunk == c // chunk) & (r <= c), 1.0, 0.0).astype(BF16)
    parts = _split3(x)
    return _dot(parts[0], tri) + _dot(parts[1], tri) + _dot(parts[2], tri)


def _proj_kernel(x_ref, nw_ref, w_ref, cw_ref, lbl_ref, alog_ref, dtb_ref, tail_ref,
                 hq_ref, hk_ref, hv_ref, hg_ref, gq_ref, gk_ref, gv_ref, gz_ref,
                 gcol_ref, grow_ref, *rest, tm, chunk, emit_tail):
    if emit_tail:
        tail_out_ref, cbuf, logf_buf, ab_buf = rest
    else:
        cbuf, logf_buf, ab_buf = rest
    j = pl.program_id(1)

    @pl.when(j == 0)
    def _():
        cbuf[0:CONV_HIST, :] = tail_ref[...]

    lbl = lbl_ref[...]
    e = jnp.exp(lbl - jnp.max(lbl, axis=0, keepdims=True))
    lb = e[0:1] / jnp.sum(e, axis=0, keepdims=True)

    rb = min(tm, ROW_BLOCK)
    for r0 in range(0, tm, rb):
        rows = slice(r0, r0 + rb)
        x = x_ref[0, rows, :]
        ms = jnp.mean(x * x, axis=-1, keepdims=True)
        u = (x * lax.rsqrt(ms + EPS) * nw_ref[...]).astype(BF16)

        def proj(a, b, u=u):
            return _dot(u, w_ref[:, a:b])

        cbuf[CONV_HIST + r0:CONV_HIST + r0 + rb, :] = proj(4 * WIDTH, 7 * WIDTH)
        xe = cbuf[r0:r0 + CONV_HIST + rb, :]
        acc = cw_ref[0:1, :] * xe
        for t in range(1, CONV_W):
            acc = pltpu.roll(acc, 1, 0) + cw_ref[t:t + 1, :] * xe
        qkv = _silu(acc[CONV_HIST:CONV_HIST + rb])
        for h in range(HEADS):
            q = qkv[:, h * DH:(h + 1) * DH]
            k = qkv[:, WIDTH + h * DH:WIDTH + (h + 1) * DH]
            q = q * lax.rsqrt(jnp.sum(q * q, axis=-1, keepdims=True) + EPS) * (DH ** -0.5)
            k = k * lax.rsqrt(jnp.sum(k * k, axis=-1, keepdims=True) + EPS)
            gq_ref[0, rows, h * DH:(h + 1) * DH] = q.astype(BF16)
            gk_ref[0, rows, h * DH:(h + 1) * DH] = k.astype(BF16)
        gv_ref[0, rows, :] = qkv[:, 2 * WIDTH:3 * WIDTH].astype(BF16)
        gz_ref[0, rows, :] = _silu(proj(7 * WIDTH, 8 * WIDTH)).astype(BF16)

        hq_ref[0, rows, :] = _silu(proj(0, WIDTH)).astype(BF16)
        f = lb + (1.0 - lb) * jax.nn.sigmoid(proj(WIDTH, 2 * WIDTH))
        hk_ref[0, rows, :] = (1.0 - f).astype(BF16)
        logf_buf[rows, :] = jnp.log(f)
        hv_ref[0, rows, :] = proj(2 * WIDTH, 3 * WIDTH).astype(BF16)

        ab_buf[rows, :] = proj(8 * WIDTH, 8 * WIDTH + AB_PAD)

    for r0 in range(0, tm, rb):
        rows = slice(r0, r0 + rb)
        hg_ref[0, rows, :] = _chunk_cumsum(logf_buf[rows, :], chunk)
        p = ab_buf[rows, :]
        lane = lax.broadcasted_iota(jnp.int32, p.shape, 1)
        g = jnp.where(lane < HEADS, -jnp.exp(alog_ref[...]) * _softplus(p + dtb_ref[...]), 0.0)
        gam_rows = _chunk_cumsum_lanes(g.T[0:2 * HEADS, :], chunk)
        for c in range(rb // chunk):
            grow_ref[0, r0 // chunk + c] = gam_rows[:, c * chunk:(c + 1) * chunk]
        gam = jnp.concatenate(
            [gam_rows, jnp.zeros((AB_PAD - 2 * HEADS, rb), F32)], axis=0).T
        gcol_ref[0, rows, :] = jnp.where(
            lane < HEADS, gam, jnp.where(lane < 2 * HEADS, jax.nn.sigmoid(p), 0.0))

    hist = cbuf[tm:tm + CONV_HIST, :]
    cbuf[0:CONV_HIST, :] = hist
    if emit_tail:
        tail_out_ref[...] = hist


def _proj_call(x, nw, w, cw, lbl, alog, dtb, tail, *, tm, chunk, emit_tail):
    B, T, _ = x.shape
    assert T % tm == 0 and tm % min(tm, ROW_BLOCK) == 0 and min(tm, ROW_BLOCK) % chunk == 0
    nt = T // tm
    tok = lambda width, dt: jax.ShapeDtypeStruct((B, T, width), dt)
    tok_spec = lambda width: pl.BlockSpec((1, tm, width), lambda b, j: (b, j, 0))
    const = lambda shape: pl.BlockSpec(shape, lambda b, j: (0,) * len(shape))
    out_shape = [tok(WIDTH, BF16), tok(WIDTH, BF16), tok(WIDTH, BF16), tok(WIDTH, F32),
                 tok(WIDTH, BF16), tok(WIDTH, BF16), tok(WIDTH, BF16), tok(WIDTH, BF16),
                 tok(AB_PAD, F32),
                 jax.ShapeDtypeStruct((B, T // chunk, 2 * HEADS, chunk), F32)]
    out_specs = [tok_spec(WIDTH), tok_spec(WIDTH), tok_spec(WIDTH), tok_spec(WIDTH),
                 tok_spec(WIDTH), tok_spec(WIDTH), tok_spec(WIDTH), tok_spec(WIDTH),
                 tok_spec(AB_PAD),
                 pl.BlockSpec((1, tm // chunk, 2 * HEADS, chunk), lambda b, j: (b, j, 0, 0))]
    if emit_tail:
        out_shape.append(jax.ShapeDtypeStruct((CONV_HIST, 3 * WIDTH), F32))
        out_specs.append(const((CONV_HIST, 3 * WIDTH)))
    return pl.pallas_call(
        functools.partial(_proj_kernel, tm=tm, chunk=chunk, emit_tail=emit_tail),
        grid=(B, nt),
        in_specs=[tok_spec(D_MODEL), const((1, D_MODEL)),
                  pl.BlockSpec((D_MODEL, IN_PAD), lambda b, j: (0, 0),
                               pipeline_mode=pl.Buffered(1)),
                  const((CONV_W, 3 * WIDTH)), const(lbl.shape), const((1, AB_PAD)),
                  const((1, AB_PAD)), const((CONV_HIST, 3 * WIDTH))],
        out_specs=out_specs,
        out_shape=out_shape,
        scratch_shapes=[pltpu.VMEM((tm + CONV_HIST, 3 * WIDTH), F32),
                        pltpu.VMEM((tm, WIDTH), F32), pltpu.VMEM((tm, AB_PAD), F32)],
        compiler_params=pltpu.CompilerParams(
            dimension_semantics=("parallel", "arbitrary"), vmem_limit_bytes=VMEM_LIMIT),
        name="proj_meta" if emit_tail else "proj",
    )(x, nw, w, cw, lbl, alog, dtb, tail)


def _run_interleaved(*gens):
    live = list(gens)
    while live:
        for g in list(live):
            try:
                next(g)
            except StopIteration:
                live.remove(g)


def _scaled(x, decay):
    return x * decay.astype(BF16)


def _spread(gen, gap):
    for _ in gen:
        yield
        for _ in range(gap):
            yield


def _hgrn2_intra_stages(q, k, v, G, out, *, bounded):
    n = len(q)
    C = q[0].shape[0]
    row = lax.broadcasted_iota(jnp.int32, (C, C), 0)
    col = lax.broadcasted_iota(jnp.int32, (C, C), 1)
    if bounded:
        g_mid = [G[i][C // 2 - 1:C // 2, :] for i in range(n)]
        p = [_dot_nt(_scaled(q[i], jnp.exp(G[i] - g_mid[i])),
                     _scaled(k[i], jnp.exp(g_mid[i] - G[i]))) for i in range(n)]
        yield
        out["o"] = [_dot(jnp.where(row >= col, p[i], 0.0).astype(BF16), v[i]) for i in range(n)]
        return
    nb = C // SUB
    qf = [q[i].astype(F32) for i in range(n)]
    kf = [k[i].astype(F32) for i in range(n)]
    a = [jnp.zeros((C, C), F32) for _ in range(n)]
    g_end = [[G[i][(j + 1) * SUB - 1:(j + 1) * SUB, :] for j in range(nb)] for i in range(n)]
    if nb > 1:
        p = []
        for i in range(n):
            g_ref = jnp.concatenate(
                [jnp.broadcast_to(g_end[i][j], (SUB, DH)) for j in range(nb)], axis=0)
            kp = (kf[i] * jnp.exp(g_ref - G[i])).astype(BF16)
            qp = jnp.concatenate(
                [qf[i] * jnp.exp(jnp.minimum(G[i] - g_end[i][j], 0.0)) for j in range(nb - 1)],
                axis=0).astype(BF16)
            p.append(_dot_nt(qp, kp))
        yield
        for i in range(n):
            for j in range(nb - 1):
                m = (col // SUB == j) & (row // SUB > j)
                a[i] = jnp.where(m, p[i][j * C:(j + 1) * C], a[i])
    rowb = lax.broadcasted_iota(jnp.int32, (SUB, C), 0)
    colb = lax.broadcasted_iota(jnp.int32, (SUB, C), 1)
    o = []
    for i in range(n):
        for b in range(nb):
            gb = G[i][b * SUB:(b + 1) * SUB]
            qb = qf[i][b * SUB:(b + 1) * SUB]
            kb = kf[i][b * SUB:(b + 1) * SUB]
            d = jnp.zeros((SUB, C), F32)
            for s in range(SUB):
                es = jnp.exp(jnp.minimum(gb - gb[s:s + 1], 0.0))
                cs = jnp.sum(qb * es * kb[s:s + 1], axis=-1, keepdims=True)
                d = jnp.where((colb == b * SUB + s) & (rowb >= s), cs, d)
            if nb == 1:
                a[i] = d
            else:
                dfull = jnp.concatenate(
                    [d if r == b else jnp.zeros((SUB, C), F32) for r in range(nb)], axis=0)
                a[i] = jnp.where((row // SUB == b) & (col // SUB == b), dfull, a[i])
        o.append(_dot(a[i].astype(BF16), v[i]))
    out["o"] = o


def _hgrn2_state_stages(q, k, v, G, st, o_intra, out):
    n = len(k)
    C = k[0].shape[0]
    g_last = [G[i][C - 1:C, :] for i in range(n)]
    kk = [_scaled(k[i], jnp.exp(g_last[i] - G[i])) for i in range(n)]
    st_new = [_dot_tn(v[i], kk[i]) for i in range(n)]
    if st is not None:
        st_new = [st_new[i] + st[i] * jnp.exp(g_last[i]) for i in range(n)]
    out["s"] = st_new
    yield
    if o_intra is not None:
        out["o"] = [o_intra[i] + _dot_nt(_scaled(q[i], jnp.exp(G[i])), st[i].astype(BF16))
                    for i in range(n)]


def _gdn_prep_stages(q, k, gam_c, gam_r, beta_c, out, *, need_o):
    n = len(k)
    C = k[0].shape[0]
    rng = range(n)
    row = lax.broadcasted_iota(jnp.int32, (C, C), 0)
    col = lax.broadcasted_iota(jnp.int32, (C, C), 1)
    if need_o:
        pk = [_dot_nt(jnp.concatenate([q[i], k[i]], axis=0), k[i]) for i in rng]
        qk, kk = [x[:C] for x in pk], [x[C:] for x in pk]
    else:
        kk = [_dot_nt(k[i], k[i]) for i in rng]
    yield
    dec = [jnp.exp(jnp.minimum(gam_c[i] - gam_r[i], 0.0)) for i in rng]
    if need_o:
        out["qkd"] = [jnp.where(row >= col, qk[i] * dec[i], 0.0).astype(BF16) for i in rng]
    a = [jnp.where(row > col, beta_c[i] * kk[i] * dec[i], 0.0) for i in rng]
    eye = jnp.where(row == col, 1.0, 0.0)
    t = [eye - jnp.where(row // 2 == col // 2, a[i], 0.0) for i in rng]
    bs = 2
    while bs < C:
        lm = (row // (2 * bs) == col // (2 * bs)) & (row // bs > col // bs)
        tb = [t[i].astype(BF16) for i in rng]
        tl = [_dot(tb[i], jnp.where(lm, a[i], 0.0).astype(BF16)) for i in rng]
        yield
        t = [t[i] - _dot(tl[i].astype(BF16), tb[i]) for i in rng]
        yield
        bs *= 2
    out["t"] = [t[i].astype(BF16) for i in rng]


def _gdn_state_stages(q, k, v, gam_c, beta_c, t, qkd, s, out):
    n = len(k)
    C = k[0].shape[0]
    rng = range(n)
    vf = [v[i].astype(F32) for i in rng]
    qs = None
    if s is not None:
        egam = [jnp.exp(gam_c[i]) for i in rng]
        sb = [s[i].astype(BF16) for i in rng]
        if qkd is not None:
            qks = [_dot(jnp.concatenate([_scaled(q[i], egam[i]), _scaled(k[i], egam[i])],
                                        axis=0), sb[i]) for i in rng]
            qs, ks = [x[:C] for x in qks], [x[C:] for x in qks]
        else:
            ks = [_dot(_scaled(k[i], egam[i]), sb[i]) for i in rng]
        yield
        rhs = [beta_c[i] * (vf[i] - ks[i]) for i in rng]
    else:
        rhs = [beta_c[i] * vf[i] for i in rng]
    ub = [_dot(t[i], rhs[i].astype(BF16)).astype(BF16) for i in rng]
    yield
    g_last = [gam_c[i][C - 1:C, :] for i in rng]
    s_new = [_dot_tn(_scaled(k[i], jnp.exp(g_last[i] - gam_c[i])), ub[i]) for i in rng]
    if s is not None:
        s_new = [s_new[i] + s[i] * jnp.exp(g_last[i]) for i in rng]
    out["s"] = s_new
    if qkd is not None:
        o = [_dot(qkd[i], ub[i]) for i in rng]
        out["o"] = o if qs is None else [o[i] + qs[i] for i in rng]


def _head_norm(o, w):
    return o * lax.rsqrt(jnp.mean(o * o, axis=-1, keepdims=True) + EPS) * w


def _mixer_kernel(hq_ref, hk_ref, hv_ref, hg_ref, gq_ref, gk_ref, gv_ref, gz_ref,
                  gcol_ref, grow_ref, sh0_ref, sg0_ref, hw_ref, gw_ref,
                  y_ref, sh_ref, sg_ref, t_buf, qkd_buf, oi_buf, bounded_ref, *, bb, tt, chunk):
    j = pl.program_id(1)
    nchunks = tt // chunk

    @pl.when(j == 0)
    def _():
        for b in range(bb):
            sh_ref[b] = sh0_ref[...]
            sg_ref[b] = sg0_ref[...]

    for c in range(nchunks):
        span = None
        for b in range(bb):
            g_mid = hg_ref[b, c * chunk + chunk // 2 - 1:c * chunk + chunk // 2, :]
            g_last = hg_ref[b, (c + 1) * chunk - 1:(c + 1) * chunk, :]
            s = jnp.maximum(-g_mid, g_mid - g_last)
            span = s if span is None else jnp.maximum(span, s)
        bounded_ref[c] = jnp.where(jnp.max(span) <= SAFE_RANGE, 1, 0)

    hw = hw_ref[...]
    gw = gw_ref[...]
    probs = [(b, h) for b in range(bb) for h in range(HEADS)]
    hs = lambda h: slice(h * DH, (h + 1) * DH)

    def chunk_rows(c):
        return pl.ds(pl.multiple_of(c * chunk, chunk), chunk)

    def prep_stages(c, pout, hout, bounded):
        rows = chunk_rows(c)
        ld = lambda ref: [ref[b, rows, hs(h)] for b, h in probs]
        gcol = [gcol_ref[b, rows, :] for b in range(bb)]
        grow = [grow_ref[b, c] for b in range(bb)]
        return (_gdn_prep_stages(ld(gq_ref), ld(gk_ref),
                                 [gcol[b][:, h:h + 1] for b, h in probs],
                                 [grow[b][h:h + 1, :] for b, h in probs],
                                 [gcol[b][:, HEADS + h:HEADS + h + 1] for b, h in probs],
                                 pout, need_o=True),
                _hgrn2_intra_stages(ld(hq_ref), ld(hk_ref), ld(hv_ref), ld(hg_ref), hout,
                                    bounded=bounded))

    def store_prep(c, pout, hout):
        for i in range(len(probs)):
            t_buf[c, i] = pout["t"][i]
            qkd_buf[c, i] = pout["qkd"][i]
            oi_buf[c, i] = hout["o"][i]

    def recur_stages(c, gout, hout):
        rows = chunk_rows(c)
        ld = lambda ref: [ref[b, rows, hs(h)] for b, h in probs]
        gcol = [gcol_ref[b, rows, :] for b in range(bb)]
        n = len(probs)
        return (_gdn_state_stages(ld(gq_ref), ld(gk_ref), ld(gv_ref),
                                  [gcol[b][:, h:h + 1] for b, h in probs],
                                  [gcol[b][:, HEADS + h:HEADS + h + 1] for b, h in probs],
                                  [t_buf[c, i] for i in range(n)],
                                  [qkd_buf[c, i] for i in range(n)],
                                  [sg_ref[b, h] for b, h in probs], gout),
                _hgrn2_state_stages(ld(hq_ref), ld(hk_ref), ld(hv_ref), ld(hg_ref),
                                    [sh_ref[b, h] for b, h in probs],
                                    [oi_buf[c, i] for i in range(n)], hout))

    def store_recur(c, gout, hout):
        rows = chunk_rows(c)
        for i, (b, h) in enumerate(probs):
            sh_ref[b, h] = hout["s"][i]
            sg_ref[b, h] = gout["s"][i]
            y_ref[b, rows, hs(h)] = _head_norm(hout["o"][i], hw).astype(BF16)
            ys = slice(WIDTH + h * DH, WIDTH + (h + 1) * DH)
            y_ref[b, rows, ys] = (_head_norm(gout["o"][i], gw)
                                  * gz_ref[b, rows, hs(h)].astype(F32)).astype(BF16)

    def prep_only(c, bounded):
        pout, hout = {}, {}
        _run_interleaved(*prep_stages(c, pout, hout, bounded))
        store_prep(c, pout, hout)

    def recur_only(c):
        gout, hout = {}, {}
        _run_interleaved(*recur_stages(c, gout, hout))
        store_recur(c, gout, hout)

    def pipelined_step():
        prep_only(0, True)

        def body(c, carry):
            pout, phout, gout, hout = {}, {}, {}, {}
            gprep, hprep = prep_stages(c + 1, pout, phout, True)
            gstate, hstate = recur_stages(c, gout, hout)
            _run_interleaved(gprep, _spread(gstate, RECUR_GAP), _spread(hstate, RECUR_GAP),
                             hprep)
            store_prep(c + 1, pout, phout)
            store_recur(c, gout, hout)
            return carry

        lax.fori_loop(0, nchunks - 1, body, 0)
        recur_only(nchunks - 1)

    def plain_step():
        def body(c, carry):
            prep_only(c, False)
            recur_only(c)
            return carry

        lax.fori_loop(0, nchunks, body, 0)

    all_bounded = bounded_ref[0] == 1
    for c in range(1, nchunks):
        all_bounded = jnp.logical_and(all_bounded, bounded_ref[c] == 1)
    pl.when(all_bounded)(pipelined_step)
    pl.when(jnp.logical_not(all_bounded))(plain_step)


def _mixer_call(hq, hk, hv, hg, gq, gk, gv, gz, gcol, grow, sh0, sg0, hw, gw, *, bb, tt, chunk):
    B, T, _ = hq.shape
    assert B % bb == 0 and T % tt == 0 and tt % chunk == 0
    nchunks = tt // chunk
    tok_spec = lambda width: pl.BlockSpec((bb, tt, width), lambda i, j: (i, j, 0))
    const = lambda shape: pl.BlockSpec(shape, lambda i, j: (0,) * len(shape))
    return pl.pallas_call(
        functools.partial(_mixer_kernel, bb=bb, tt=tt, chunk=chunk),
        grid=(B // bb, T // tt),
        in_specs=[tok_spec(WIDTH), tok_spec(WIDTH), tok_spec(WIDTH), tok_spec(WIDTH),
                  tok_spec(WIDTH), tok_spec(WIDTH), tok_spec(WIDTH), tok_spec(WIDTH),
                  tok_spec(AB_PAD),
                  pl.BlockSpec((bb, nchunks, 2 * HEADS, chunk), lambda i, j: (i, j, 0, 0)),
                  const((HEADS, DH, DH)), const((HEADS, DH, DH)),
                  const((1, DH)), const((1, DH))],
        out_specs=tok_spec(2 * WIDTH),
        out_shape=jax.ShapeDtypeStruct((B, T, 2 * WIDTH), BF16),
        scratch_shapes=[pltpu.VMEM((bb, HEADS, DH, DH), F32),
                        pltpu.VMEM((bb, HEADS, DH, DH), F32),
                        pltpu.VMEM((nchunks, bb * HEADS, chunk, chunk), BF16),
                        pltpu.VMEM((nchunks, bb * HEADS, chunk, chunk), BF16),
                        pltpu.VMEM((nchunks, bb * HEADS, chunk, DH), F32),
                        pltpu.SMEM((nchunks,), jnp.int32)],
        compiler_params=pltpu.CompilerParams(
            dimension_semantics=("parallel", "arbitrary"), vmem_limit_bytes=VMEM_LIMIT),
        name="mixer",
    )(hq, hk, hv, hg, gq, gk, gv, gz, gcol, grow, sh0, sg0, hw, gw)


def _meta_state_kernel(hk_ref, hv_ref, hg_ref, gk_ref, gv_ref, gcol_ref, grow_ref,
                       sh_ref, sg_ref):
    gcol = gcol_ref[0]
    grow = grow_ref[0, 0]
    hs = lambda h: slice(h * DH, (h + 1) * DH)
    heads = range(HEADS)
    hk = [hk_ref[0, :, hs(h)] for h in heads]
    gk = [gk_ref[0, :, hs(h)] for h in heads]
    gam_c = [gcol[:, h:h + 1] for h in heads]
    beta_c = [gcol[:, HEADS + h:HEADS + h + 1] for h in heads]
    prep, hout, gout = {}, {}, {}
    _run_interleaved(
        _gdn_prep_stages(None, gk, gam_c, [grow[h:h + 1, :] for h in heads], beta_c, prep,
                         need_o=False),
        _hgrn2_state_stages(None, hk, [hv_ref[0, :, hs(h)] for h in heads],
                            [hg_ref[0, :, hs(h)] for h in heads], None, None, hout))
    _run_interleaved(
        _gdn_state_stages(None, gk, [gv_ref[0, :, hs(h)] for h in heads], gam_c, beta_c,
                          prep["t"], None, None, gout))
    for h in heads:
        sh_ref[h] = hout["s"][h]
        sg_ref[h] = gout["s"][h]


def _meta_state_call(hk, hv, hg, gk, gv, gcol, grow):
    state = jax.ShapeDtypeStruct((HEADS, DH, DH), F32)
    return pl.pallas_call(
        _meta_state_kernel,
        out_shape=[state, state],
        name="meta_state",
    )(hk, hv, hg, gk, gv, gcol, grow)


def _out_kernel(y_ref, x_ref, nw_ref, wz_ref, w_ref, fw_ref, o_ref):
    tm = x_ref.shape[0]
    rb = min(tm, OUT_ROW_BLOCK)
    blocks = [slice(r0, r0 + rb) for r0 in range(0, tm, rb)]
    gates = []
    for rows in blocks:
        x = x_ref[rows, :]
        ms = jnp.mean(x * x, axis=-1, keepdims=True)
        u = (x * lax.rsqrt(ms + EPS) * nw_ref[...]).astype(BF16)
        gates.append(_silu(_dot(u, wz_ref[...])))
    for rows, gate in zip(blocks, gates):
        yh = (y_ref[rows, 0:WIDTH].astype(F32) * gate).astype(BF16)
        h = (x_ref[rows, :] + _dot(yh, w_ref[0:WIDTH, :])
             + _dot(y_ref[rows, WIDTH:2 * WIDTH], w_ref[WIDTH:2 * WIDTH, :]))
        ms = jnp.mean(h * h, axis=-1, keepdims=True)
        o_ref[rows, :] = h * lax.rsqrt(ms + EPS) * fw_ref[...]


def _out_call(y, x, nw, wz, w, fw, *, tm):
    R = x.shape[0]
    assert R % tm == 0 and tm % min(tm, OUT_ROW_BLOCK) == 0
    row_spec = pl.BlockSpec((tm, D_MODEL), lambda i: (i, 0))
    const = lambda shape: pl.BlockSpec(shape, lambda i: (0, 0))
    return pl.pallas_call(
        _out_kernel,
        grid=(R // tm,),
        in_specs=[row_spec, row_spec, const((1, D_MODEL)), const((D_MODEL, WIDTH)),
                  const((2 * WIDTH, D_MODEL)), const((1, D_MODEL))],
        out_specs=row_spec,
        out_shape=jax.ShapeDtypeStruct((R, D_MODEL), F32),
        compiler_params=pltpu.CompilerParams(
            dimension_semantics=("parallel",), vmem_limit_bytes=VMEM_LIMIT),
        name="out",
    )(y, x, nw, wz, w, fw)


def kernel(x, meta_tokens, norm_w, w_in, conv_w, hg_lb_logits, hg_norm_w, gdn_A_log,
           gdn_dt_bias, gdn_norm_w, w_out, final_norm_w):
    B, T, D = x.shape
    assert D == D_MODEL and norm_w.shape[0] == 1 and meta_tokens.shape == (N_META, D_MODEL)
    assert w_in.shape == (1, D_MODEL, 8 * WIDTH + 2 * HEADS)

    w = jnp.pad(w_in[0], ((0, 0), (0, AB_PAD - 2 * HEADS))).astype(BF16)
    pad = lambda v: jnp.zeros((1, AB_PAD), F32).at[0, :HEADS].set(v.astype(F32))
    alog, dtb = pad(gdn_A_log[0]), pad(gdn_dt_bias[0])
    nw = norm_w[0].reshape(1, D_MODEL)
    cw = conv_w[0]
    proj = functools.partial(_proj_call, nw=nw, w=w, cw=cw, lbl=hg_lb_logits, alog=alog,
                             dtb=dtb)

    zero_tail = jnp.zeros((CONV_HIST, 3 * WIDTH), F32)
    m = proj(meta_tokens[None], tail=zero_tail, tm=N_META, chunk=N_META, emit_tail=True)
    _, mhk, mhv, mhg, _, mgk, mgv, _, mgcol, mgrow, tail = m
    sh0, sg0 = _meta_state_call(mhk, mhv, mhg, mgk, mgv, mgcol, mgrow)

    hq, hk, hv, hg, gq, gk, gv, gz, gcol, grow = proj(
        x, tail=tail, tm=PROJ_ROWS, chunk=CHUNK, emit_tail=False)
    y = _mixer_call(hq, hk, hv, hg, gq, gk, gv, gz, gcol, grow, sh0, sg0,
                    hg_norm_w[0].reshape(1, DH), gdn_norm_w[0].reshape(1, DH),
                    bb=MIXER_BATCH, tt=MIXER_ROWS, chunk=CHUNK)
    wz = w_in[0, :, 3 * WIDTH:4 * WIDTH].astype(BF16)
    out = _out_call(y.reshape(B * T, 2 * WIDTH), x.reshape(B * T, D_MODEL), nw, wz,
                    w_out[0].astype(BF16), final_norm_w.reshape(1, D_MODEL), tm=OUT_ROWS)
    return out.reshape(B, T, D_MODEL)
```

```python
import functools

import jax
import jax.numpy as jnp
from jax import lax
from jax.experimental import pallas as pl
from jax.experimental.pallas import tpu as pltpu

F32 = jnp.float32
BF16 = jnp.bfloat16

D_MODEL = 1024
N_META = 16
HEADS = 4
DH = 128
WIDTH = HEADS * DH
CONV_W = 4
EPS = 1e-6
AB_PAD = 128
IN_PAD = 8 * WIDTH + AB_PAD

CHUNK = 64
SUB = 16
SAFE_RANGE = 60.0
CONV_HIST = 8
CUM_ROWS = 256
ROW_BLOCK = 128
PROJ_ROWS = 1024
MIXER_BATCH = 8
MIXER_ROWS = 128
OUT_ROWS = 1024
OUT_ROW_BLOCK = 128
RECUR_GAP = 3
V7X_VMEM_BYTES = 64 * 1024 * 1024
VMEM_LIMIT = V7X_VMEM_BYTES * 7 // 8


def _dot(a, b):
    return jnp.dot(a, b, preferred_element_type=F32)


def _dot_nt(a, b):
    return lax.dot_general(a, b, (((1,), (1,)), ((), ())), preferred_element_type=F32)


def _dot_tn(a, b):
    return lax.dot_general(a, b, (((0,), (0,)), ((), ())), preferred_element_type=F32)


def _split3(x):
    x1 = x.astype(BF16)
    r = x - x1.astype(F32)
    x2 = r.astype(BF16)
    x3 = (r - x2.astype(F32)).astype(BF16)
    return x1, x2, x3


def _silu(x):
    return x * jax.nn.sigmoid(x)


def _softplus(x):
    return jnp.maximum(x, 0.0) + jnp.log1p(jnp.exp(-jnp.abs(x)))


def _chunk_cumsum(x, chunk):
    rows = x.shape[0]
    blk = min(rows, CUM_ROWS)
    r = lax.broadcasted_iota(jnp.int32, (blk, blk), 0)
    c = lax.broadcasted_iota(jnp.int32, (blk, blk), 1)
    tri = jnp.where((r // chunk == c // chunk) & (c <= r), 1.0, 0.0).astype(BF16)
    outs = []
    for i in range(rows // blk):
        parts = _split3(x[i * blk:(i + 1) * blk])
        outs.append(_dot(tri, parts[0]) + _dot(tri, parts[1]) + _dot(tri, parts[2]))
    return outs[0] if len(outs) == 1 else jnp.concatenate(outs, axis=0)


def _chunk_cumsum_lanes(x, chunk):
    n = x.shape[1]
    r = lax.broadcasted_iota(jnp.int32, (n, n), 0)
    c = lax.broadcasted_iota(jnp.int32, (n, n), 1)
    tri = jnp.where((r // chunk == c // chunk) & (r <= c), 1.0, 0.0).astype(BF16)
    parts = _split3(x)
    return _dot(parts[0], tri) + _dot(parts[1], tri) + _dot(parts[2], tri)


def _proj_kernel(x_ref, nw_ref, w_ref, cw_ref, lbl_ref, alog_ref, dtb_ref, tail_ref,
                 hq_ref, hk_ref, hv_ref, hg_ref, gq_ref, gk_ref, gv_ref, gz_ref,
                 gcol_ref, grow_ref, *rest, tm, chunk, emit_tail):
    if emit_tail:
        tail_out_ref, cbuf, logf_buf, ab_buf = rest
    else:
        cbuf, logf_buf, ab_buf = rest
    j = pl.program_id(1)

    @pl.when(j == 0)
    def _():
        cbuf[0:CONV_HIST, :] = tail_ref[...]

    lbl = lbl_ref[...]
    e = jnp.exp(lbl - jnp.max(lbl, axis=0, keepdims=True))
    lb = e[0:1] / jnp.sum(e, axis=0, keepdims=True)

    rb = min(tm, ROW_BLOCK)
    for r0 in range(0, tm, rb):
        rows = slice(r0, r0 + rb)
        x = x_ref[0, rows, :]
        ms = jnp.mean(x * x, axis=-1, keepdims=True)
        u = (x * lax.rsqrt(ms + EPS) * nw_ref[...]).astype(BF16)

        def proj(a, b, u=u):
            return _dot(u, w_ref[:, a:b])

        cbuf[CONV_HIST + r0:CONV_HIST + r0 + rb, :] = proj(4 * WIDTH, 7 * WIDTH)
        xe = cbuf[r0:r0 + CONV_HIST + rb, :]
        acc = cw_ref[0:1, :] * xe
        for t in range(1, CONV_W):
            acc = pltpu.roll(acc, 1, 0) + cw_ref[t:t + 1, :] * xe
        qkv = _silu(acc[CONV_HIST:CONV_HIST + rb])
        for h in range(HEADS):
            q = qkv[:, h * DH:(h + 1) * DH]
            k = qkv[:, WIDTH + h * DH:WIDTH + (h + 1) * DH]
            q = q * lax.rsqrt(jnp.sum(q * q, axis=-1, keepdims=True) + EPS) * (DH ** -0.5)
            k = k * lax.rsqrt(jnp.sum(k * k, axis=-1, keepdims=True) + EPS)
            gq_ref[0, rows, h * DH:(h + 1) * DH] = q.astype(BF16)
            gk_ref[0, rows, h * DH:(h + 1) * DH] = k.astype(BF16)
        gv_ref[0, rows, :] = qkv[:, 2 * WIDTH:3 * WIDTH].astype(BF16)
        gz_ref[0, rows, :] = _silu(proj(7 * WIDTH, 8 * WIDTH)).astype(BF16)

        hq_ref[0, rows, :] = _silu(proj(0, WIDTH)).astype(BF16)
        f = lb + (1.0 - lb) * jax.nn.sigmoid(proj(WIDTH, 2 * WIDTH))
        hk_ref[0, rows, :] = (1.0 - f).astype(BF16)
        logf_buf[rows, :] = jnp.log(f)
        hv_ref[0, rows, :] = proj(2 * WIDTH, 3 * WIDTH).astype(BF16)

        ab_buf[rows, :] = proj(8 * WIDTH, 8 * WIDTH + AB_PAD)

    for r0 in range(0, tm, rb):
        rows = slice(r0, r0 + rb)
        hg_ref[0, rows, :] = _chunk_cumsum(logf_buf[rows, :], chunk)
        p = ab_buf[rows, :]
        lane = lax.broadcasted_iota(jnp.int32, p.shape, 1)
        g = jnp.where(lane < HEADS, -jnp.exp(alog_ref[...]) * _softplus(p + dtb_ref[...]), 0.0)
        gam_rows = _chunk_cumsum_lanes(g.T[0:2 * HEADS, :], chunk)
        for c in range(rb // chunk):
            grow_ref[0, r0 // chunk + c] = gam_rows[:, c * chunk:(c + 1) * chunk]
        gam = jnp.concatenate(
            [gam_rows, jnp.zeros((AB_PAD - 2 * HEADS, rb), F32)], axis=0).T
        gcol_ref[0, rows, :] = jnp.where(
            lane < HEADS, gam, jnp.where(lane < 2 * HEADS, jax.nn.sigmoid(p), 0.0))

    hist = cbuf[tm:tm + CONV_HIST, :]
    cbuf[0:CONV_HIST, :] = hist
    if emit_tail:
        tail_out_ref[...] = hist


def _proj_call(x, nw, w, cw, lbl, alog, dtb, tail, *, tm, chunk, emit_tail):
    B, T, _ = x.shape
    assert T % tm == 0 and tm % min(tm, ROW_BLOCK) == 0 and min(tm, ROW_BLOCK) % chunk == 0
    nt = T // tm
    tok = lambda width, dt: jax.ShapeDtypeStruct((B, T, width), dt)
    tok_spec = lambda width: pl.BlockSpec((1, tm, width), lambda b, j: (b, j, 0))
    const = lambda shape: pl.BlockSpec(shape, lambda b, j: (0,) * len(shape))
    out_shape = [tok(WIDTH, BF16), tok(WIDTH, BF16), tok(WIDTH, BF16), tok(WIDTH, F32),
                 tok(WIDTH, BF16), tok(WIDTH, BF16), tok(WIDTH, BF16), tok(WIDTH, BF16),
                 tok(AB_PAD, F32),
                 jax.ShapeDtypeStruct((B, T // chunk, 2 * HEADS, chunk), F32)]
    out_specs = [tok_spec(WIDTH), tok_spec(WIDTH), tok_spec(WIDTH), tok_spec(WIDTH),
                 tok_spec(WIDTH), tok_spec(WIDTH), tok_spec(WIDTH), tok_spec(WIDTH),
                 tok_spec(AB_PAD),
                 pl.BlockSpec((1, tm // chunk, 2 * HEADS, chunk), lambda b, j: (b, j, 0, 0))]
    if emit_tail:
        out_shape.append(jax.ShapeDtypeStruct((CONV_HIST, 3 * WIDTH), F32))
        out_specs.append(const((CONV_HIST, 3 * WIDTH)))
    return pl.pallas_call(
        functools.partial(_proj_kernel, tm=tm, chunk=chunk, emit_tail=emit_tail),
        grid=(B, nt),
        in_specs=[tok_spec(D_MODEL), const((1, D_MODEL)),
                  pl.BlockSpec((D_MODEL, IN_PAD), lambda b, j: (0, 0),
                               pipeline_mode=pl.Buffered(1)),
                  const((CONV_W, 3 * WIDTH)), const(lbl.shape), const((1, AB_PAD)),
                  const((1, AB_PAD)), const((CONV_HIST, 3 * WIDTH))],
        out_specs=out_specs,
        out_shape=out_shape,
        scratch_shapes=[pltpu.VMEM((tm + CONV_HIST, 3 * WIDTH), F32),
                        pltpu.VMEM((tm, WIDTH), F32), pltpu.VMEM((tm, AB_PAD), F32)],
        compiler_params=pltpu.CompilerParams(
            dimension_semantics=("parallel", "arbitrary"), vmem_limit_bytes=VMEM_LIMIT),
        name="proj_meta" if emit_tail else "proj",
    )(x, nw, w, cw, lbl, alog, dtb, tail)


def _run_interleaved(*gens):
    live = list(gens)
    while live:
        for g in list(live):
            try:
                next(g)
            except StopIteration:
                live.remove(g)


def _scaled(x, decay):
    return x * decay.astype(BF16)


def _spread(gen, gap):
    for _ in gen:
        yield
        for _ in range(gap):
            yield


def _hgrn2_intra_stages(q, k, v, G, out, *, bounded):
    n = len(q)
    C = q[0].shape[0]
    row = lax.broadcasted_iota(jnp.int32, (C, C), 0)
    col = lax.broadcasted_iota(jnp.int32, (C, C), 1)
    if bounded:
        g_mid = [G[i][C // 2 - 1:C // 2, :] for i in range(n)]
        p = [_dot_nt(_scaled(q[i], jnp.exp(G[i] - g_mid[i])),
                     _scaled(k[i], jnp.exp(g_mid[i] - G[i]))) for i in range(n)]
        yield
        out["o"] = [_dot(jnp.where(row >= col, p[i], 0.0).astype(BF16), v[i]) for i in range(n)]
        return
    nb = C // SUB
    qf = [q[i].astype(F32) for i in range(n)]
    kf = [k[i].astype(F32) for i in range(n)]
    a = [jnp.zeros((C, C), F32) for _ in range(n)]
    g_end = [[G[i][(j + 1) * SUB - 1:(j + 1) * SUB, :] for j in range(nb)] for i in range(n)]
    if nb > 1:
        p = []
        for i in range(n):
            g_ref = jnp.concatenate(
                [jnp.broadcast_to(g_end[i][j], (SUB, DH)) for j in range(nb)], axis=0)
            kp = (kf[i] * jnp.exp(g_ref - G[i])).astype(BF16)
            qp = jnp.concatenate(
                [qf[i] * jnp.exp(jnp.minimum(G[i] - g_end[i][j], 0.0)) for j in range(nb - 1)],
                axis=0).astype(BF16)
            p.append(_dot_nt(qp, kp))
        yield
        for i in range(n):
            for j in range(nb - 1):
                m = (col // SUB == j) & (row // SUB > j)
                a[i] = jnp.where(m, p[i][j * C:(j + 1) * C], a[i])
    rowb = lax.broadcasted_iota(jnp.int32, (SUB, C), 0)
    colb = lax.broadcasted_iota(jnp.int32, (SUB, C), 1)
    o = []
    for i in range(n):
        for b in range(nb):
            gb = G[i][b * SUB:(b + 1) * SUB]
            qb = qf[i][b * SUB:(b + 1) * SUB]
            kb = kf[i][b * SUB:(b + 1) * SUB]
            d = jnp.zeros((SUB, C), F32)
            for s in range(SUB):
                es = jnp.exp(jnp.minimum(gb - gb[s:s + 1], 0.0))
                cs = jnp.sum(qb * es * kb[s:s + 1], axis=-1, keepdims=True)
                d = jnp.where((colb == b * SUB + s) & (rowb >= s), cs, d)
            if nb == 1:
                a[i] = d
            else:
                dfull = jnp.concatenate(
                    [d if r == b else jnp.zeros((SUB, C), F32) for r in range(nb)], axis=0)
                a[i] = jnp.where((row // SUB == b) & (col // SUB == b), dfull, a[i])
        o.append(_dot(a[i].astype(BF16), v[i]))
    out["o"] = o


def _hgrn2_state_stages(q, k, v, G, st, o_intra, out):
    n = len(k)
    C = k[0].shape[0]
    g_last = [G[i][C - 1:C, :] for i in range(n)]
    kk = [_scaled(k[i], jnp.exp(g_last[i] - G[i])) for i in range(n)]
    st_new = [_dot_tn(v[i], kk[i]) for i in range(n)]
    if st is not None:
        st_new = [st_new[i] + st[i] * jnp.exp(g_last[i]) for i in range(n)]
    out["s"] = st_new
    yield
    if o_intra is not None:
        out["o"] = [o_intra[i] + _dot_nt(_scaled(q[i], jnp.exp(G[i])), st[i].astype(BF16))
                    for i in range(n)]


def _gdn_prep_stages(q, k, gam_c, gam_r, beta_c, out, *, need_o):
    n = len(k)
    C = k[0].shape[0]
    rng = range(n)
    row = lax.broadcasted_iota(jnp.int32, (C, C), 0)
    col = lax.broadcasted_iota(jnp.int32, (C, C), 1)
    if need_o:
        pk = [_dot_nt(jnp.concatenate([q[i], k[i]], axis=0), k[i]) for i in rng]
        qk, kk = [x[:C] for x in pk], [x[C:] for x in pk]
    else:
        kk = [_dot_nt(k[i], k[i]) for i in rng]
    yield
    dec = [jnp.exp(jnp.minimum(gam_c[i] - gam_r[i], 0.0)) for i in rng]
    if need_o:
        out["qkd"] = [jnp.where(row >= col, qk[i] * dec[i], 0.0).astype(BF16) for i in rng]
    a = [jnp.where(row > col, beta_c[i] * kk[i] * dec[i], 0.0) for i in rng]
    eye = jnp.where(row == col, 1.0, 0.0)
    t = [eye - jnp.where(row // 2 == col // 2, a[i], 0.0) for i in rng]
    bs = 2
    while bs < C:
        lm = (row // (2 * bs) == col // (2 * bs)) & (row // bs > col // bs)
        tb = [t[i].astype(BF16) for i in rng]
        tl = [_dot(tb[i], jnp.where(lm, a[i], 0.0).astype(BF16)) for i in rng]
        yield
        t = [t[i] - _dot(tl[i].astype(BF16), tb[i]) for i in rng]
        yield
        bs *= 2
    out["t"] = [t[i].astype(BF16) for i in rng]


def _gdn_state_stages(q, k, v, gam_c, beta_c, t, qkd, s, out):
    n = len(k)
    C = k[0].shape[0]
    rng = range(n)
    vf = [v[i].astype(F32) for i in rng]
    qs = None
    if s is not None:
        egam = [jnp.exp(gam_c[i]) for i in rng]
        sb = [s[i].astype(BF16) for i in rng]
        if qkd is not None:
            qks = [_dot(jnp.concatenate([_scaled(q[i], egam[i]), _scaled(k[i], egam[i])],
                                        axis=0), sb[i]) for i in rng]
            qs, ks = [x[:C] for x in qks], [x[C:] for x in qks]
        else:
            ks = [_dot(_scaled(k[i], egam[i]), sb[i]) for i in rng]
        yield
        rhs = [beta_c[i] * (vf[i] - ks[i]) for i in rng]
    else:
        rhs = [beta_c[i] * vf[i] for i in rng]
    ub = [_dot(t[i], rhs[i].astype(BF16)).astype(BF16) for i in rng]
    yield
    g_last = [gam_c[i][C - 1:C, :] for i in rng]
    s_new = [_dot_tn(_scaled(k[i], jnp.exp(g_last[i] - gam_c[i])), ub[i]) for i in rng]
    if s is not None:
        s_new = [s_new[i] + s[i] * jnp.exp(g_last[i]) for i in rng]
    out["s"] = s_new
    if qkd is not None:
        o = [_dot(qkd[i], ub[i]) for i in rng]
        out["o"] = o if qs is None else [o[i] + qs[i] for i in rng]


def _head_norm(o, w):
    return o * lax.rsqrt(jnp.mean(o * o, axis=-1, keepdims=True) + EPS) * w


def _mixer_kernel(hq_ref, hk_ref, hv_ref, hg_ref, gq_ref, gk_ref, gv_ref, gz_ref,
                  gcol_ref, grow_ref, sh0_ref, sg0_ref, hw_ref, gw_ref,
                  y_ref, sh_ref, sg_ref, t_buf, qkd_buf, oi_buf, bounded_ref, *, bb, tt, chunk):
    j = pl.program_id(1)
    nchunks = tt // chunk

    @pl.when(j == 0)
    def _():
        for b in range(bb):
            sh_ref[b] = sh0_ref[...]
            sg_ref[b] = sg0_ref[...]

    for c in range(nchunks):
        span = None
        for b in range(bb):
            g_mid = hg_ref[b, c * chunk + chunk // 2 - 1:c * chunk + chunk // 2, :]
            g_last = hg_ref[b, (c + 1) * chunk - 1:(c + 1) * chunk, :]
            s = jnp.maximum(-g_mid, g_mid - g_last)
            span = s if span is None else jnp.maximum(span, s)
        bounded_ref[c] = jnp.where(jnp.max(span) <= SAFE_RANGE, 1, 0)

    hw = hw_ref[...]
    gw = gw_ref[...]
    probs = [(b, h) for b in range(bb) for h in range(HEADS)]
    hs = lambda h: slice(h * DH, (h + 1) * DH)

    def chunk_rows(c):
        return pl.ds(pl.multiple_of(c * chunk, chunk), chunk)

    def prep_stages(c, pout, hout, bounded):
        rows = chunk_rows(c)
        ld = lambda ref: [ref[b, rows, hs(h)] for b, h in probs]
        gcol = [gcol_ref[b, rows, :] for b in range(bb)]
        grow = [grow_ref[b, c] for b in range(bb)]
        return (_gdn_prep_stages(ld(gq_ref), ld(gk_ref),
                                 [gcol[b][:, h:h + 1] for b, h in probs],
                                 [grow[b][h:h + 1, :] for b, h in probs],
                                 [gcol[b][:, HEADS + h:HEADS + h + 1] for b, h in probs],
                                 pout, need_o=True),
                _hgrn2_intra_stages(ld(hq_ref), ld(hk_ref), ld(hv_ref), ld(hg_ref), hout,
                                    bounded=bounded))

    def store_prep(c, pout, hout):
        for i in range(len(probs)):
            t_buf[c, i] = pout["t"][i]
            qkd_buf[c, i] = pout["qkd"][i]
            oi_buf[c, i] = hout["o"][i]

    def recur_stages(c, gout, hout):
        rows = chunk_rows(c)
        ld = lambda ref: [ref[b, rows, hs(h)] for b, h in probs]
        gcol = [gcol_ref[b, rows, :] for b in range(bb)]
        n = len(probs)
        return (_gdn_state_stages(ld(gq_ref), ld(gk_ref), ld(gv_ref),
                                  [gcol[b][:, h:h + 1] for b, h in probs],
                                  [gcol[b][:, HEADS + h:HEADS + h + 1] for b, h in probs],
                                  [t_buf[c, i] for i in range(n)],
                                  [qkd_buf[c, i] for i in range(n)],
                                  [sg_ref[b, h] for b, h in probs], gout),
                _hgrn2_state_stages(ld(hq_ref), ld(hk_ref), ld(hv_ref), ld(hg_ref),
                                    [sh_ref[b, h] for b, h in probs],
                                    [oi_buf[c, i] for i in range(n)], hout))

    def store_recur(c, gout, hout):
        rows = chunk_rows(c)
        for i, (b, h) in enumerate(probs):
            sh_ref[b, h] = hout["s"][i]
            sg_ref[b, h] = gout["s"][i]
            y_ref[b, rows, hs(h)] = _head_norm(hout["o"][i], hw).astype(BF16)
            ys = slice(WIDTH + h * DH, WIDTH + (h + 1) * DH)
            y_ref[b, rows, ys] = (_head_norm(gout["o"][i], gw)
                                  * gz_ref[b, rows, hs(h)].astype(F32)).astype(BF16)

    def prep_only(c, bounded):
        pout, hout = {}, {}
        _run_interleaved(*prep_stages(c, pout, hout, bounded))
        store_prep(c, pout, hout)

    def recur_only(c):
        gout, hout = {}, {}
        _run_interleaved(*recur_stages(c, gout, hout))
        store_recur(c, gout, hout)

    def pipelined_step():
        prep_only(0, True)

        def body(c, carry):
            pout, phout, gout, hout = {}, {}, {}, {}
            gprep, hprep = prep_stages(c + 1, pout, phout, True)
            gstate, hstate = recur_stages(c, gout, hout)
            _run_interleaved(gprep, _spread(gstate, RECUR_GAP), _spread(hstate, RECUR_GAP),
                             hprep)
            store_prep(c + 1, pout, phout)
            store_recur(c, gout, hout)
            return carry

        lax.fori_loop(0, nchunks - 1, body, 0)
        recur_only(nchunks - 1)

    def plain_step():
        def body(c, carry):
            prep_only(c, False)
            recur_only(c)
            return carry

        lax.fori_loop(0, nchunks, body, 0)

    all_bounded = bounded_ref[0] == 1
    for c in range(1, nchunks):
        all_bounded = jnp.logical_and(all_bounded, bounded_ref[c] == 1)
    pl.when(all_bounded)(pipelined_step)
    pl.when(jnp.logical_not(all_bounded))(plain_step)


def _mixer_call(hq, hk, hv, hg, gq, gk, gv, gz, gcol, grow, sh0, sg0, hw, gw, *, bb, tt, chunk):
    B, T, _ = hq.shape
    assert B % bb == 0 and T % tt == 0 and tt % chunk == 0
    nchunks = tt // chunk
    tok_spec = lambda width: pl.BlockSpec((bb, tt, width), lambda i, j: (i, j, 0))
    const = lambda shape: pl.BlockSpec(shape, lambda i, j: (0,) * len(shape))
    return pl.pallas_call(
        functools.partial(_mixer_kernel, bb=bb, tt=tt, chunk=chunk),
        grid=(B // bb, T // tt),
        in_specs=[tok_spec(WIDTH), tok_spec(WIDTH), tok_spec(WIDTH), tok_spec(WIDTH),
                  tok_spec(WIDTH), tok_spec(WIDTH), tok_spec(WIDTH), tok_spec(WIDTH),
                  tok_spec(AB_PAD),
                  pl.BlockSpec((bb, nchunks, 2 * HEADS, chunk), lambda i, j: (i, j, 0, 0)),
                  const((HEADS, DH, DH)), const((HEADS, DH, DH)),
                  const((1, DH)), const((1, DH))],
        out_specs=tok_spec(2 * WIDTH),
        out_shape=jax.ShapeDtypeStruct((B, T, 2 * WIDTH), BF16),
        scratch_shapes=[pltpu.VMEM((bb, HEADS, DH, DH), F32),
                        pltpu.VMEM((bb, HEADS, DH, DH), F32),
                        pltpu.VMEM((nchunks, bb * HEADS, chunk, chunk), BF16),
                        pltpu.VMEM((nchunks, bb * HEADS, chunk, chunk), BF16),
                        pltpu.VMEM((nchunks, bb * HEADS, chunk, DH), F32),
                        pltpu.SMEM((nchunks,), jnp.int32)],
        compiler_params=pltpu.CompilerParams(
            dimension_semantics=("parallel", "arbitrary"), vmem_limit_bytes=VMEM_LIMIT),
        name="mixer",
    )(hq, hk, hv, hg, gq, gk, gv, gz, gcol, grow, sh0, sg0, hw, gw)


def _meta_state_kernel(hk_ref, hv_ref, hg_ref, gk_ref, gv_ref, gcol_ref, grow_ref,
                       sh_ref, sg_ref):
    gcol = gcol_ref[0]
    grow = grow_ref[0, 0]
    hs = lambda h: slice(h * DH, (h + 1) * DH)
    heads = range(HEADS)
    hk = [hk_ref[0, :, hs(h)] for h in heads]
    gk = [gk_ref[0, :, hs(h)] for h in heads]
    gam_c = [gcol[:, h:h + 1] for h in heads]
    beta_c = [gcol[:, HEADS + h:HEADS + h + 1] for h in heads]
    prep, hout, gout = {}, {}, {}
    _run_interleaved(
        _gdn_prep_stages(None, gk, gam_c, [grow[h:h + 1, :] for h in heads], beta_c, prep,
                         need_o=False),
        _hgrn2_state_stages(None, hk, [hv_ref[0, :, hs(h)] for h in heads],
                            [hg_ref[0, :, hs(h)] for h in heads], None, None, hout))
    _run_interleaved(
        _gdn_state_stages(None, gk, [gv_ref[0, :, hs(h)] for h in heads], gam_c, beta_c,
                          prep["t"], None, None, gout))
    for h in heads:
        sh_ref[h] = hout["s"][h]
        sg_ref[h] = gout["s"][h]


def _meta_state_call(hk, hv, hg, gk, gv, gcol, grow):
    state = jax.ShapeDtypeStruct((HEADS, DH, DH), F32)
    return pl.pallas_call(
        _meta_state_kernel,
        out_shape=[state, state],
        name="meta_state",
    )(hk, hv, hg, gk, gv, gcol, grow)


def _out_kernel(y_ref, x_ref, nw_ref, wz_ref, w_ref, fw_ref, o_ref):
    tm = x_ref.shape[0]
    rb = min(tm, OUT_ROW_BLOCK)
    blocks = [slice(r0, r0 + rb) for r0 in range(0, tm, rb)]
    gates = []
    for rows in blocks:
        x = x_ref[rows, :]
        ms = jnp.mean(x * x, axis=-1, keepdims=True)
        u = (x * lax.rsqrt(ms + EPS) * nw_ref[...]).astype(BF16)
        gates.append(_silu(_dot(u, wz_ref[...])))
    for rows, gate in zip(blocks, gates):
        yh = (y_ref[rows, 0:WIDTH].astype(F32) * gate).astype(BF16)
        h = (x_ref[rows, :] + _dot(yh, w_ref[0:WIDTH, :])
             + _dot(y_ref[rows, WIDTH:2 * WIDTH], w_ref[WIDTH:2 * WIDTH, :]))
        ms = jnp.mean(h * h, axis=-1, keepdims=True)
        o_ref[rows, :] = h * lax.rsqrt(ms + EPS) * fw_ref[...]


def _out_call(y, x, nw, wz, w, fw, *, tm):
    R = x.shape[0]
    assert R % tm == 0 and tm % min(tm, OUT_ROW_BLOCK) == 0
    row_spec = pl.BlockSpec((tm, D_MODEL), lambda i: (i, 0))
    const = lambda shape: pl.BlockSpec(shape, lambda i: (0, 0))
    return pl.pallas_call(
        _out_kernel,
        grid=(R // tm,),
        in_specs=[row_spec, row_spec, const((1, D_MODEL)), const((D_MODEL, WIDTH)),
                  const((2 * WIDTH, D_MODEL)), const((1, D_MODEL))],
        out_specs=row_spec,
        out_shape=jax.ShapeDtypeStruct((R, D_MODEL), F32),
        compiler_params=pltpu.CompilerParams(
            dimension_semantics=("parallel",), vmem_limit_bytes=VMEM_LIMIT),
        name="out",
    )(y, x, nw, wz, w, fw)


def kernel(x, meta_tokens, norm_w, w_in, conv_w, hg_lb_logits, hg_norm_w, gdn_A_log,
           gdn_dt_bias, gdn_norm_w, w_out, final_norm_w):
    B, T, D = x.shape
    assert D == D_MODEL and norm_w.shape[0] == 1 and meta_tokens.shape == (N_META, D_MODEL)
    assert w_in.shape == (1, D_MODEL, 8 * WIDTH + 2 * HEADS)

    w = jnp.pad(w_in[0], ((0, 0), (0, AB_PAD - 2 * HEADS))).astype(BF16)
    pad = lambda v: jnp.zeros((1, AB_PAD), F32).at[0, :HEADS].set(v.astype(F32))
    alog, dtb = pad(gdn_A_log[0]), pad(gdn_dt_bias[0])
    nw = norm_w[0].reshape(1, D_MODEL)
    cw = conv_w[0]
    proj = functools.partial(_proj_call, nw=nw, w=w, cw=cw, lbl=hg_lb_logits, alog=alog,
                             dtb=dtb)

    zero_tail = jnp.zeros((CONV_HIST, 3 * WIDTH), F32)
    m = proj(meta_tokens[None], tail=zero_tail, tm=N_META, chunk=N_META, emit_tail=True)
    _, mhk, mhv, mhg, _, mgk, mgv, _, mgcol, mgrow, tail = m
    sh0, sg0 = _meta_state_call(mhk, mhv, mhg, mgk, mgv, mgcol, mgrow)

    hq, hk, hv, hg, gq, gk, gv, gz, gcol, grow = proj(
        x, tail=tail, tm=PROJ_ROWS, chunk=CHUNK, emit_tail=False)
    y = _mixer_call(hq, hk, hv, hg, gq, gk, gv, gz, gcol, grow, sh0, sg0,
                    hg_norm_w[0].reshape(1, DH), gdn_norm_w[0].reshape(1, DH),
                    bb=MIXER_BATCH, tt=MIXER_ROWS, chunk=CHUNK)
    wz = w_in[0, :, 3 * WIDTH:4 * WIDTH].astype(BF16)
    out = _out_call(y.reshape(B * T, 2 * WIDTH), x.reshape(B * T, D_MODEL), nw, wz,
                    w_out[0].astype(BF16), final_norm_w.reshape(1, D_MODEL), tm=OUT_ROWS)
    return out.reshape(B, T, D_MODEL)
```

```python
import functools

import jax
import jax.numpy as jnp
from jax import lax
from jax.experimental import pallas as pl
from jax.experimental.pallas import tpu as pltpu

F32 = jnp.float32
BF16 = jnp.bfloat16

D_MODEL = 1024
N_META = 16
HEADS = 4
DH = 128
WIDTH = HEADS * DH
CONV_W = 4
EPS = 1e-6
AB_PAD = 128
IN_PAD = 8 * WIDTH + AB_PAD

CHUNK = 64
SUB = 16
SAFE_RANGE = 60.0
CONV_HIST = 8
CUM_ROWS = 256
ROW_BLOCK = 128
PROJ_ROWS = 1024
MIXER_BATCH = 8
MIXER_ROWS = 128
OUT_ROWS = 2048
OUT_ROW_BLOCK = 256
RECUR_GAP = 3
V7X_VMEM_BYTES = 64 * 1024 * 1024
VMEM_LIMIT = V7X_VMEM_BYTES * 7 // 8


def _dot(a, b):
    return jnp.dot(a, b, preferred_element_type=F32)


def _dot_nt(a, b):
    return lax.dot_general(a, b, (((1,), (1,)), ((), ())), preferred_element_type=F32)


def _dot_tn(a, b):
    return lax.dot_general(a, b, (((0,), (0,)), ((), ())), preferred_element_type=F32)


def _split3(x):
    x1 = x.astype(BF16)
    r = x - x1.astype(F32)
    x2 = r.astype(BF16)
    x3 = (r - x2.astype(F32)).astype(BF16)
    return x1, x2, x3


def _silu(x):
    return x * jax.nn.sigmoid(x)


def _softplus(x):
    return jnp.maximum(x, 0.0) + jnp.log1p(jnp.exp(-jnp.abs(x)))


def _chunk_cumsum(x, chunk):
    rows = x.shape[0]
    blk = min(rows, CUM_ROWS)
    r = lax.broadcasted_iota(jnp.int32, (blk, blk), 0)
    c = lax.broadcasted_iota(jnp.int32, (blk, blk), 1)
    tri = jnp.where((r // chunk == c // chunk) & (c <= r), 1.0, 0.0).astype(BF16)
    outs = []
    for i in range(rows // blk):
        parts = _split3(x[i * blk:(i + 1) * blk])
        outs.append(_dot(tri, parts[0]) + _dot(tri, parts[1]) + _dot(tri, parts[2]))
    return outs[0] if len(outs) == 1 else jnp.concatenate(outs, axis=0)


def _chunk_cumsum_lanes(x, chunk):
    n = x.shape[1]
    r = lax.broadcasted_iota(jnp.int32, (n, n), 0)
    c = lax.broadcasted_iota(jnp.int32, (n, n), 1)
    tri = jnp.where((r // chunk == c // chunk) & (r <= c), 1.0, 0.0).astype(BF16)
    parts = _split3(x)
    return _dot(parts[0], tri) + _dot(parts[1], tri) + _dot(parts[2], tri)


def _proj_kernel(x_ref, nw_ref, w_ref, cw_ref, lbl_ref, alog_ref, dtb_ref, tail_ref,
                 hq_ref, hk_ref, hv_ref, hg_ref, gq_ref, gk_ref, gv_ref, gz_ref,
                 gcol_ref, grow_ref, *rest, tm, chunk, emit_tail):
    if emit_tail:
        tail_out_ref, cbuf, logf_buf, ab_buf = rest
    else:
        cbuf, logf_buf, ab_buf = rest
    j = pl.program_id(1)

    @pl.when(j == 0)
    def _():
        cbuf[0:CONV_HIST, :] = tail_ref[...]

    lbl = lbl_ref[...]
    e = jnp.exp(lbl - jnp.max(lbl, axis=0, keepdims=True))
    lb = e[0:1] / jnp.sum(e, axis=0, keepdims=True)

    rb = min(tm, ROW_BLOCK)
    for r0 in range(0, tm, rb):
        rows = slice(r0, r0 + rb)
        x = x_ref[0, rows, :]
        ms = jnp.mean(x * x, axis=-1, keepdims=True)
        u = (x * lax.rsqrt(ms + EPS) * nw_ref[...]).astype(BF16)

        def proj(a, b, u=u):
            return _dot(u, w_ref[:, a:b])

        cbuf[CONV_HIST + r0:CONV_HIST + r0 + rb, :] = proj(4 * WIDTH, 7 * WIDTH)
        xe = cbuf[r0:r0 + CONV_HIST + rb, :]
        acc = cw_ref[0:1, :] * xe
        for t in range(1, CONV_W):
            acc = pltpu.roll(acc, 1, 0) + cw_ref[t:t + 1, :] * xe
        qkv = _silu(acc[CONV_HIST:CONV_HIST + rb])
        for h in range(HEADS):
            q = qkv[:, h * DH:(h + 1) * DH]
            k = qkv[:, WIDTH + h * DH:WIDTH + (h + 1) * DH]
            q = q * lax.rsqrt(jnp.sum(q * q, axis=-1, keepdims=True) + EPS) * (DH ** -0.5)
            k = k * lax.rsqrt(jnp.sum(k * k, axis=-1, keepdims=True) + EPS)
            gq_ref[0, rows, h * DH:(h + 1) * DH] = q.astype(BF16)
            gk_ref[0, rows, h * DH:(h + 1) * DH] = k.astype(BF16)
        gv_ref[0, rows, :] = qkv[:, 2 * WIDTH:3 * WIDTH].astype(BF16)
        gz_ref[0, rows, :] = _silu(proj(7 * WIDTH, 8 * WIDTH)).astype(BF16)

        hq_ref[0, rows, :] = _silu(proj(0, WIDTH)).astype(BF16)
        f = lb + (1.0 - lb) * jax.nn.sigmoid(proj(WIDTH, 2 * WIDTH))
        hk_ref[0, rows, :] = (1.0 - f).astype(BF16)
        logf_buf[rows, :] = jnp.log(f)
        hv_ref[0, rows, :] = proj(2 * WIDTH, 3 * WIDTH).astype(BF16)

        ab_buf[rows, :] = proj(8 * WIDTH, 8 * WIDTH + AB_PAD)

    for r0 in range(0, tm, rb):
        rows = slice(r0, r0 + rb)
        hg_ref[0, rows, :] = _chunk_cumsum(logf_buf[rows, :], chunk)
        p = ab_buf[rows, :]
        lane = lax.broadcasted_iota(jnp.int32, p.shape, 1)
        g = jnp.where(lane < HEADS, -jnp.exp(alog_ref[...]) * _softplus(p + dtb_ref[...]), 0.0)
        gam_rows = _chunk_cumsum_lanes(g.T[0:2 * HEADS, :], chunk)
        for c in range(rb // chunk):
            grow_ref[0, r0 // chunk + c] = gam_rows[:, c * chunk:(c + 1) * chunk]
        gam = jnp.concatenate(
            [gam_rows, jnp.zeros((AB_PAD - 2 * HEADS, rb), F32)], axis=0).T
        gcol_ref[0, rows, :] = jnp.where(
            lane < HEADS, gam, jnp.where(lane < 2 * HEADS, jax.nn.sigmoid(p), 0.0))

    hist = cbuf[tm:tm + CONV_HIST, :]
    cbuf[0:CONV_HIST, :] = hist
    if emit_tail:
        tail_out_ref[...] = hist


def _proj_call(x, nw, w, cw, lbl, alog, dtb, tail, *, tm, chunk, emit_tail):
    B, T, _ = x.shape
    assert T % tm == 0 and tm % min(tm, ROW_BLOCK) == 0 and min(tm, ROW_BLOCK) % chunk == 0
    nt = T // tm
    tok = lambda width, dt: jax.ShapeDtypeStruct((B, T, width), dt)
    tok_spec = lambda width: pl.BlockSpec((1, tm, width), lambda b, j: (b, j, 0))
    const = lambda shape: pl.BlockSpec(shape, lambda b, j: (0,) * len(shape))
    out_shape = [tok(WIDTH, BF16), tok(WIDTH, BF16), tok(WIDTH, BF16), tok(WIDTH, F32),
                 tok(WIDTH, BF16), tok(WIDTH, BF16), tok(WIDTH, BF16), tok(WIDTH, BF16),
                 tok(AB_PAD, F32),
                 jax.ShapeDtypeStruct((B, T // chunk, 2 * HEADS, chunk), F32)]
    out_specs = [tok_spec(WIDTH), tok_spec(WIDTH), tok_spec(WIDTH), tok_spec(WIDTH),
                 tok_spec(WIDTH), tok_spec(WIDTH), tok_spec(WIDTH), tok_spec(WIDTH),
                 tok_spec(AB_PAD),
                 pl.BlockSpec((1, tm // chunk, 2 * HEADS, chunk), lambda b, j: (b, j, 0, 0))]
    if emit_tail:
        out_shape.append(jax.ShapeDtypeStruct((CONV_HIST, 3 * WIDTH), F32))
        out_specs.append(const((CONV_HIST, 3 * WIDTH)))
    return pl.pallas_call(
        functools.partial(_proj_kernel, tm=tm, chunk=chunk, emit_tail=emit_tail),
        grid=(B, nt),
        in_specs=[tok_spec(D_MODEL), const((1, D_MODEL)),
                  pl.BlockSpec((D_MODEL, IN_PAD), lambda b, j: (0, 0),
                               pipeline_mode=pl.Buffered(1)),
                  const((CONV_W, 3 * WIDTH)), const(lbl.shape), const((1, AB_PAD)),
                  const((1, AB_PAD)), const((CONV_HIST, 3 * WIDTH))],
        out_specs=out_specs,
        out_shape=out_shape,
        scratch_shapes=[pltpu.VMEM((tm + CONV_HIST, 3 * WIDTH), F32),
                        pltpu.VMEM((tm, WIDTH), F32), pltpu.VMEM((tm, AB_PAD), F32)],
        compiler_params=pltpu.CompilerParams(
            dimension_semantics=("parallel", "arbitrary"), vmem_limit_bytes=VMEM_LIMIT),
        name="proj_meta" if emit_tail else "proj",
    )(x, nw, w, cw, lbl, alog, dtb, tail)


def _run_interleaved(*gens):
    live = list(gens)
    while live:
        for g in list(live):
            try:
                next(g)
            except StopIteration:
                live.remove(g)


def _scaled(x, decay):
    return x * decay.astype(BF16)


def _spread(gen, gap):
    for _ in gen:
        yield
        for _ in range(gap):
            yield


def _hgrn2_intra_stages(q, k, v, G, out, *, bounded):
    n = len(q)
    C = q[0].shape[0]
    row = lax.broadcasted_iota(jnp.int32, (C, C), 0)
    col = lax.broadcasted_iota(jnp.int32, (C, C), 1)
    if bounded:
        g_mid = [G[i][C // 2 - 1:C // 2, :] for i in range(n)]
        p = [_dot_nt(_scaled(q[i], jnp.exp(G[i] - g_mid[i])),
                     _scaled(k[i], jnp.exp(g_mid[i] - G[i]))) for i in range(n)]
        yield
        out["o"] = [_dot(jnp.where(row >= col, p[i], 0.0).astype(BF16), v[i]) for i in range(n)]
        return
    nb = C // SUB
    qf = [q[i].astype(F32) for i in range(n)]
    kf = [k[i].astype(F32) for i in range(n)]
    a = [jnp.zeros((C, C), F32) for _ in range(n)]
    g_end = [[G[i][(j + 1) * SUB - 1:(j + 1) * SUB, :] for j in range(nb)] for i in range(n)]
    if nb > 1:
        p = []
        for i in range(n):
            g_ref = jnp.concatenate(
                [jnp.broadcast_to(g_end[i][j], (SUB, DH)) for j in range(nb)], axis=0)
            kp = (kf[i] * jnp.exp(g_ref - G[i])).astype(BF16)
            qp = jnp.concatenate(
                [qf[i] * jnp.exp(jnp.minimum(G[i] - g_end[i][j], 0.0)) for j in range(nb - 1)],
                axis=0).astype(BF16)
            p.append(_dot_nt(qp, kp))
        yield
        for i in range(n):
            for j in range(nb - 1):
                m = (col // SUB == j) & (row // SUB > j)
                a[i] = jnp.where(m, p[i][j * C:(j + 1) * C], a[i])
    rowb = lax.broadcasted_iota(jnp.int32, (SUB, C), 0)
    colb = lax.broadcasted_iota(jnp.int32, (SUB, C), 1)
    o = []
    for i in range(n):
        for b in range(nb):
            gb = G[i][b * SUB:(b + 1) * SUB]
            qb = qf[i][b * SUB:(b + 1) * SUB]
            kb = kf[i][b * SUB:(b + 1) * SUB]
            d = jnp.zeros((SUB, C), F32)
            for s in range(SUB):
                es = jnp.exp(jnp.minimum(gb - gb[s:s + 1], 0.0))
                cs = jnp.sum(qb * es * kb[s:s + 1], axis=-1, keepdims=True)
                d = jnp.where((colb == b * SUB + s) & (rowb >= s), cs, d)
            if nb == 1:
                a[i] = d
            else:
                dfull = jnp.concatenate(
                    [d if r == b else jnp.zeros((SUB, C), F32) for r in range(nb)], axis=0)
                a[i] = jnp.where((row // SUB == b) & (col // SUB == b), dfull, a[i])
        o.append(_dot(a[i].astype(BF16), v[i]))
    out["o"] = o


def _hgrn2_state_stages(q, k, v, G, st, o_intra, out):
    n = len(k)
    C = k[0].shape[0]
    g_last = [G[i][C - 1:C, :] for i in range(n)]
    kk = [_scaled(k[i], jnp.exp(g_last[i] - G[i])) for i in range(n)]
    st_new = [_dot_tn(v[i], kk[i]) for i in range(n)]
    if st is not None:
        st_new = [st_new[i] + st[i] * jnp.exp(g_last[i]) for i in range(n)]
    out["s"] = st_new
    yield
    if o_intra is not None:
        out["o"] = [o_intra[i] + _dot_nt(_scaled(q[i], jnp.exp(G[i])), st[i].astype(BF16))
                    for i in range(n)]


def _gdn_prep_stages(q, k, gam_c, gam_r, beta_c, out, *, need_o):
    n = len(k)
    C = k[0].shape[0]
    rng = range(n)
    row = lax.broadcasted_iota(jnp.int32, (C, C), 0)
    col = lax.broadcasted_iota(jnp.int32, (C, C), 1)
    if need_o:
        pk = [_dot_nt(jnp.concatenate([q[i], k[i]], axis=0), k[i]) for i in rng]
        qk, kk = [x[:C] for x in pk], [x[C:] for x in pk]
    else:
        kk = [_dot_nt(k[i], k[i]) for i in rng]
    yield
    dec = [jnp.exp(jnp.minimum(gam_c[i] - gam_r[i], 0.0)) for i in rng]
    if need_o:
        out["qkd"] = [jnp.where(row >= col, qk[i] * dec[i], 0.0).astype(BF16) for i in rng]
    a = [jnp.where(row > col, beta_c[i] * kk[i] * dec[i], 0.0) for i in rng]
    eye = jnp.where(row == col, 1.0, 0.0)
    t = [eye - jnp.where(row // 2 == col // 2, a[i], 0.0) for i in rng]
    bs = 2
    while bs < C:
        lm = (row // (2 * bs) == col // (2 * bs)) & (row // bs > col // bs)
        tb = [t[i].astype(BF16) for i in rng]
        tl = [_dot(tb[i], jnp.where(lm, a[i], 0.0).astype(BF16)) for i in rng]
        yield
        t = [t[i] - _dot(tl[i].astype(BF16), tb[i]) for i in rng]
        yield
        bs *= 2
    out["t"] = [t[i].astype(BF16) for i in rng]


def _gdn_state_stages(q, k, v, gam_c, beta_c, t, qkd, s, out):
    n = len(k)
    C = k[0].shape[0]
    rng = range(n)
    vf = [v[i].astype(F32) for i in rng]
    qs = None
    if s is not None:
        egam = [jnp.exp(gam_c[i]) for i in rng]
        sb = [s[i].astype(BF16) for i in rng]
        if qkd is not None:
            qks = [_dot(jnp.concatenate([_scaled(q[i], egam[i]), _scaled(k[i], egam[i])],
                                        axis=0), sb[i]) for i in rng]
            qs, ks = [x[:C] for x in qks], [x[C:] for x in qks]
        else:
            ks = [_dot(_scaled(k[i], egam[i]), sb[i]) for i in rng]
        yield
        rhs = [beta_c[i] * (vf[i] - ks[i]) for i in rng]
    else:
        rhs = [beta_c[i] * vf[i] for i in rng]
    ub = [_dot(t[i], rhs[i].astype(BF16)).astype(BF16) for i in rng]
    yield
    g_last = [gam_c[i][C - 1:C, :] for i in rng]
    s_new = [_dot_tn(_scaled(k[i], jnp.exp(g_last[i] - gam_c[i])), ub[i]) for i in rng]
    if s is not None:
        s_new = [s_new[i] + s[i] * jnp.exp(g_last[i]) for i in rng]
    out["s"] = s_new
    if qkd is not None:
        o = [_dot(qkd[i], ub[i]) for i in rng]
        out["o"] = o if qs is None else [o[i] + qs[i] for i in rng]


def _head_norm(o, w):
    return o * lax.rsqrt(jnp.mean(o * o, axis=-1, keepdims=True) + EPS) * w


def _mixer_kernel(hq_ref, hk_ref, hv_ref, hg_ref, gq_ref, gk_ref, gv_ref, gz_ref,
                  gcol_ref, grow_ref, sh0_ref, sg0_ref, hw_ref, gw_ref,
                  y_ref, sh_ref, sg_ref, t_buf, qkd_buf, oi_buf, bounded_ref, *, bb, tt, chunk):
    j = pl.program_id(1)
    nchunks = tt // chunk

    @pl.when(j == 0)
    def _():
        for b in range(bb):
            sh_ref[b] = sh0_ref[...]
            sg_ref[b] = sg0_ref[...]

    for c in range(nchunks):
        span = None
        for b in range(bb):
            g_mid = hg_ref[b, c * chunk + chunk // 2 - 1:c * chunk + chunk // 2, :]
            g_last = hg_ref[b, (c + 1) * chunk - 1:(c + 1) * chunk, :]
            s = jnp.maximum(-g_mid, g_mid - g_last)
            span = s if span is None else jnp.maximum(span, s)
        bounded_ref[c] = jnp.where(jnp.max(span) <= SAFE_RANGE, 1, 0)

    hw = hw_ref[...]
    gw = gw_ref[...]
    probs = [(b, h) for b in range(bb) for h in range(HEADS)]
    hs = lambda h: slice(h * DH, (h + 1) * DH)

    def chunk_rows(c):
        return pl.ds(pl.multiple_of(c * chunk, chunk), chunk)

    def prep_stages(c, pout, hout, bounded):
        rows = chunk_rows(c)
        ld = lambda ref: [ref[b, rows, hs(h)] for b, h in probs]
        gcol = [gcol_ref[b, rows, :] for b in range(bb)]
        grow = [grow_ref[b, c] for b in range(bb)]
        return (_gdn_prep_stages(ld(gq_ref), ld(gk_ref),
                                 [gcol[b][:, h:h + 1] for b, h in probs],
                                 [grow[b][h:h + 1, :] for b, h in probs],
                                 [gcol[b][:, HEADS + h:HEADS + h + 1] for b, h in probs],
                                 pout, need_o=True),
                _hgrn2_intra_stages(ld(hq_ref), ld(hk_ref), ld(hv_ref), ld(hg_ref), hout,
                                    bounded=bounded))

    def store_prep(c, pout, hout):
        for i in range(len(probs)):
            t_buf[c, i] = pout["t"][i]
            qkd_buf[c, i] = pout["qkd"][i]
            oi_buf[c, i] = hout["o"][i]

    def recur_stages(c, gout, hout):
        rows = chunk_rows(c)
        ld = lambda ref: [ref[b, rows, hs(h)] for b, h in probs]
        gcol = [gcol_ref[b, rows, :] for b in range(bb)]
        n = len(probs)
        return (_gdn_state_stages(ld(gq_ref), ld(gk_ref), ld(gv_ref),
                                  [gcol[b][:, h:h + 1] for b, h in probs],
                                  [gcol[b][:, HEADS + h:HEADS + h + 1] for b, h in probs],
                                  [t_buf[c, i] for i in range(n)],
                                  [qkd_buf[c, i] for i in range(n)],
                                  [sg_ref[b, h] for b, h in probs], gout),
                _hgrn2_state_stages(ld(hq_ref), ld(hk_ref), ld(hv_ref), ld(hg_ref),
                                    [sh_ref[b, h] for b, h in probs],
                                    [oi_buf[c, i] for i in range(n)], hout))

    def store_recur(c, gout, hout):
        rows = chunk_rows(c)
        for i, (b, h) in enumerate(probs):
            sh_ref[b, h] = hout["s"][i]
            sg_ref[b, h] = gout["s"][i]
            y_ref[b, rows, hs(h)] = _head_norm(hout["o"][i], hw).astype(BF16)
            ys = slice(WIDTH + h * DH, WIDTH + (h + 1) * DH)
            y_ref[b, rows, ys] = (_head_norm(gout["o"][i], gw)
                                  * gz_ref[b, rows, hs(h)].astype(F32)).astype(BF16)

    def prep_only(c, bounded):
        pout, hout = {}, {}
        _run_interleaved(*prep_stages(c, pout, hout, bounded))
        store_prep(c, pout, hout)

    def recur_only(c):
        gout, hout = {}, {}
        _run_interleaved(*recur_stages(c, gout, hout))
        store_recur(c, gout, hout)

    def pipelined_step():
        prep_only(0, True)

        def body(c, carry):
            pout, phout, gout, hout = {}, {}, {}, {}
            gprep, hprep = prep_stages(c + 1, pout, phout, True)
            gstate, hstate = recur_stages(c, gout, hout)
            _run_interleaved(gprep, _spread(gstate, RECUR_GAP), _spread(hstate, RECUR_GAP),
                             hprep)
            store_prep(c + 1, pout, phout)
            store_recur(c, gout, hout)
            return carry

        lax.fori_loop(0, nchunks - 1, body, 0)
        recur_only(nchunks - 1)

    def plain_step():
        def body(c, carry):
            prep_only(c, False)
            recur_only(c)
            return carry

        lax.fori_loop(0, nchunks, body, 0)

    all_bounded = bounded_ref[0] == 1
    for c in range(1, nchunks):
        all_bounded = jnp.logical_and(all_bounded, bounded_ref[c] == 1)
    pl.when(all_bounded)(pipelined_step)
    pl.when(jnp.logical_not(all_bounded))(plain_step)


def _mixer_call(hq, hk, hv, hg, gq, gk, gv, gz, gcol, grow, sh0, sg0, hw, gw, *, bb, tt, chunk):
    B, T, _ = hq.shape
    assert B % bb == 0 and T % tt == 0 and tt % chunk == 0
    nchunks = tt // chunk
    tok_spec = lambda width: pl.BlockSpec((bb, tt, width), lambda i, j: (i, j, 0))
    const = lambda shape: pl.BlockSpec(shape, lambda i, j: (0,) * len(shape))
    return pl.pallas_call(
        functools.partial(_mixer_kernel, bb=bb, tt=tt, chunk=chunk),
        grid=(B // bb, T // tt),
        in_specs=[tok_spec(WIDTH), tok_spec(WIDTH), tok_spec(WIDTH), tok_spec(WIDTH),
                  tok_spec(WIDTH), tok_spec(WIDTH), tok_spec(WIDTH), tok_spec(WIDTH),
                  tok_spec(AB_PAD),
                  pl.BlockSpec((bb, nchunks, 2 * HEADS, chunk), lambda i, j: (i, j, 0, 0)),
                  const((HEADS, DH, DH)), const((HEADS, DH, DH)),
                  const((1, DH)), const((1, DH))],
        out_specs=tok_spec(2 * WIDTH),
        out_shape=jax.ShapeDtypeStruct((B, T, 2 * WIDTH), BF16),
        scratch_shapes=[pltpu.VMEM((bb, HEADS, DH, DH), F32),
                        pltpu.VMEM((bb, HEADS, DH, DH), F32),
                        pltpu.VMEM((nchunks, bb * HEADS, chunk, chunk), BF16),
                        pltpu.VMEM((nchunks, bb * HEADS, chunk, chunk), BF16),
                        pltpu.VMEM((nchunks, bb * HEADS, chunk, DH), F32),
                        pltpu.SMEM((nchunks,), jnp.int32)],
        compiler_params=pltpu.CompilerParams(
            dimension_semantics=("parallel", "arbitrary"), vmem_limit_bytes=VMEM_LIMIT),
        name="mixer",
    )(hq, hk, hv, hg, gq, gk, gv, gz, gcol, grow, sh0, sg0, hw, gw)


def _meta_state_kernel(hk_ref, hv_ref, hg_ref, gk_ref, gv_ref, gcol_ref, grow_ref,
                       sh_ref, sg_ref):
    gcol = gcol_ref[0]
    grow = grow_ref[0, 0]
    hs = lambda h: slice(h * DH, (h + 1) * DH)
    heads = range(HEADS)
    hk = [hk_ref[0, :, hs(h)] for h in heads]
    gk = [gk_ref[0, :, hs(h)] for h in heads]
    gam_c = [gcol[:, h:h + 1] for h in heads]
    beta_c = [gcol[:, HEADS + h:HEADS + h + 1] for h in heads]
    prep, hout, gout = {}, {}, {}
    _run_interleaved(
        _gdn_prep_stages(None, gk, gam_c, [grow[h:h + 1, :] for h in heads], beta_c, prep,
                         need_o=False),
        _hgrn2_state_stages(None, hk, [hv_ref[0, :, hs(h)] for h in heads],
                            [hg_ref[0, :, hs(h)] for h in heads], None, None, hout))
    _run_interleaved(
        _gdn_state_stages(None, gk, [gv_ref[0, :, hs(h)] for h in heads], gam_c, beta_c,
                          prep["t"], None, None, gout))
    for h in heads:
        sh_ref[h] = hout["s"][h]
        sg_ref[h] = gout["s"][h]


def _meta_state_call(hk, hv, hg, gk, gv, gcol, grow):
    state = jax.ShapeDtypeStruct((HEADS, DH, DH), F32)
    return pl.pallas_call(
        _meta_state_kernel,
        out_shape=[state, state],
        name="meta_state",
    )(hk, hv, hg, gk, gv, gcol, grow)


def _out_kernel(y_ref, x_ref, nw_ref, wz_ref, w_ref, fw_ref, o_ref):
    tm = x_ref.shape[0]
    rb = min(tm, OUT_ROW_BLOCK)
    blocks = [slice(r0, r0 + rb) for r0 in range(0, tm, rb)]
    gates = []
    for rows in blocks:
        x = x_ref[rows, :]
        ms = jnp.mean(x * x, axis=-1, keepdims=True)
        u = (x * lax.rsqrt(ms + EPS) * nw_ref[...]).astype(BF16)
        gates.append(_silu(_dot(u, wz_ref[...])))
    for rows, gate in zip(blocks, gates):
        yh = (y_ref[rows, 0:WIDTH].astype(F32) * gate).astype(BF16)
        h = (x_ref[rows, :] + _dot(yh, w_ref[0:WIDTH, :])
             + _dot(y_ref[rows, WIDTH:2 * WIDTH], w_ref[WIDTH:2 * WIDTH, :]))
        ms = jnp.mean(h * h, axis=-1, keepdims=True)
        o_ref[rows, :] = h * lax.rsqrt(ms + EPS) * fw_ref[...]


def _out_call(y, x, nw, wz, w, fw, *, tm):
    R = x.shape[0]
    assert R % tm == 0 and tm % min(tm, OUT_ROW_BLOCK) == 0
    row_spec = pl.BlockSpec((tm, D_MODEL), lambda i: (i, 0))
    const = lambda shape: pl.BlockSpec(shape, lambda i: (0, 0))
    return pl.pallas_call(
        _out_kernel,
        grid=(R // tm,),
        in_specs=[row_spec, row_spec, const((1, D_MODEL)), const((D_MODEL, WIDTH)),
                  const((2 * WIDTH, D_MODEL)), const((1, D_MODEL))],
        out_specs=row_spec,
        out_shape=jax.ShapeDtypeStruct((R, D_MODEL), F32),
        compiler_params=pltpu.CompilerParams(
            dimension_semantics=("parallel",), vmem_limit_bytes=VMEM_LIMIT),
        name="out",
    )(y, x, nw, wz, w, fw)


def kernel(x, meta_tokens, norm_w, w_in, conv_w, hg_lb_logits, hg_norm_w, gdn_A_log,
           gdn_dt_bias, gdn_norm_w, w_out, final_norm_w):
    B, T, D = x.shape
    assert D == D_MODEL and norm_w.shape[0] == 1 and meta_tokens.shape == (N_META, D_MODEL)
    assert w_in.shape == (1, D_MODEL, 8 * WIDTH + 2 * HEADS)

    w = jnp.pad(w_in[0], ((0, 0), (0, AB_PAD - 2 * HEADS))).astype(BF16)
    pad = lambda v: jnp.zeros((1, AB_PAD), F32).at[0, :HEADS].set(v.astype(F32))
    alog, dtb = pad(gdn_A_log[0]), pad(gdn_dt_bias[0])
    nw = norm_w[0].reshape(1, D_MODEL)
    cw = conv_w[0]
    proj = functools.partial(_proj_call, nw=nw, w=w, cw=cw, lbl=hg_lb_logits, alog=alog,
                             dtb=dtb)

    zero_tail = jnp.zeros((CONV_HIST, 3 * WIDTH), F32)
    m = proj(meta_tokens[None], tail=zero_tail, tm=N_META, chunk=N_META, emit_tail=True)
    _, mhk, mhv, mhg, _, mgk, mgv, _, mgcol, mgrow, tail = m
    sh0, sg0 = _meta_state_call(mhk, mhv, mhg, mgk, mgv, mgcol, mgrow)

    hq, hk, hv, hg, gq, gk, gv, gz, gcol, grow = proj(
        x, tail=tail, tm=PROJ_ROWS, chunk=CHUNK, emit_tail=False)
    y = _mixer_call(hq, hk, hv, hg, gq, gk, gv, gz, gcol, grow, sh0, sg0,
                    hg_norm_w[0].reshape(1, DH), gdn_norm_w[0].reshape(1, DH),
                    bb=MIXER_BATCH, tt=MIXER_ROWS, chunk=CHUNK)
    wz = w_in[0, :, 3 * WIDTH:4 * WIDTH].astype(BF16)
    out = _out_call(y.reshape(B * T, 2 * WIDTH), x.reshape(B * T, D_MODEL), nw, wz,
                    w_out[0].astype(BF16), final_norm_w.reshape(1, D_MODEL), tm=OUT_ROWS)
    return out.reshape(B, T, D_MODEL)
```

```python
import functools

import jax
import jax.numpy as jnp
from jax import lax
from jax.experimental import pallas as pl
from jax.experimental.pallas import tpu as pltpu

F32 = jnp.float32
BF16 = jnp.bfloat16

D_MODEL = 1024
N_META = 16
HEADS = 4
DH = 128
WIDTH = HEADS * DH
CONV_W = 4
EPS = 1e-6
AB_PAD = 128
IN_PAD = 8 * WIDTH + AB_PAD

CHUNK = 64
SUB = 16
SAFE_RANGE = 60.0
CONV_HIST = 8
CUM_ROWS = 256
ROW_BLOCK = 128
PROJ_ROWS = 1024
MIXER_BATCH = 8
MIXER_ROWS = 128
OUT_ROWS = 2048
OUT_ROW_BLOCK = 512
RECUR_GAP = 3
V7X_VMEM_BYTES = 64 * 1024 * 1024
VMEM_LIMIT = V7X_VMEM_BYTES * 7 // 8


def _dot(a, b):
    return jnp.dot(a, b, preferred_element_type=F32)


def _dot_nt(a, b):
    return lax.dot_general(a, b, (((1,), (1,)), ((), ())), preferred_element_type=F32)


def _dot_tn(a, b):
    return lax.dot_general(a, b, (((0,), (0,)), ((), ())), preferred_element_type=F32)


def _split3(x):
    x1 = x.astype(BF16)
    r = x - x1.astype(F32)
    x2 = r.astype(BF16)
    x3 = (r - x2.astype(F32)).astype(BF16)
    return x1, x2, x3


def _silu(x):
    return x * jax.nn.sigmoid(x)


def _softplus(x):
    return jnp.maximum(x, 0.0) + jnp.log1p(jnp.exp(-jnp.abs(x)))


def _chunk_cumsum(x, chunk):
    rows = x.shape[0]
    blk = min(rows, CUM_ROWS)
    r = lax.broadcasted_iota(jnp.int32, (blk, blk), 0)
    c = lax.broadcasted_iota(jnp.int32, (blk, blk), 1)
    tri = jnp.where((r // chunk == c // chunk) & (c <= r), 1.0, 0.0).astype(BF16)
    outs = []
    for i in range(rows // blk):
        parts = _split3(x[i * blk:(i + 1) * blk])
        outs.append(_dot(tri, parts[0]) + _dot(tri, parts[1]) + _dot(tri, parts[2]))
    return outs[0] if len(outs) == 1 else jnp.concatenate(outs, axis=0)


def _chunk_cumsum_lanes(x, chunk):
    n = x.shape[1]
    r = lax.broadcasted_iota(jnp.int32, (n, n), 0)
    c = lax.broadcasted_iota(jnp.int32, (n, n), 1)
    tri = jnp.where((r // chunk == c // chunk) & (r <= c), 1.0, 0.0).astype(BF16)
    parts = _split3(x)
    return _dot(parts[0], tri) + _dot(parts[1], tri) + _dot(parts[2], tri)


def _proj_kernel(x_ref, nw_ref, w_ref, cw_ref, lbl_ref, alog_ref, dtb_ref, tail_ref,
                 hq_ref, hk_ref, hv_ref, hg_ref, gq_ref, gk_ref, gv_ref, gz_ref,
                 gcol_ref, grow_ref, *rest, tm, chunk, emit_tail):
    if emit_tail:
        tail_out_ref, cbuf, logf_buf, ab_buf = rest
    else:
        cbuf, logf_buf, ab_buf = rest
    j = pl.program_id(1)

    @pl.when(j == 0)
    def _():
        cbuf[0:CONV_HIST, :] = tail_ref[...]

    lbl = lbl_ref[...]
    e = jnp.exp(lbl - jnp.max(lbl, axis=0, keepdims=True))
    lb = e[0:1] / jnp.sum(e, axis=0, keepdims=True)

    rb = min(tm, ROW_BLOCK)
    for r0 in range(0, tm, rb):
        rows = slice(r0, r0 + rb)
        x = x_ref[0, rows, :]
        ms = jnp.mean(x * x, axis=-1, keepdims=True)
        u = (x * lax.rsqrt(ms + EPS) * nw_ref[...]).astype(BF16)

        def proj(a, b, u=u):
            return _dot(u, w_ref[:, a:b])

        cbuf[CONV_HIST + r0:CONV_HIST + r0 + rb, :] = proj(4 * WIDTH, 7 * WIDTH)
        xe = cbuf[r0:r0 + CONV_HIST + rb, :]
        acc = cw_ref[0:1, :] * xe
        for t in range(1, CONV_W):
            acc = pltpu.roll(acc, 1, 0) + cw_ref[t:t + 1, :] * xe
        qkv = _silu(acc[CONV_HIST:CONV_HIST + rb])
        for h in range(HEADS):
            q = qkv[:, h * DH:(h + 1) * DH]
            k = qkv[:, WIDTH + h * DH:WIDTH + (h + 1) * DH]
            q = q * lax.rsqrt(jnp.sum(q * q, axis=-1, keepdims=True) + EPS) * (DH ** -0.5)
            k = k * lax.rsqrt(jnp.sum(k * k, axis=-1, keepdims=True) + EPS)
            gq_ref[0, rows, h * DH:(h + 1) * DH] = q.astype(BF16)
            gk_ref[0, rows, h * DH:(h + 1) * DH] = k.astype(BF16)
        gv_ref[0, rows, :] = qkv[:, 2 * WIDTH:3 * WIDTH].astype(BF16)
        gz_ref[0, rows, :] = _silu(proj(7 * WIDTH, 8 * WIDTH)).astype(BF16)

        hq_ref[0, rows, :] = _silu(proj(0, WIDTH)).astype(BF16)
        f = lb + (1.0 - lb) * jax.nn.sigmoid(proj(WIDTH, 2 * WIDTH))
        hk_ref[0, rows, :] = (1.0 - f).astype(BF16)
        logf_buf[rows, :] = jnp.log(f)
        hv_ref[0, rows, :] = proj(2 * WIDTH, 3 * WIDTH).astype(BF16)

        ab_buf[rows, :] = proj(8 * WIDTH, 8 * WIDTH + AB_PAD)

    for r0 in range(0, tm, rb):
        rows = slice(r0, r0 + rb)
        hg_ref[0, rows, :] = _chunk_cumsum(logf_buf[rows, :], chunk)
        p = ab_buf[rows, :]
        lane = lax.broadcasted_iota(jnp.int32, p.shape, 1)
        g = jnp.where(lane < HEADS, -jnp.exp(alog_ref[...]) * _softplus(p + dtb_ref[...]), 0.0)
        gam_rows = _chunk_cumsum_lanes(g.T[0:2 * HEADS, :], chunk)
        for c in range(rb // chunk):
            grow_ref[0, r0 // chunk + c] = gam_rows[:, c * chunk:(c + 1) * chunk]
        gam = jnp.concatenate(
            [gam_rows, jnp.zeros((AB_PAD - 2 * HEADS, rb), F32)], axis=0).T
        gcol_ref[0, rows, :] = jnp.where(
            lane < HEADS, gam, jnp.where(lane < 2 * HEADS, jax.nn.sigmoid(p), 0.0))

    hist = cbuf[tm:tm + CONV_HIST, :]
    cbuf[0:CONV_HIST, :] = hist
    if emit_tail:
        tail_out_ref[...] = hist


def _proj_call(x, nw, w, cw, lbl, alog, dtb, tail, *, tm, chunk, emit_tail):
    B, T, _ = x.shape
    assert T % tm == 0 and tm % min(tm, ROW_BLOCK) == 0 and min(tm, ROW_BLOCK) % chunk == 0
    nt = T // tm
    tok = lambda width, dt: jax.ShapeDtypeStruct((B, T, width), dt)
    tok_spec = lambda width: pl.BlockSpec((1, tm, width), lambda b, j: (b, j, 0))
    const = lambda shape: pl.BlockSpec(shape, lambda b, j: (0,) * len(shape))
    out_shape = [tok(WIDTH, BF16), tok(WIDTH, BF16), tok(WIDTH, BF16), tok(WIDTH, F32),
                 tok(WIDTH, BF16), tok(WIDTH, BF16), tok(WIDTH, BF16), tok(WIDTH, BF16),
                 tok(AB_PAD, F32),
                 jax.ShapeDtypeStruct((B, T // chunk, 2 * HEADS, chunk), F32)]
    out_specs = [tok_spec(WIDTH), tok_spec(WIDTH), tok_spec(WIDTH), tok_spec(WIDTH),
                 tok_spec(WIDTH), tok_spec(WIDTH), tok_spec(WIDTH), tok_spec(WIDTH),
                 tok_spec(AB_PAD),
                 pl.BlockSpec((1, tm // chunk, 2 * HEADS, chunk), lambda b, j: (b, j, 0, 0))]
    if emit_tail:
        out_shape.append(jax.ShapeDtypeStruct((CONV_HIST, 3 * WIDTH), F32))
        out_specs.append(const((CONV_HIST, 3 * WIDTH)))
    return pl.pallas_call(
        functools.partial(_proj_kernel, tm=tm, chunk=chunk, emit_tail=emit_tail),
        grid=(B, nt),
        in_specs=[tok_spec(D_MODEL), const((1, D_MODEL)),
                  pl.BlockSpec((D_MODEL, IN_PAD), lambda b, j: (0, 0),
                               pipeline_mode=pl.Buffered(1)),
                  const((CONV_W, 3 * WIDTH)), const(lbl.shape), const((1, AB_PAD)),
                  const((1, AB_PAD)), const((CONV_HIST, 3 * WIDTH))],
        out_specs=out_specs,
        out_shape=out_shape,
        scratch_shapes=[pltpu.VMEM((tm + CONV_HIST, 3 * WIDTH), F32),
                        pltpu.VMEM((tm, WIDTH), F32), pltpu.VMEM((tm, AB_PAD), F32)],
        compiler_params=pltpu.CompilerParams(
            dimension_semantics=("parallel", "arbitrary"), vmem_limit_bytes=VMEM_LIMIT),
        name="proj_meta" if emit_tail else "proj",
    )(x, nw, w, cw, lbl, alog, dtb, tail)


def _run_interleaved(*gens):
    live = list(gens)
    while live:
        for g in list(live):
            try:
                next(g)
            except StopIteration:
                live.remove(g)


def _scaled(x, decay):
    return x * decay.astype(BF16)


def _spread(gen, gap):
    for _ in gen:
        yield
        for _ in range(gap):
            yield


def _hgrn2_intra_stages(q, k, v, G, out, *, bounded):
    n = len(q)
    C = q[0].shape[0]
    row = lax.broadcasted_iota(jnp.int32, (C, C), 0)
    col = lax.broadcasted_iota(jnp.int32, (C, C), 1)
    if bounded:
        g_mid = [G[i][C // 2 - 1:C // 2, :] for i in range(n)]
        p = [_dot_nt(_scaled(q[i], jnp.exp(G[i] - g_mid[i])),
                     _scaled(k[i], jnp.exp(g_mid[i] - G[i]))) for i in range(n)]
        yield
        out["o"] = [_dot(jnp.where(row >= col, p[i], 0.0).astype(BF16), v[i]) for i in range(n)]
        return
    nb = C // SUB
    qf = [q[i].astype(F32) for i in range(n)]
    kf = [k[i].astype(F32) for i in range(n)]
    a = [jnp.zeros((C, C), F32) for _ in range(n)]
    g_end = [[G[i][(j + 1) * SUB - 1:(j + 1) * SUB, :] for j in range(nb)] for i in range(n)]
    if nb > 1:
        p = []
        for i in range(n):
            g_ref = jnp.concatenate(
                [jnp.broadcast_to(g_end[i][j], (SUB, DH)) for j in range(nb)], axis=0)
            kp = (kf[i] * jnp.exp(g_ref - G[i])).astype(BF16)
            qp = jnp.concatenate(
                [qf[i] * jnp.exp(jnp.minimum(G[i] - g_end[i][j], 0.0)) for j in range(nb - 1)],
                axis=0).astype(BF16)
            p.append(_dot_nt(qp, kp))
        yield
        for i in range(n):
            for j in range(nb - 1):
                m = (col // SUB == j) & (row // SUB > j)
                a[i] = jnp.where(m, p[i][j * C:(j + 1) * C], a[i])
    rowb = lax.broadcasted_iota(jnp.int32, (SUB, C), 0)
    colb = lax.broadcasted_iota(jnp.int32, (SUB, C), 1)
    o = []
    for i in range(n):
        for b in range(nb):
            gb = G[i][b * SUB:(b + 1) * SUB]
            qb = qf[i][b * SUB:(b + 1) * SUB]
            kb = kf[i][b * SUB:(b + 1) * SUB]
            d = jnp.zeros((SUB, C), F32)
            for s in range(SUB):
                es = jnp.exp(jnp.minimum(gb - gb[s:s + 1], 0.0))
                cs = jnp.sum(qb * es * kb[s:s + 1], axis=-1, keepdims=True)
                d = jnp.where((colb == b * SUB + s) & (rowb >= s), cs, d)
            if nb == 1:
                a[i] = d
            else:
                dfull = jnp.concatenate(
                    [d if r == b else jnp.zeros((SUB, C), F32) for r in range(nb)], axis=0)
                a[i] = jnp.where((row // SUB == b) & (col // SUB == b), dfull, a[i])
        o.append(_dot(a[i].astype(BF16), v[i]))
    out["o"] = o


def _hgrn2_state_stages(q, k, v, G, st, o_intra, out):
    n = len(k)
    C = k[0].shape[0]
    g_last = [G[i][C - 1:C, :] for i in range(n)]
    kk = [_scaled(k[i], jnp.exp(g_last[i] - G[i])) for i in range(n)]
    st_new = [_dot_tn(v[i], kk[i]) for i in range(n)]
    if st is not None:
        st_new = [st_new[i] + st[i] * jnp.exp(g_last[i]) for i in range(n)]
    out["s"] = st_new
    yield
    if o_intra is not None:
        out["o"] = [o_intra[i] + _dot_nt(_scaled(q[i], jnp.exp(G[i])), st[i].astype(BF16))
                    for i in range(n)]


def _gdn_prep_stages(q, k, gam_c, gam_r, beta_c, out, *, need_o):
    n = len(k)
    C = k[0].shape[0]
    rng = range(n)
    row = lax.broadcasted_iota(jnp.int32, (C, C), 0)
    col = lax.broadcasted_iota(jnp.int32, (C, C), 1)
    if need_o:
        pk = [_dot_nt(jnp.concatenate([q[i], k[i]], axis=0), k[i]) for i in rng]
        qk, kk = [x[:C] for x in pk], [x[C:] for x in pk]
    else:
        kk = [_dot_nt(k[i], k[i]) for i in rng]
    yield
    dec = [jnp.exp(jnp.minimum(gam_c[i] - gam_r[i], 0.0)) for i in rng]
    if need_o:
        out["qkd"] = [jnp.where(row >= col, qk[i] * dec[i], 0.0).astype(BF16) for i in rng]
    a = [jnp.where(row > col, beta_c[i] * kk[i] * dec[i], 0.0) for i in rng]
    eye = jnp.where(row == col, 1.0, 0.0)
    t = [eye - jnp.where(row // 2 == col // 2, a[i], 0.0) for i in rng]
    bs = 2
    while bs < C:
        lm = (row // (2 * bs) == col // (2 * bs)) & (row // bs > col // bs)
        tb = [t[i].astype(BF16) for i in rng]
        tl = [_dot(tb[i], jnp.where(lm, a[i], 0.0).astype(BF16)) for i in rng]
        yield
        t = [t[i] - _dot(tl[i].astype(BF16), tb[i]) for i in rng]
        yield
        bs *= 2
    out["t"] = [t[i].astype(BF16) for i in rng]


def _gdn_state_stages(q, k, v, gam_c, beta_c, t, qkd, s, out):
    n = len(k)
    C = k[0].shape[0]
    rng = range(n)
    vf = [v[i].astype(F32) for i in rng]
    qs = None
    if s is not None:
        egam = [jnp.exp(gam_c[i]) for i in rng]
        sb = [s[i].astype(BF16) for i in rng]
        if qkd is not None:
            qks = [_dot(jnp.concatenate([_scaled(q[i], egam[i]), _scaled(k[i], egam[i])],
                                        axis=0), sb[i]) for i in rng]
            qs, ks = [x[:C] for x in qks], [x[C:] for x in qks]
        else:
            ks = [_dot(_scaled(k[i], egam[i]), sb[i]) for i in rng]
        yield
        rhs = [beta_c[i] * (vf[i] - ks[i]) for i in rng]
    else:
        rhs = [beta_c[i] * vf[i] for i in rng]
    ub = [_dot(t[i], rhs[i].astype(BF16)).astype(BF16) for i in rng]
    yield
    g_last = [gam_c[i][C - 1:C, :] for i in rng]
    s_new = [_dot_tn(_scaled(k[i], jnp.exp(g_last[i] - gam_c[i])), ub[i]) for i in rng]
    if s is not None:
        s_new = [s_new[i] + s[i] * jnp.exp(g_last[i]) for i in rng]
    out["s"] = s_new
    if qkd is not None:
        o = [_dot(qkd[i], ub[i]) for i in rng]
        out["o"] = o if qs is None else [o[i] + qs[i] for i in rng]


def _head_norm(o, w):
    return o * lax.rsqrt(jnp.mean(o * o, axis=-1, keepdims=True) + EPS) * w


def _mixer_kernel(hq_ref, hk_ref, hv_ref, hg_ref, gq_ref, gk_ref, gv_ref, gz_ref,
                  gcol_ref, grow_ref, sh0_ref, sg0_ref, hw_ref, gw_ref,
                  y_ref, sh_ref, sg_ref, t_buf, qkd_buf, oi_buf, bounded_ref, *, bb, tt, chunk):
    j = pl.program_id(1)
    nchunks = tt // chunk

    @pl.when(j == 0)
    def _():
        for b in range(bb):
            sh_ref[b] = sh0_ref[...]
            sg_ref[b] = sg0_ref[...]

    for c in range(nchunks):
        span = None
        for b in range(bb):
            g_mid = hg_ref[b, c * chunk + chunk // 2 - 1:c * chunk + chunk // 2, :]
            g_last = hg_ref[b, (c + 1) * chunk - 1:(c + 1) * chunk, :]
            s = jnp.maximum(-g_mid, g_mid - g_last)
            span = s if span is None else jnp.maximum(span, s)
        bounded_ref[c] = jnp.where(jnp.max(span) <= SAFE_RANGE, 1, 0)

    hw = hw_ref[...]
    gw = gw_ref[...]
    probs = [(b, h) for b in range(bb) for h in range(HEADS)]
    hs = lambda h: slice(h * DH, (h + 1) * DH)

    def chunk_rows(c):
        return pl.ds(pl.multiple_of(c * chunk, chunk), chunk)

    def prep_stages(c, pout, hout, bounded):
        rows = chunk_rows(c)
        ld = lambda ref: [ref[b, rows, hs(h)] for b, h in probs]
        gcol = [gcol_ref[b, rows, :] for b in range(bb)]
        grow = [grow_ref[b, c] for b in range(bb)]
        return (_gdn_prep_stages(ld(gq_ref), ld(gk_ref),
                                 [gcol[b][:, h:h + 1] for b, h in probs],
                                 [grow[b][h:h + 1, :] for b, h in probs],
                                 [gcol[b][:, HEADS + h:HEADS + h + 1] for b, h in probs],
                                 pout, need_o=True),
                _hgrn2_intra_stages(ld(hq_ref), ld(hk_ref), ld(hv_ref), ld(hg_ref), hout,
                                    bounded=bounded))

    def store_prep(c, pout, hout):
        for i in range(len(probs)):
            t_buf[c, i] = pout["t"][i]
            qkd_buf[c, i] = pout["qkd"][i]
            oi_buf[c, i] = hout["o"][i]

    def recur_stages(c, gout, hout):
        rows = chunk_rows(c)
        ld = lambda ref: [ref[b, rows, hs(h)] for b, h in probs]
        gcol = [gcol_ref[b, rows, :] for b in range(bb)]
        n = len(probs)
        return (_gdn_state_stages(ld(gq_ref), ld(gk_ref), ld(gv_ref),
                                  [gcol[b][:, h:h + 1] for b, h in probs],
                                  [gcol[b][:, HEADS + h:HEADS + h + 1] for b, h in probs],
                                  [t_buf[c, i] for i in range(n)],
                                  [qkd_buf[c, i] for i in range(n)],
                                  [sg_ref[b, h] for b, h in probs], gout),
                _hgrn2_state_stages(ld(hq_ref), ld(hk_ref), ld(hv_ref), ld(hg_ref),
                                    [sh_ref[b, h] for b, h in probs],
                                    [oi_buf[c, i] for i in range(n)], hout))

    def store_recur(c, gout, hout):
        rows = chunk_rows(c)
        for i, (b, h) in enumerate(probs):
            sh_ref[b, h] = hout["s"][i]
            sg_ref[b, h] = gout["s"][i]
            y_ref[b, rows, hs(h)] = _head_norm(hout["o"][i], hw).astype(BF16)
            ys = slice(WIDTH + h * DH, WIDTH + (h + 1) * DH)
            y_ref[b, rows, ys] = (_head_norm(gout["o"][i], gw)
                                  * gz_ref[b, rows, hs(h)].astype(F32)).astype(BF16)

    def prep_only(c, bounded):
        pout, hout = {}, {}
        _run_interleaved(*prep_stages(c, pout, hout, bounded))
        store_prep(c, pout, hout)

    def recur_only(c):
        gout, hout = {}, {}
        _run_interleaved(*recur_stages(c, gout, hout))
        store_recur(c, gout, hout)

    def pipelined_step():
        prep_only(0, True)

        def body(c, carry):
            pout, phout, gout, hout = {}, {}, {}, {}
            gprep, hprep = prep_stages(c + 1, pout, phout, True)
            gstate, hstate = recur_stages(c, gout, hout)
            _run_interleaved(gprep, _spread(gstate, RECUR_GAP), _spread(hstate, RECUR_GAP),
                             hprep)
            store_prep(c + 1, pout, phout)
            store_recur(c, gout, hout)
            return carry

        lax.fori_loop(0, nchunks - 1, body, 0)
        recur_only(nchunks - 1)

    def plain_step():
        def body(c, carry):
            prep_only(c, False)
            recur_only(c)
            return carry

        lax.fori_loop(0, nchunks, body, 0)

    all_bounded = bounded_ref[0] == 1
    for c in range(1, nchunks):
        all_bounded = jnp.logical_and(all_bounded, bounded_ref[c] == 1)
    pl.when(all_bounded)(pipelined_step)
    pl.when(jnp.logical_not(all_bounded))(plain_step)


def _mixer_call(hq, hk, hv, hg, gq, gk, gv, gz, gcol, grow, sh0, sg0, hw, gw, *, bb, tt, chunk):
    B, T, _ = hq.shape
    assert B % bb == 0 and T % tt == 0 and tt % chunk == 0
    nchunks = tt // chunk
    tok_spec = lambda width: pl.BlockSpec((bb, tt, width), lambda i, j: (i, j, 0))
    const = lambda shape: pl.BlockSpec(shape, lambda i, j: (0,) * len(shape))
    return pl.pallas_call(
        functools.partial(_mixer_kernel, bb=bb, tt=tt, chunk=chunk),
        grid=(B // bb, T // tt),
        in_specs=[tok_spec(WIDTH), tok_spec(WIDTH), tok_spec(WIDTH), tok_spec(WIDTH),
                  tok_spec(WIDTH), tok_spec(WIDTH), tok_spec(WIDTH), tok_spec(WIDTH),
                  tok_spec(AB_PAD),
                  pl.BlockSpec((bb, nchunks, 2 * HEADS, chunk), lambda i, j: (i, j, 0, 0)),
                  const((HEADS, DH, DH)), const((HEADS, DH, DH)),
                  const((1, DH)), const((1, DH))],
        out_specs=tok_spec(2 * WIDTH),
        out_shape=jax.ShapeDtypeStruct((B, T, 2 * WIDTH), BF16),
        scratch_shapes=[pltpu.VMEM((bb, HEADS, DH, DH), F32),
                        pltpu.VMEM((bb, HEADS, DH, DH), F32),
                        pltpu.VMEM((nchunks, bb * HEADS, chunk, chunk), BF16),
                        pltpu.VMEM((nchunks, bb * HEADS, chunk, chunk), BF16),
                        pltpu.VMEM((nchunks, bb * HEADS, chunk, DH), F32),
                        pltpu.SMEM((nchunks,), jnp.int32)],
        compiler_params=pltpu.CompilerParams(
            dimension_semantics=("parallel", "arbitrary"), vmem_limit_bytes=VMEM_LIMIT),
        name="mixer",
    )(hq, hk, hv, hg, gq, gk, gv, gz, gcol, grow, sh0, sg0, hw, gw)


def _meta_state_kernel(hk_ref, hv_ref, hg_ref, gk_ref, gv_ref, gcol_ref, grow_ref,
                       sh_ref, sg_ref):
    gcol = gcol_ref[0]
    grow = grow_ref[0, 0]
    hs = lambda h: slice(h * DH, (h + 1) * DH)
    heads = range(HEADS)
    hk = [hk_ref[0, :, hs(h)] for h in heads]
    gk = [gk_ref[0, :, hs(h)] for h in heads]
    gam_c = [gcol[:, h:h + 1] for h in heads]
    beta_c = [gcol[:, HEADS + h:HEADS + h + 1] for h in heads]
    prep, hout, gout = {}, {}, {}
    _run_interleaved(
        _gdn_prep_stages(None, gk, gam_c, [grow[h:h + 1, :] for h in heads], beta_c, prep,
                         need_o=False),
        _hgrn2_state_stages(None, hk, [hv_ref[0, :, hs(h)] for h in heads],
                            [hg_ref[0, :, hs(h)] for h in heads], None, None, hout))
    _run_interleaved(
        _gdn_state_stages(None, gk, [gv_ref[0, :, hs(h)] for h in heads], gam_c, beta_c,
                          prep["t"], None, None, gout))
    for h in heads:
        sh_ref[h] = hout["s"][h]
        sg_ref[h] = gout["s"][h]


def _meta_state_call(hk, hv, hg, gk, gv, gcol, grow):
    state = jax.ShapeDtypeStruct((HEADS, DH, DH), F32)
    return pl.pallas_call(
        _meta_state_kernel,
        out_shape=[state, state],
        name="meta_state",
    )(hk, hv, hg, gk, gv, gcol, grow)


def _out_kernel(y_ref, x_ref, nw_ref, wz_ref, w_ref, fw_ref, o_ref):
    tm = x_ref.shape[0]
    rb = min(tm, OUT_ROW_BLOCK)
    blocks = [slice(r0, r0 + rb) for r0 in range(0, tm, rb)]
    gates = []
    for rows in blocks:
        x = x_ref[rows, :]
        ms = jnp.mean(x * x, axis=-1, keepdims=True)
        u = (x * lax.rsqrt(ms + EPS) * nw_ref[...]).astype(BF16)
        gates.append(_silu(_dot(u, wz_ref[...])))
    for rows, gate in zip(blocks, gates):
        yh = (y_ref[rows, 0:WIDTH].astype(F32) * gate).astype(BF16)
        h = (x_ref[rows, :] + _dot(yh, w_ref[0:WIDTH, :])
             + _dot(y_ref[rows, WIDTH:2 * WIDTH], w_ref[WIDTH:2 * WIDTH, :]))
        ms = jnp.mean(h * h, axis=-1, keepdims=True)
        o_ref[rows, :] = h * lax.rsqrt(ms + EPS) * fw_ref[...]


def _out_call(y, x, nw, wz, w, fw, *, tm):
    R = x.shape[0]
    assert R % tm == 0 and tm % min(tm, OUT_ROW_BLOCK) == 0
    row_spec = pl.BlockSpec((tm, D_MODEL), lambda i: (i, 0))
    const = lambda shape: pl.BlockSpec(shape, lambda i: (0, 0))
    return pl.pallas_call(
        _out_kernel,
        grid=(R // tm,),
        in_specs=[row_spec, row_spec, const((1, D_MODEL)), const((D_MODEL, WIDTH)),
                  const((2 * WIDTH, D_MODEL)), const((1, D_MODEL))],
        out_specs=row_spec,
        out_shape=jax.ShapeDtypeStruct((R, D_MODEL), F32),
        compiler_params=pltpu.CompilerParams(
            dimension_semantics=("parallel",), vmem_limit_bytes=VMEM_LIMIT),
        name="out",
    )(y, x, nw, wz, w, fw)


def kernel(x, meta_tokens, norm_w, w_in, conv_w, hg_lb_logits, hg_norm_w, gdn_A_log,
           gdn_dt_bias, gdn_norm_w, w_out, final_norm_w):
    B, T, D = x.shape
    assert D == D_MODEL and norm_w.shape[0] == 1 and meta_tokens.shape == (N_META, D_MODEL)
    assert w_in.shape == (1, D_MODEL, 8 * WIDTH + 2 * HEADS)

    w = jnp.pad(w_in[0], ((0, 0), (0, AB_PAD - 2 * HEADS))).astype(BF16)
    pad = lambda v: jnp.zeros((1, AB_PAD), F32).at[0, :HEADS].set(v.astype(F32))
    alog, dtb = pad(gdn_A_log[0]), pad(gdn_dt_bias[0])
    nw = norm_w[0].reshape(1, D_MODEL)
    cw = conv_w[0]
    proj = functools.partial(_proj_call, nw=nw, w=w, cw=cw, lbl=hg_lb_logits, alog=alog,
                             dtb=dtb)

    zero_tail = jnp.zeros((CONV_HIST, 3 * WIDTH), F32)
    m = proj(meta_tokens[None], tail=zero_tail, tm=N_META, chunk=N_META, emit_tail=True)
    _, mhk, mhv, mhg, _, mgk, mgv, _, mgcol, mgrow, tail = m
    sh0, sg0 = _meta_state_call(mhk, mhv, mhg, mgk, mgv, mgcol, mgrow)

    hq, hk, hv, hg, gq, gk, gv, gz, gcol, grow = proj(
        x, tail=tail, tm=PROJ_ROWS, chunk=CHUNK, emit_tail=False)
    y = _mixer_call(hq, hk, hv, hg, gq, gk, gv, gz, gcol, grow, sh0, sg0,
                    hg_norm_w[0].reshape(1, DH), gdn_norm_w[0].reshape(1, DH),
                    bb=MIXER_BATCH, tt=MIXER_ROWS, chunk=CHUNK)
    wz = w_in[0, :, 3 * WIDTH:4 * WIDTH].astype(BF16)
    out = _out_call(y.reshape(B * T, 2 * WIDTH), x.reshape(B * T, D_MODEL), nw, wz,
                    w_out[0].astype(BF16), final_norm_w.reshape(1, D_MODEL), tm=OUT_ROWS)
    return out.reshape(B, T, D_MODEL)
```

```python
import functools

import jax
import jax.numpy as jnp
from jax import lax
from jax.experimental import pallas as pl
from jax.experimental.pallas import tpu as pltpu

F32 = jnp.float32
BF16 = jnp.bfloat16

D_MODEL = 1024
N_META = 16
HEADS = 4
DH = 128
WIDTH = HEADS * DH
CONV_W = 4
EPS = 1e-6
AB_PAD = 128
IN_PAD = 8 * WIDTH + AB_PAD

CHUNK = 64
SUB = 16
SAFE_RANGE = 60.0
CONV_HIST = 8
CUM_ROWS = 256
ROW_BLOCK = 128
PROJ_ROWS = 1024
MIXER_BATCH = 8
MIXER_ROWS = 128
OUT_ROWS = 2048
OUT_ROW_BLOCK = 256
RECUR_GAP = 3
V7X_VMEM_BYTES = 64 * 1024 * 1024
VMEM_LIMIT = V7X_VMEM_BYTES * 7 // 8


def _dot(a, b):
    return jnp.dot(a, b, preferred_element_type=F32)


def _dot_nt(a, b):
    return lax.dot_general(a, b, (((1,), (1,)), ((), ())), preferred_element_type=F32)


def _dot_tn(a, b):
    return lax.dot_general(a, b, (((0,), (0,)), ((), ())), preferred_element_type=F32)


def _split3(x):
    x1 = x.astype(BF16)
    r = x - x1.astype(F32)
    x2 = r.astype(BF16)
    x3 = (r - x2.astype(F32)).astype(BF16)
    return x1, x2, x3


def _silu(x):
    return x * jax.nn.sigmoid(x)


def _softplus(x):
    return jnp.maximum(x, 0.0) + jnp.log1p(jnp.exp(-jnp.abs(x)))


def _chunk_cumsum(x, chunk):
    rows = x.shape[0]
    blk = min(rows, CUM_ROWS)
    r = lax.broadcasted_iota(jnp.int32, (blk, blk), 0)
    c = lax.broadcasted_iota(jnp.int32, (blk, blk), 1)
    tri = jnp.where((r // chunk == c // chunk) & (c <= r), 1.0, 0.0).astype(BF16)
    outs = []
    for i in range(rows // blk):
        parts = _split3(x[i * blk:(i + 1) * blk])
        outs.append(_dot(tri, parts[0]) + _dot(tri, parts[1]) + _dot(tri, parts[2]))
    return outs[0] if len(outs) == 1 else jnp.concatenate(outs, axis=0)


def _chunk_cumsum_lanes(x, chunk):
    n = x.shape[1]
    r = lax.broadcasted_iota(jnp.int32, (n, n), 0)
    c = lax.broadcasted_iota(jnp.int32, (n, n), 1)
    tri = jnp.where((r // chunk == c // chunk) & (r <= c), 1.0, 0.0).astype(BF16)
    parts = _split3(x)
    return _dot(parts[0], tri) + _dot(parts[1], tri) + _dot(parts[2], tri)


def _proj_kernel(x_ref, nw_ref, w_ref, cw_ref, lbl_ref, alog_ref, dtb_ref, tail_ref,
                 hq_ref, hk_ref, hv_ref, hg_ref, gq_ref, gk_ref, gv_ref, gz_ref,
                 gcol_ref, grow_ref, *rest, tm, chunk, emit_tail):
    if emit_tail:
        tail_out_ref, cbuf, logf_buf, ab_buf = rest
    else:
        cbuf, logf_buf, ab_buf = rest
    j = pl.program_id(1)

    @pl.when(j == 0)
    def _():
        cbuf[0:CONV_HIST, :] = tail_ref[...]

    lbl = lbl_ref[...]
    e = jnp.exp(lbl - jnp.max(lbl, axis=0, keepdims=True))
    lb = e[0:1] / jnp.sum(e, axis=0, keepdims=True)

    rb = min(tm, ROW_BLOCK)
    for r0 in range(0, tm, rb):
        rows = slice(r0, r0 + rb)
        x = x_ref[0, rows, :]
        ms = jnp.mean(x * x, axis=-1, keepdims=True)
        u = (x * lax.rsqrt(ms + EPS) * nw_ref[...]).astype(BF16)

        def proj(a, b, u=u):
            return _dot(u, w_ref[:, a:b])

        cbuf[CONV_HIST + r0:CONV_HIST + r0 + rb, :] = proj(4 * WIDTH, 7 * WIDTH)
        xe = cbuf[r0:r0 + CONV_HIST + rb, :]
        acc = cw_ref[0:1, :] * xe
        for t in range(1, CONV_W):
            acc = pltpu.roll(acc, 1, 0) + cw_ref[t:t + 1, :] * xe
        qkv = _silu(acc[CONV_HIST:CONV_HIST + rb])
        for h in range(HEADS):
            q = qkv[:, h * DH:(h + 1) * DH]
            k = qkv[:, WIDTH + h * DH:WIDTH + (h + 1) * DH]
            q = q * (lax.rsqrt(jnp.sum(q * q, axis=-1, keepdims=True) + EPS) * (DH ** -0.5))
            k = k * lax.rsqrt(jnp.sum(k * k, axis=-1, keepdims=True) + EPS)
            gq_ref[0, rows, h * DH:(h + 1) * DH] = q.astype(BF16)
            gk_ref[0, rows, h * DH:(h + 1) * DH] = k.astype(BF16)
        gv_ref[0, rows, :] = qkv[:, 2 * WIDTH:3 * WIDTH].astype(BF16)
        gz_ref[0, rows, :] = _silu(proj(7 * WIDTH, 8 * WIDTH)).astype(BF16)

        hq_ref[0, rows, :] = _silu(proj(0, WIDTH)).astype(BF16)
        f = lb + (1.0 - lb) * jax.nn.sigmoid(proj(WIDTH, 2 * WIDTH))
        hk_ref[0, rows, :] = (1.0 - f).astype(BF16)
        logf_buf[rows, :] = jnp.log(f)
        hv_ref[0, rows, :] = proj(2 * WIDTH, 3 * WIDTH).astype(BF16)

        ab_buf[rows, :] = proj(8 * WIDTH, 8 * WIDTH + AB_PAD)

    for r0 in range(0, tm, rb):
        rows = slice(r0, r0 + rb)
        hg_ref[0, rows, :] = _chunk_cumsum(logf_buf[rows, :], chunk)
        p = ab_buf[rows, :]
        lane = lax.broadcasted_iota(jnp.int32, p.shape, 1)
        g = jnp.where(lane < HEADS, -jnp.exp(alog_ref[...]) * _softplus(p + dtb_ref[...]), 0.0)
        gam_rows = _chunk_cumsum_lanes(g.T[0:2 * HEADS, :], chunk)
        for c in range(rb // chunk):
            grow_ref[0, r0 // chunk + c] = gam_rows[:, c * chunk:(c + 1) * chunk]
        gam = jnp.concatenate(
            [gam_rows, jnp.zeros((AB_PAD - 2 * HEADS, rb), F32)], axis=0).T
        gcol_ref[0, rows, :] = jnp.where(
            lane < HEADS, gam, jnp.where(lane < 2 * HEADS, jax.nn.sigmoid(p), 0.0))

    hist = cbuf[tm:tm + CONV_HIST, :]
    cbuf[0:CONV_HIST, :] = hist
    if emit_tail:
        tail_out_ref[...] = hist


def _proj_call(x, nw, w, cw, lbl, alog, dtb, tail, *, tm, chunk, emit_tail):
    B, T, _ = x.shape
    assert T % tm == 0 and tm % min(tm, ROW_BLOCK) == 0 and min(tm, ROW_BLOCK) % chunk == 0
    nt = T // tm
    tok = lambda width, dt: jax.ShapeDtypeStruct((B, T, width), dt)
    tok_spec = lambda width: pl.BlockSpec((1, tm, width), lambda b, j: (b, j, 0))
    const = lambda shape: pl.BlockSpec(shape, lambda b, j: (0,) * len(shape))
    out_shape = [tok(WIDTH, BF16), tok(WIDTH, BF16), tok(WIDTH, BF16), tok(WIDTH, F32),
                 tok(WIDTH, BF16), tok(WIDTH, BF16), tok(WIDTH, BF16), tok(WIDTH, BF16),
                 tok(AB_PAD, F32),
                 jax.ShapeDtypeStruct((B, T // chunk, 2 * HEADS, chunk), F32)]
    out_specs = [tok_spec(WIDTH), tok_spec(WIDTH), tok_spec(WIDTH), tok_spec(WIDTH),
                 tok_spec(WIDTH), tok_spec(WIDTH), tok_spec(WIDTH), tok_spec(WIDTH),
                 tok_spec(AB_PAD),
                 pl.BlockSpec((1, tm // chunk, 2 * HEADS, chunk), lambda b, j: (b, j, 0, 0))]
    if emit_tail:
        out_shape.append(jax.ShapeDtypeStruct((CONV_HIST, 3 * WIDTH), F32))
        out_specs.append(const((CONV_HIST, 3 * WIDTH)))
    return pl.pallas_call(
        functools.partial(_proj_kernel, tm=tm, chunk=chunk, emit_tail=emit_tail),
        grid=(B, nt),
        in_specs=[tok_spec(D_MODEL), const((1, D_MODEL)),
                  pl.BlockSpec((D_MODEL, IN_PAD), lambda b, j: (0, 0),
                               pipeline_mode=pl.Buffered(1)),
                  const((CONV_W, 3 * WIDTH)), const(lbl.shape), const((1, AB_PAD)),
                  const((1, AB_PAD)), const((CONV_HIST, 3 * WIDTH))],
        out_specs=out_specs,
        out_shape=out_shape,
        scratch_shapes=[pltpu.VMEM((tm + CONV_HIST, 3 * WIDTH), F32),
                        pltpu.VMEM((tm, WIDTH), F32), pltpu.VMEM((tm, AB_PAD), F32)],
        compiler_params=pltpu.CompilerParams(
            dimension_semantics=("parallel", "arbitrary"), vmem_limit_bytes=VMEM_LIMIT),
        name="proj_meta" if emit_tail else "proj",
    )(x, nw, w, cw, lbl, alog, dtb, tail)


def _run_interleaved(*gens):
    live = list(gens)
    while live:
        for g in list(live):
            try:
                next(g)
            except StopIteration:
                live.remove(g)


def _scaled(x, decay):
    return x * decay.astype(BF16)


def _spread(gen, gap):
    for _ in gen:
        yield
        for _ in range(gap):
            yield


def _hgrn2_intra_stages(q, k, v, G, out, *, bounded):
    n = len(q)
    C = q[0].shape[0]
    row = lax.broadcasted_iota(jnp.int32, (C, C), 0)
    col = lax.broadcasted_iota(jnp.int32, (C, C), 1)
    if bounded:
        g_mid = [G[i][C // 2 - 1:C // 2, :] for i in range(n)]
        p = [_dot_nt(_scaled(q[i], jnp.exp(G[i] - g_mid[i])),
                     _scaled(k[i], jnp.exp(g_mid[i] - G[i]))) for i in range(n)]
        yield
        out["o"] = [_dot(jnp.where(row >= col, p[i], 0.0).astype(BF16), v[i]) for i in range(n)]
        return
    nb = C // SUB
    qf = [q[i].astype(F32) for i in range(n)]
    kf = [k[i].astype(F32) for i in range(n)]
    a = [jnp.zeros((C, C), F32) for _ in range(n)]
    g_end = [[G[i][(j + 1) * SUB - 1:(j + 1) * SUB, :] for j in range(nb)] for i in range(n)]
    if nb > 1:
        p = []
        for i in range(n):
            g_ref = jnp.concatenate(
                [jnp.broadcast_to(g_end[i][j], (SUB, DH)) for j in range(nb)], axis=0)
            kp = (kf[i] * jnp.exp(g_ref - G[i])).astype(BF16)
            qp = jnp.concatenate(
                [qf[i] * jnp.exp(jnp.minimum(G[i] - g_end[i][j], 0.0)) for j in range(nb - 1)],
                axis=0).astype(BF16)
            p.append(_dot_nt(qp, kp))
        yield
        for i in range(n):
            for j in range(nb - 1):
                m = (col // SUB == j) & (row // SUB > j)
                a[i] = jnp.where(m, p[i][j * C:(j + 1) * C], a[i])
    rowb = lax.broadcasted_iota(jnp.int32, (SUB, C), 0)
    colb = lax.broadcasted_iota(jnp.int32, (SUB, C), 1)
    o = []
    for i in range(n):
        for b in range(nb):
            gb = G[i][b * SUB:(b + 1) * SUB]
            qb = qf[i][b * SUB:(b + 1) * SUB]
            kb = kf[i][b * SUB:(b + 1) * SUB]
            d = jnp.zeros((SUB, C), F32)
            for s in range(SUB):
                es = jnp.exp(jnp.minimum(gb - gb[s:s + 1], 0.0))
                cs = jnp.sum(qb * es * kb[s:s + 1], axis=-1, keepdims=True)
                d = jnp.where((colb == b * SUB + s) & (rowb >= s), cs, d)
            if nb == 1:
                a[i] = d
            else:
                dfull = jnp.concatenate(
                    [d if r == b else jnp.zeros((SUB, C), F32) for r in range(nb)], axis=0)
                a[i] = jnp.where((row // SUB == b) & (col // SUB == b), dfull, a[i])
        o.append(_dot(a[i].astype(BF16), v[i]))
    out["o"] = o


def _hgrn2_state_stages(q, k, v, G, st, o_intra, out):
    n = len(k)
    C = k[0].shape[0]
    g_last = [G[i][C - 1:C, :] for i in range(n)]
    kk = [_scaled(k[i], jnp.exp(g_last[i] - G[i])) for i in range(n)]
    st_new = [_dot_tn(v[i], kk[i]) for i in range(n)]
    if st is not None:
        st_new = [st_new[i] + st[i] * jnp.exp(g_last[i]) for i in range(n)]
    out["s"] = st_new
    yield
    if o_intra is not None:
        out["o"] = [o_intra[i] + _dot_nt(_scaled(q[i], jnp.exp(G[i])), st[i].astype(BF16))
                    for i in range(n)]


def _gdn_prep_stages(q, k, gam_c, gam_r, beta_c, out, *, need_o):
    n = len(k)
    C = k[0].shape[0]
    rng = range(n)
    row = lax.broadcasted_iota(jnp.int32, (C, C), 0)
    col = lax.broadcasted_iota(jnp.int32, (C, C), 1)
    if need_o:
        pk = [_dot_nt(jnp.concatenate([q[i], k[i]], axis=0), k[i]) for i in rng]
        qk, kk = [x[:C] for x in pk], [x[C:] for x in pk]
    else:
        kk = [_dot_nt(k[i], k[i]) for i in rng]
    yield
    dec = [jnp.exp(jnp.minimum(gam_c[i] - gam_r[i], 0.0)) for i in rng]
    if need_o:
        out["qkd"] = [jnp.where(row >= col, qk[i] * dec[i], 0.0).astype(BF16) for i in rng]
    a = [jnp.where(row > col, beta_c[i] * kk[i] * dec[i], 0.0) for i in rng]
    eye = jnp.where(row == col, 1.0, 0.0)
    t = [eye - jnp.where(row // 2 == col // 2, a[i], 0.0) for i in rng]
    bs = 2
    while bs < C:
        lm = (row // (2 * bs) == col // (2 * bs)) & (row // bs > col // bs)
        tb = [t[i].astype(BF16) for i in rng]
        tl = [_dot(tb[i], jnp.where(lm, a[i], 0.0).astype(BF16)) for i in rng]
        yield
        t = [t[i] - _dot(tl[i].astype(BF16), tb[i]) for i in rng]
        yield
        bs *= 2
    out["t"] = [t[i].astype(BF16) for i in rng]


def _gdn_state_stages(q, k, v, gam_c, beta_c, t, qkd, s, out):
    n = len(k)
    C = k[0].shape[0]
    rng = range(n)
    vf = [v[i].astype(F32) for i in rng]
    qs = None
    if s is not None:
        egam = [jnp.exp(gam_c[i]) for i in rng]
        sb = [s[i].astype(BF16) for i in rng]
        if qkd is not None:
            qks = [_dot(jnp.concatenate([_scaled(q[i], egam[i]), _scaled(k[i], egam[i])],
                                        axis=0), sb[i]) for i in rng]
            qs, ks = [x[:C] for x in qks], [x[C:] for x in qks]
        else:
            ks = [_dot(_scaled(k[i], egam[i]), sb[i]) for i in rng]
        yield
        rhs = [beta_c[i] * (vf[i] - ks[i]) for i in rng]
    else:
        rhs = [beta_c[i] * vf[i] for i in rng]
    ub = [_dot(t[i], rhs[i].astype(BF16)).astype(BF16) for i in rng]
    yield
    g_last = [gam_c[i][C - 1:C, :] for i in rng]
    s_new = [_dot_tn(_scaled(k[i], jnp.exp(g_last[i] - gam_c[i])), ub[i]) for i in rng]
    if s is not None:
        s_new = [s_new[i] + s[i] * jnp.exp(g_last[i]) for i in rng]
    out["s"] = s_new
    if qkd is not None:
        o = [_dot(qkd[i], ub[i]) for i in rng]
        out["o"] = o if qs is None else [o[i] + qs[i] for i in rng]


def _head_norm(o, w):
    return o * lax.rsqrt(jnp.mean(o * o, axis=-1, keepdims=True) + EPS) * w


def _mixer_kernel(hq_ref, hk_ref, hv_ref, hg_ref, gq_ref, gk_ref, gv_ref, gz_ref,
                  gcol_ref, grow_ref, sh0_ref, sg0_ref, hw_ref, gw_ref,
                  y_ref, sh_ref, sg_ref, t_buf, qkd_buf, oi_buf, bounded_ref, *, bb, tt, chunk):
    j = pl.program_id(1)
    nchunks = tt // chunk

    @pl.when(j == 0)
    def _():
        for b in range(bb):
            sh_ref[b] = sh0_ref[...]
            sg_ref[b] = sg0_ref[...]

    for c in range(nchunks):
        span = None
        for b in range(bb):
            g_mid = hg_ref[b, c * chunk + chunk // 2 - 1:c * chunk + chunk // 2, :]
            g_last = hg_ref[b, (c + 1) * chunk - 1:(c + 1) * chunk, :]
            s = jnp.maximum(-g_mid, g_mid - g_last)
            span = s if span is None else jnp.maximum(span, s)
        bounded_ref[c] = jnp.where(jnp.max(span) <= SAFE_RANGE, 1, 0)

    hw = hw_ref[...]
    gw = gw_ref[...]
    probs = [(b, h) for b in range(bb) for h in range(HEADS)]
    hs = lambda h: slice(h * DH, (h + 1) * DH)

    def chunk_rows(c):
        return pl.ds(pl.multiple_of(c * chunk, chunk), chunk)

    def prep_stages(c, pout, hout, bounded):
        rows = chunk_rows(c)
        ld = lambda ref: [ref[b, rows, hs(h)] for b, h in probs]
        gcol = [gcol_ref[b, rows, :] for b in range(bb)]
        grow = [grow_ref[b, c] for b in range(bb)]
        return (_gdn_prep_stages(ld(gq_ref), ld(gk_ref),
                                 [gcol[b][:, h:h + 1] for b, h in probs],
                                 [grow[b][h:h + 1, :] for b, h in probs],
                                 [gcol[b][:, HEADS + h:HEADS + h + 1] for b, h in probs],
                                 pout, need_o=True),
                _hgrn2_intra_stages(ld(hq_ref), ld(hk_ref), ld(hv_ref), ld(hg_ref), hout,
                                    bounded=bounded))

    def store_prep(c, pout, hout):
        for i in range(len(probs)):
            t_buf[c, i] = pout["t"][i]
            qkd_buf[c, i] = pout["qkd"][i]
            oi_buf[c, i] = hout["o"][i]

    def recur_stages(c, gout, hout):
        rows = chunk_rows(c)
        ld = lambda ref: [ref[b, rows, hs(h)] for b, h in probs]
        gcol = [gcol_ref[b, rows, :] for b in range(bb)]
        n = len(probs)
        return (_gdn_state_stages(ld(gq_ref), ld(gk_ref), ld(gv_ref),
                                  [gcol[b][:, h:h + 1] for b, h in probs],
                                  [gcol[b][:, HEADS + h:HEADS + h + 1] for b, h in probs],
                                  [t_buf[c, i] for i in range(n)],
                                  [qkd_buf[c, i] for i in range(n)],
                                  [sg_ref[b, h] for b, h in probs], gout),
                _hgrn2_state_stages(ld(hq_ref), ld(hk_ref), ld(hv_ref), ld(hg_ref),
                                    [sh_ref[b, h] for b, h in probs],
                                    [oi_buf[c, i] for i in range(n)], hout))

    def store_recur(c, gout, hout):
        rows = chunk_rows(c)
        for i, (b, h) in enumerate(probs):
            sh_ref[b, h] = hout["s"][i]
            sg_ref[b, h] = gout["s"][i]
            y_ref[b, rows, hs(h)] = _head_norm(hout["o"][i], hw).astype(BF16)
            ys = slice(WIDTH + h * DH, WIDTH + (h + 1) * DH)
            y_ref[b, rows, ys] = (_head_norm(gout["o"][i], gw)
                                  * gz_ref[b, rows, hs(h)].astype(F32)).astype(BF16)

    def prep_only(c, bounded):
        pout, hout = {}, {}
        _run_interleaved(*prep_stages(c, pout, hout, bounded))
        store_prep(c, pout, hout)

    def recur_only(c):
        gout, hout = {}, {}
        _run_interleaved(*recur_stages(c, gout, hout))
        store_recur(c, gout, hout)

    def pipelined_step():
        prep_only(0, True)

        def body(c, carry):
            pout, phout, gout, hout = {}, {}, {}, {}
            gprep, hprep = prep_stages(c + 1, pout, phout, True)
            gstate, hstate = recur_stages(c, gout, hout)
            _run_interleaved(gprep, _spread(gstate, RECUR_GAP), _spread(hstate, RECUR_GAP),
                             hprep)
            store_prep(c + 1, pout, phout)
            store_recur(c, gout, hout)
            return carry

        lax.fori_loop(0, nchunks - 1, body, 0)
        recur_only(nchunks - 1)

    def plain_step():
        def body(c, carry):
            prep_only(c, False)
            recur_only(c)
            return carry

        lax.fori_loop(0, nchunks, body, 0)

    all_bounded = bounded_ref[0] == 1
    for c in range(1, nchunks):
        all_bounded = jnp.logical_and(all_bounded, bounded_ref[c] == 1)
    pl.when(all_bounded)(pipelined_step)
    pl.when(jnp.logical_not(all_bounded))(plain_step)


def _mixer_call(hq, hk, hv, hg, gq, gk, gv, gz, gcol, grow, sh0, sg0, hw, gw, *, bb, tt, chunk):
    B, T, _ = hq.shape
    assert B % bb == 0 and T % tt == 0 and tt % chunk == 0
    nchunks = tt // chunk
    tok_spec = lambda width: pl.BlockSpec((bb, tt, width), lambda i, j: (i, j, 0))
    const = lambda shape: pl.BlockSpec(shape, lambda i, j: (0,) * len(shape))
    return pl.pallas_call(
        functools.partial(_mixer_kernel, bb=bb, tt=tt, chunk=chunk),
        grid=(B // bb, T // tt),
        in_specs=[tok_spec(WIDTH), tok_spec(WIDTH), tok_spec(WIDTH), tok_spec(WIDTH),
                  tok_spec(WIDTH), tok_spec(WIDTH), tok_spec(WIDTH), tok_spec(WIDTH),
                  tok_spec(AB_PAD),
                  pl.BlockSpec((bb, nchunks, 2 * HEADS, chunk), lambda i, j: (i, j, 0, 0)),
                  const((HEADS, DH, DH)), const((HEADS, DH, DH)),
                  const((1, DH)), const((1, DH))],
        out_specs=tok_spec(2 * WIDTH),
        out_shape=jax.ShapeDtypeStruct((B, T, 2 * WIDTH), BF16),
        scratch_shapes=[pltpu.VMEM((bb, HEADS, DH, DH), F32),
                        pltpu.VMEM((bb, HEADS, DH, DH), F32),
                        pltpu.VMEM((nchunks, bb * HEADS, chunk, chunk), BF16),
                        pltpu.VMEM((nchunks, bb * HEADS, chunk, chunk), BF16),
                        pltpu.VMEM((nchunks, bb * HEADS, chunk, DH), F32),
                        pltpu.SMEM((nchunks,), jnp.int32)],
        compiler_params=pltpu.CompilerParams(
            dimension_semantics=("parallel", "arbitrary"), vmem_limit_bytes=VMEM_LIMIT),
        name="mixer",
    )(hq, hk, hv, hg, gq, gk, gv, gz, gcol, grow, sh0, sg0, hw, gw)


def _meta_state_kernel(hk_ref, hv_ref, hg_ref, gk_ref, gv_ref, gcol_ref, grow_ref,
                       sh_ref, sg_ref):
    gcol = gcol_ref[0]
    grow = grow_ref[0, 0]
    hs = lambda h: slice(h * DH, (h + 1) * DH)
    heads = range(HEADS)
    hk = [hk_ref[0, :, hs(h)] for h in heads]
    gk = [gk_ref[0, :, hs(h)] for h in heads]
    gam_c = [gcol[:, h:h + 1] for h in heads]
    beta_c = [gcol[:, HEADS + h:HEADS + h + 1] for h in heads]
    prep, hout, gout = {}, {}, {}
    _run_interleaved(
        _gdn_prep_stages(None, gk, gam_c, [grow[h:h + 1, :] for h in heads], beta_c, prep,
                         need_o=False),
        _hgrn2_state_stages(None, hk, [hv_ref[0, :, hs(h)] for h in heads],
                            [hg_ref[0, :, hs(h)] for h in heads], None, None, hout))
    _run_interleaved(
        _gdn_state_stages(None, gk, [gv_ref[0, :, hs(h)] for h in heads], gam_c, beta_c,
                          prep["t"], None, None, gout))
    for h in heads:
        sh_ref[h] = hout["s"][h]
        sg_ref[h] = gout["s"][h]


def _meta_state_call(hk, hv, hg, gk, gv, gcol, grow):
    state = jax.ShapeDtypeStruct((HEADS, DH, DH), F32)
    return pl.pallas_call(
        _meta_state_kernel,
        out_shape=[state, state],
        name="meta_state",
    )(hk, hv, hg, gk, gv, gcol, grow)


def _out_kernel(y_ref, x_ref, nw_ref, wz_ref, w_ref, fw_ref, o_ref):
    tm = x_ref.shape[0]
    rb = min(tm, OUT_ROW_BLOCK)
    blocks = [slice(r0, r0 + rb) for r0 in range(0, tm, rb)]
    gates = []
    for rows in blocks:
        x = x_ref[rows, :]
        ms = jnp.mean(x * x, axis=-1, keepdims=True)
        u = (x * lax.rsqrt(ms + EPS) * nw_ref[...]).astype(BF16)
        gates.append(_silu(_dot(u, wz_ref[...])))
    for rows, gate in zip(blocks, gates):
        yh = (y_ref[rows, 0:WIDTH].astype(F32) * gate).astype(BF16)
        h = (x_ref[rows, :] + _dot(yh, w_ref[0:WIDTH, :])
             + _dot(y_ref[rows, WIDTH:2 * WIDTH], w_ref[WIDTH:2 * WIDTH, :]))
        ms = jnp.mean(h * h, axis=-1, keepdims=True)
        o_ref[rows, :] = h * lax.rsqrt(ms + EPS) * fw_ref[...]


def _out_call(y, x, nw, wz, w, fw, *, tm):
    R = x.shape[0]
    assert R % tm == 0 and tm % min(tm, OUT_ROW_BLOCK) == 0
    row_spec = pl.BlockSpec((tm, D_MODEL), lambda i: (i, 0))
    const = lambda shape: pl.BlockSpec(shape, lambda i: (0, 0))
    return pl.pallas_call(
        _out_kernel,
        grid=(R // tm,),
        in_specs=[row_spec, row_spec, const((1, D_MODEL)), const((D_MODEL, WIDTH)),
                  const((2 * WIDTH, D_MODEL)), const((1, D_MODEL))],
        out_specs=row_spec,
        out_shape=jax.ShapeDtypeStruct((R, D_MODEL), F32),
        compiler_params=pltpu.CompilerParams(
            dimension_semantics=("parallel",), vmem_limit_bytes=VMEM_LIMIT),
        name="out",
    )(y, x, nw, wz, w, fw)


def kernel(x, meta_tokens, norm_w, w_in, conv_w, hg_lb_logits, hg_norm_w, gdn_A_log,
           gdn_dt_bias, gdn_norm_w, w_out, final_norm_w):
    B, T, D = x.shape
    assert D == D_MODEL and norm_w.shape[0] == 1 and meta_tokens.shape == (N_META, D_MODEL)
    assert w_in.shape == (1, D_MODEL, 8 * WIDTH + 2 * HEADS)

    w = jnp.pad(w_in[0], ((0, 0), (0, AB_PAD - 2 * HEADS))).astype(BF16)
    pad = lambda v: jnp.zeros((1, AB_PAD), F32).at[0, :HEADS].set(v.astype(F32))
    alog, dtb = pad(gdn_A_log[0]), pad(gdn_dt_bias[0])
    nw = norm_w[0].reshape(1, D_MODEL)
    cw = conv_w[0]
    proj = functools.partial(_proj_call, nw=nw, w=w, cw=cw, lbl=hg_lb_logits, alog=alog,
                             dtb=dtb)

    zero_tail = jnp.zeros((CONV_HIST, 3 * WIDTH), F32)
    m = proj(meta_tokens[None], tail=zero_tail, tm=N_META, chunk=N_META, emit_tail=True)
    _, mhk, mhv, mhg, _, mgk, mgv, _, mgcol, mgrow, tail = m
    sh0, sg0 = _meta_state_call(mhk, mhv, mhg, mgk, mgv, mgcol, mgrow)

    hq, hk, hv, hg, gq, gk, gv, gz, gcol, grow = proj(
        x, tail=tail, tm=PROJ_ROWS, chunk=CHUNK, emit_tail=False)
    y = _mixer_call(hq, hk, hv, hg, gq, gk, gv, gz, gcol, grow, sh0, sg0,
                    hg_norm_w[0].reshape(1, DH), gdn_norm_w[0].reshape(1, DH),
                    bb=MIXER_BATCH, tt=MIXER_ROWS, chunk=CHUNK)
    wz = w_in[0, :, 3 * WIDTH:4 * WIDTH].astype(BF16)
    out = _out_call(y.reshape(B * T, 2 * WIDTH), x.reshape(B * T, D_MODEL), nw, wz,
                    w_out[0].astype(BF16), final_norm_w.reshape(1, D_MODEL), tm=OUT_ROWS)
    return out.reshape(B, T, D_MODEL)
```
